```python
import math
import jax
import jax.numpy as jnp
from jax import lax
import numpy as np

D_MODEL = 1024
BATCH = 8
SEQ = 8192
DEPTH = 2
DEC_BATCH = 32
DEC_SEQ = 16
PAST_LEN = 4096

CHUNK = 64
N_BAND = 8
A_WIN = N_BAND * CHUNK
D_MIX = D_MODEL
GROUP_W = D_MIX // 4
H_A = 4
DH_A = GROUP_W // H_A
REL_CLIP = 128
H_B = 4
DH_B = GROUP_W // H_B
H_C = 4
P_C = GROUP_W // H_C
G_C = 2
N_C = 128
CONV_C = 4
XBC_W = GROUP_W + 2 * G_C * N_C
G_D = 4
DG_D = GROUP_W // G_D
SPAT = 128
D_FF = ((8 * D_MODEL // 3 + 127) // 128) * 128
CONV_F = 3
IN_SIZES = (GROUP_W, GROUP_W, GROUP_W, GROUP_W, GROUP_W, GROUP_W, GROUP_W, H_B, H_B, GROUP_W, XBC_W, H_C, GROUP_W, GROUP_W)
W_IN_COLS = sum(IN_SIZES)
EPS = 1e-6
NEG = -1e30

kernel_name = 'hybrid_chunk_streaming_encoder_step'


def _split_cols(a, sizes):
    out, start = [], 0
    for s in sizes:
        out.append(a[..., start:start + s])
        start += s
    return out


def _rmsnorm(x, g):
    xf = x.astype(jnp.float32)
    y = xf * lax.rsqrt(jnp.mean(jnp.square(xf), axis=-1, keepdims=True) + EPS)
    return (y * g.astype(jnp.float32)).astype(x.dtype)


def _layernorm(x, g):
    xf = x.astype(jnp.float32)
    mu = jnp.mean(xf, axis=-1, keepdims=True)
    var = jnp.mean(jnp.square(xf - mu), axis=-1, keepdims=True)
    return ((xf - mu) * lax.rsqrt(var + EPS) * g.astype(jnp.float32)).astype(x.dtype)


def _causal_dwconv(x, buf, w, b):
    t = x.shape[1]
    width = w.shape[0]
    xp = jnp.concatenate([buf.astype(x.dtype), x], axis=1)
    y = b + w[0] * xp[:, 0:t]
    for j in range(1, width):
        y = y + w[j] * xp[:, j:j + t]
    return y, xp[:, xp.shape[1] - (width - 1):]


def _to_chunks(a, nc, length):
    return jnp.moveaxis(a.reshape(a.shape[0], nc, length, *a.shape[2:]), 1, 0)


def _from_chunks(a):
    a = jnp.moveaxis(a, 0, 1)
    return a.reshape(a.shape[0], a.shape[1] * a.shape[2], *a.shape[3:])


def _band_attention(q, k, v, k_hist, v_hist, rel_bias):
    bsz, t, h, dh = q.shape
    p = k_hist.shape[1]
    nq = -(-t // CHUNK)
    tp = nq * CHUNK
    pad_new = ((0, 0), (0, tp - t), (0, 0), (0, 0))
    hist_pad = jnp.zeros((bsz, A_WIN - p, h, dh), q.dtype)
    k_all = jnp.concatenate([hist_pad, k_hist.astype(q.dtype), jnp.pad(k, pad_new)], axis=1)
    v_all = jnp.concatenate([hist_pad, v_hist.astype(q.dtype), jnp.pad(v, pad_new)], axis=1)
    slot = jnp.arange(A_WIN + tp)
    valid = (slot >= A_WIN - p) & (slot < A_WIN + t)
    qc = jnp.pad(q, pad_new).reshape(bsz, nq, CHUNK, h, dh)
    kc = k_all.reshape(bsz, nq + N_BAND, CHUNK, h, dh)
    vc = v_all.reshape(bsz, nq + N_BAND, CHUNK, h, dh)
    vm = valid.reshape(nq + N_BAND, CHUNK)
    k_band = jnp.concatenate([kc[:, j:j + nq] for j in range(N_BAND + 1)], axis=2)
    v_band = jnp.concatenate([vc[:, j:j + nq] for j in range(N_BAND + 1)], axis=2)
    m_band = jnp.concatenate([vm[j:j + nq] for j in range(N_BAND + 1)], axis=1)
    band = (N_BAND + 1) * CHUNK
    dist = A_WIN + jnp.arange(CHUNK)[:, None] - jnp.arange(band)[None, :]
    bias = rel_bias.astype(jnp.float32)[:, jnp.clip(dist, -REL_CLIP, REL_CLIP) + REL_CLIP]
    s = jnp.einsum('bnqhd,bnkhd->bnhqk', qc, k_band).astype(jnp.float32) * (dh ** -0.5) + bias
    s = jnp.where(m_band[None, :, None, None, :], s, NEG)
    pr = jax.nn.softmax(s, axis=-1).astype(v.dtype)
    o = jnp.einsum('bnhqk,bnkhd->bnqhd', pr, v_band)
    return o.reshape(bsz, tp, h * dh)[:, :t]


def _mlstm(q, k, v, ig, lf, c0, n0, m0):
    bsz, t, h, d = q.shape
    length = min(t, CHUNK)
    nc = t // length
    causal = jnp.tril(jnp.ones((length, length), dtype=bool))[None, :, :, None]

    def step(carry, inp):
        c, n, m = carry
        qc, kc, vc, ic, fc = inp
        b = jnp.cumsum(fc, axis=1)
        dmat = jnp.where(causal, b[:, :, None] - b[:, None] + ic[:, None], -jnp.inf)
        m_t = jnp.maximum(b + m[:, None], jnp.max(dmat, axis=2))
        wqk = jnp.exp(dmat - m_t[:, :, None]) * jnp.einsum('bthd,bshd->btsh', qc, kc)
        inter = jnp.exp(b + m[:, None] - m_t)
        num = inter[..., None] * jnp.einsum('bthd,bhde->bthe', qc, c) + jnp.einsum('btsh,bshe->bthe', wqk, vc)
        den = inter * jnp.einsum('bthd,bhd->bth', qc, n) + jnp.sum(wqk, axis=2)
        hout = num / jnp.maximum(jnp.abs(den), jnp.exp(-m_t))[..., None]
        m_last = m_t[:, -1]
        decay = jnp.exp(b[:, -1] + m - m_last)
        ws = jnp.exp(b[:, -1:] - b + ic - m_last[:, None])
        c = decay[..., None, None] * c + jnp.einsum('bsh,bshd,bshe->bhde', ws, kc, vc)
        n = decay[..., None] * n + jnp.einsum('bsh,bshd->bhd', ws, kc)
        return (c, n, m_last), hout

    f32 = jnp.float32
    inp = tuple(_to_chunks(a.astype(f32), nc, length) for a in (q, k, v, ig, lf))
    (c, n, m), hs = lax.scan(step, (c0.astype(f32), n0.astype(f32), m0.astype(f32)), inp)
    return _from_chunks(hs), c, n, m


def _ssd(x, dt, a, bm, cm, h0):
    t = x.shape[1]
    length = min(t, CHUNK)
    nc = t // length
    causal = jnp.tril(jnp.ones((length, length), dtype=bool))[None, :, :, None]

    def step(h, inp):
        xc, dtc, bc, cc = inp
        cs = jnp.cumsum(dtc * a, axis=1)
        dec = jnp.exp(jnp.where(causal, cs[:, :, None] - cs[:, None], -jnp.inf))
        cb = jnp.einsum('bthn,bshn->btsh', cc, bc)
        y = jnp.einsum('btsh,bshp->bthp', cb * dec * dtc[:, None], xc) + jnp.exp(cs)[..., None] * jnp.einsum('bthn,bhpn->bthp', cc, h)
        wl = jnp.exp(cs[:, -1:] - cs) * dtc
        h = jnp.exp(cs[:, -1])[..., None, None] * h + jnp.einsum('bsh,bshp,bshn->bhpn', wl, xc, bc)
        return h, y

    f32 = jnp.float32
    inp = tuple(_to_chunks(z.astype(f32), nc, length) for z in (x, dt, bm, cm))
    h, ys = lax.scan(step, h0.astype(f32), inp)
    return _from_chunks(ys), h


def _spatial_gate(u, v, w_s, b_s):
    bsz, t, _ = v.shape
    nk = -(-t // SPAT)
    tp = nk * SPAT
    vp = jnp.pad(v, ((0, 0), (0, tp - t), (0, 0))).reshape(bsz, nk, SPAT, G_D, DG_D)
    wm = w_s * jnp.tril(jnp.ones((SPAT, SPAT), w_s.dtype))
    f = jnp.einsum('gts,bnsgc->bntgc', wm, vp) + b_s.T[:, :, None]
    return u * f.reshape(bsz, tp, GROUP_W)[:, :t]


def _layer(x, c, k_hist, v_hist, b_c, b_n, b_m, c_ssm, c_conv, f_conv,
           w_ada, b_ada, g_mix, g_ffn, w_in, w_out, a_rel_bias,
           b_i_bias, b_f_bias, b_norm_g,
           c_conv_w, c_conv_b, c_dt_bias, c_a_log, c_d_skip, c_norm_g,
           d_norm_g, d_w_s, d_b_s,
           f_w_gate, f_w_up, f_conv_w, f_conv_b, f_w_down):
    bsz, t, _ = x.shape
    dt_x = x.dtype
    mod = jax.nn.silu(c) @ w_ada + b_ada
    sh1, sc1, gt1, sh2, sc2, gt2 = jnp.split(mod[:, None, :], 6, axis=-1)
    h = _rmsnorm(x, g_mix) * (1 + sc1) + sh1
    proj = h @ w_in
    qa, ka, va, qb, kb, vb, ob, ib, fb, zc, xbc, dtc, ud, vd = _split_cols(proj, IN_SIZES)

    qa = qa.reshape(bsz, t, H_A, DH_A)
    ka = ka.reshape(bsz, t, H_A, DH_A)
    va = va.reshape(bsz, t, H_A, DH_A)
    y_a = _band_attention(qa, ka, va, k_hist, v_hist, a_rel_bias).astype(dt_x)
    keep = min(A_WIN, t)
    new_k = ka[:, t - keep:]
    new_v = va[:, t - keep:]

    igate = ib.astype(jnp.float32) + b_i_bias
    logf = jax.nn.log_sigmoid(fb.astype(jnp.float32) + b_f_bias)
    hb, new_bc, new_bn, new_bm = _mlstm(qb.reshape(bsz, t, H_B, DH_B),
                                        kb.reshape(bsz, t, H_B, DH_B) * (DH_B ** -0.5),
                                        vb.reshape(bsz, t, H_B, DH_B), igate, logf, b_c, b_n, b_m)
    hb = _layernorm(hb, b_norm_g.reshape(H_B, DH_B)).reshape(bsz, t, GROUP_W)
    y_b = (jax.nn.sigmoid(ob) * hb).astype(dt_x)

    xbc, new_conv = _causal_dwconv(xbc, c_conv, c_conv_w, c_conv_b)
    xbc = jax.nn.silu(xbc)
    xs, bm, cm = _split_cols(xbc, (GROUP_W, G_C * N_C, G_C * N_C))
    rep = H_C // G_C
    bm = jnp.repeat(bm.reshape(bsz, t, G_C, N_C), rep, axis=2)
    cm = jnp.repeat(cm.reshape(bsz, t, G_C, N_C), rep, axis=2)
    dt = jax.nn.softplus(dtc.astype(jnp.float32) + c_dt_bias)
    a = -jnp.exp(c_a_log.astype(jnp.float32))
    xs = xs.reshape(bsz, t, H_C, P_C)
    yc, new_ssm = _ssd(xs, dt, a, bm, cm, c_ssm)
    yc = (yc + c_d_skip[:, None] * xs).reshape(bsz, t, GROUP_W)
    y_c = _rmsnorm((yc * jax.nn.silu(zc)).astype(dt_x), c_norm_g)

    ud = jax.nn.gelu(ud)
    vd = _layernorm(jax.nn.gelu(vd), d_norm_g)
    y_d = _spatial_gate(ud, vd, d_w_s, d_b_s).astype(dt_x)

    mix = jnp.concatenate([y_a, y_b, y_c, y_d], axis=-1) @ w_out
    x = x + gt1 * mix

    h2 = _rmsnorm(x, g_ffn) * (1 + sc2) + sh2
    g = h2 @ f_w_gate
    u = h2 @ f_w_up
    gconv, new_fconv = _causal_dwconv(g, f_conv, f_conv_w, f_conv_b)
    x = x + gt2 * ((jax.nn.silu(gconv) * u) @ f_w_down)
    return x, (new_k, new_v, new_bc, new_bn, new_bm, new_ssm, new_conv, new_fconv, vd)


def setup_inputs(seed: int = 0) -> dict:
    key = jax.random.key(seed)
    ks = iter(jax.random.split(key, 48))
    f32 = jnp.float32

    def nrm(shape, s=1.0):
        return s * jax.random.normal(next(ks), shape, f32)

    a_cache = min(A_WIN, PAST_LEN)
    dt0 = jnp.exp(jax.random.uniform(next(ks), (DEPTH, H_C), f32, math.log(1e-3), math.log(1e-1)))
    return {
        'x_prompt': nrm((BATCH, SEQ, D_MODEL)),
        'x_sample': nrm((DEC_BATCH, DEC_SEQ, D_MODEL)),
        'c_prompt': nrm((BATCH, D_MODEL)),
        'c_sample': nrm((DEC_BATCH, D_MODEL)),
        'cache_a_k': nrm((DEPTH, DEC_BATCH, a_cache, H_A, DH_A)),
        'cache_a_v': nrm((DEPTH, DEC_BATCH, a_cache, H_A, DH_A)),
        'state_b_c': nrm((DEPTH, DEC_BATCH, H_B, DH_B, DH_B), 0.1),
        'state_b_n': nrm((DEPTH, DEC_BATCH, H_B, DH_B), 0.1),
        'state_b_m': nrm((DEPTH, DEC_BATCH, H_B)),
        'state_c_ssm': nrm((DEPTH, DEC_BATCH, H_C, P_C, N_C), 0.1),
        'state_c_conv': nrm((DEPTH, DEC_BATCH, CONV_C - 1, XBC_W)),
        'state_ffn_conv': nrm((DEPTH, DEC_BATCH, CONV_F - 1, D_FF)),
        'w_ada': nrm((DEPTH, D_MODEL, 6 * D_MODEL), D_MODEL ** -0.5),
        'b_ada': nrm((DEPTH, 6 * D_MODEL), 0.02),
        'g_norm_mix': 1.0 + nrm((DEPTH, D_MODEL), 0.05),
        'g_norm_ffn': 1.0 + nrm((DEPTH, D_MODEL), 0.05),
        'w_in': nrm((DEPTH, D_MODEL, W_IN_COLS), D_MODEL ** -0.5),
        'w_out': nrm((DEPTH, D_MIX, D_MODEL), D_MIX ** -0.5),
        'a_rel_bias': nrm((DEPTH, H_A, 2 * REL_CLIP + 1), 0.1),
        'b_i_bias': nrm((DEPTH, H_B), 0.1),
        'b_f_bias': jnp.linspace(3.0, 6.0, H_B, dtype=f32)[None] + nrm((DEPTH, H_B), 0.1),
        'b_norm_g': 1.0 + nrm((DEPTH, GROUP_W), 0.05),
        'c_conv_w': nrm((DEPTH, CONV_C, XBC_W), CONV_C ** -0.5),
        'c_conv_b': nrm((DEPTH, XBC_W), 0.02),
        'c_dt_bias': dt0 + jnp.log(-jnp.expm1(-dt0)),
        'c_a_log': jnp.log(jax.random.uniform(next(ks), (DEPTH, H_C), f32, 1.0, 16.0)),
        'c_d_skip': 1.0 + nrm((DEPTH, H_C), 0.1),
        'c_norm_g': 1.0 + nrm((DEPTH, GROUP_W), 0.05),
        'd_norm_g': 1.0 + nrm((DEPTH, GROUP_W), 0.05),
        'd_w_s': nrm((DEPTH, G_D, SPAT, SPAT), 0.5 * SPAT ** -0.5),
        'd_b_s': 1.0 + nrm((DEPTH, G_D, SPAT), 0.1),
        'f_w_gate': nrm((DEPTH, D_MODEL, D_FF), D_MODEL ** -0.5),
        'f_w_up': nrm((DEPTH, D_MODEL, D_FF), D_MODEL ** -0.5),
        'f_conv_w': nrm((DEPTH, CONV_F, D_FF), CONV_F ** -0.5),
        'f_conv_b': nrm((DEPTH, D_FF), 0.02),
        'f_w_down': nrm((DEPTH, D_FF, D_MODEL), D_FF ** -0.5),
        'g_final': 1.0 + nrm((D_MODEL,), 0.05),
    }


def reference(x_prompt, x_sample, c_prompt, c_sample,
              cache_a_k, cache_a_v, state_b_c, state_b_n, state_b_m,
              state_c_ssm, state_c_conv, state_ffn_conv,
              w_ada, b_ada, g_norm_mix, g_norm_ffn, w_in, w_out, a_rel_bias,
              b_i_bias, b_f_bias, b_norm_g,
              c_conv_w, c_conv_b, c_dt_bias, c_a_log, c_d_skip, c_norm_g,
              d_norm_g, d_w_s, d_b_s,
              f_w_gate, f_w_up, f_conv_w, f_conv_b, f_w_down, g_final):
    f32 = jnp.float32
    bp = x_prompt.shape[0]
    xp = x_prompt
    xs = x_sample
    p_states = []
    s_states = []
    for l in range(DEPTH):
        params = (w_ada[l], b_ada[l], g_norm_mix[l], g_norm_ffn[l], w_in[l], w_out[l], a_rel_bias[l],
                  b_i_bias[l], b_f_bias[l], b_norm_g[l],
                  c_conv_w[l], c_conv_b[l], c_dt_bias[l], c_a_log[l], c_d_skip[l], c_norm_g[l],
                  d_norm_g[l], d_w_s[l], d_b_s[l],
                  f_w_gate[l], f_w_up[l], f_conv_w[l], f_conv_b[l], f_w_down[l])
        z_hist = jnp.zeros((bp, 0, H_A, DH_A), xp.dtype)
        xp, sp = _layer(xp, c_prompt, z_hist, z_hist,
                        jnp.zeros((bp, H_B, DH_B, DH_B), f32), jnp.zeros((bp, H_B, DH_B), f32),
                        jnp.zeros((bp, H_B), f32), jnp.zeros((bp, H_C, P_C, N_C), f32),
                        jnp.zeros((bp, CONV_C - 1, XBC_W), xp.dtype),
                        jnp.zeros((bp, CONV_F - 1, D_FF), xp.dtype), *params)
        xs, ss = _layer(xs, c_sample, cache_a_k[l], cache_a_v[l], state_b_c[l], state_b_n[l],
                        state_b_m[l], state_c_ssm[l], state_c_conv[l], state_ffn_conv[l], *params)
        p_states.append(sp)
        s_states.append(ss)
    y_prompt = _rmsnorm(xp, g_final)
    y_sample = _rmsnorm(xs, g_final)
    p_a_k = jnp.stack([s[0] for s in p_states])
    p_a_v = jnp.stack([s[1] for s in p_states])
    p_b_c = jnp.stack([s[2] for s in p_states])
    p_b_n = jnp.stack([s[3] for s in p_states])
    p_b_m = jnp.stack([s[4] for s in p_states])
    p_c_ssm = jnp.stack([s[5] for s in p_states])
    p_c_conv = jnp.stack([s[6] for s in p_states])
    p_ffn_conv = jnp.stack([s[7] for s in p_states])
    s_a_k = jnp.stack([s[0] for s in s_states])
    s_a_v = jnp.stack([s[1] for s in s_states])
    s_b_c = jnp.stack([s[2] for s in s_states])
    s_b_n = jnp.stack([s[3] for s in s_states])
    s_b_m = jnp.stack([s[4] for s in s_states])
    s_c_ssm = jnp.stack([s[5] for s in s_states])
    s_c_conv = jnp.stack([s[6] for s in s_states])
    s_ffn_conv = jnp.stack([s[7] for s in s_states])
    s_d_v = jnp.stack([s[8] for s in s_states])
    return (y_prompt, y_sample, p_a_k, p_a_v, p_b_c, p_b_n, p_b_m, p_c_ssm, p_c_conv, p_ffn_conv,
            s_a_k, s_a_v, s_b_c, s_b_n, s_b_m, s_c_ssm, s_c_conv, s_ffn_conv, s_d_v)
```

```python
import functools

import jax
import jax.numpy as jnp
from jax import lax
from jax.experimental import pallas as pl
from jax.experimental.pallas import tpu as pltpu

F32 = jnp.float32
BF16 = jnp.bfloat16

EPS = 1e-6
NEG = -1e30

CHUNK = 64
N_BAND = 8
A_WIN = N_BAND * CHUNK
REL_CLIP = 128
N_HEADS = 4
SPAT = 128
CONV_C = 4
CONV_F = 3
LANE = 128
SUBLANE = 8
VMEM_CAP = 64 * 1024 * 1024


def _vmem_limit(nbytes):
    return int(min(max(nbytes, 16 * 1024 * 1024), VMEM_CAP - 8 * 1024 * 1024))


def _params(sem, nbytes):
    return pltpu.CompilerParams(dimension_semantics=sem, vmem_limit_bytes=_vmem_limit(nbytes))


def _const_spec(shape):
    nd = len(shape)
    return pl.BlockSpec(shape, lambda *_: (0,) * nd, pipeline_mode=pl.Buffered(1))


def _iota(shape, dim):
    return lax.broadcasted_iota(jnp.int32, shape, dim)


def _split_dot(x, e):
    acc = None
    r = x
    for _ in range(3):
        hi = r.astype(BF16)
        d = jnp.dot(hi, e, preferred_element_type=F32)
        acc = d if acc is None else acc + d
        r = r - hi.astype(F32)
    return acc


def _split_dot_left(e, x):
    acc = None
    r = x
    for _ in range(3):
        hi = r.astype(BF16)
        d = jnp.dot(e, hi, preferred_element_type=F32)
        acc = d if acc is None else acc + d
        r = r - hi.astype(F32)
    return acc


def _dot_nt(a, b):
    return lax.dot_general(a, b, (((1,), (1,)), ((), ())), preferred_element_type=F32)


def _dot_tn(a, b):
    return lax.dot_general(a, b, (((0,), (0,)), ((), ())), preferred_element_type=F32)


def _head_selectors(width):
    dh = width // N_HEADS
    e = (_iota((width, LANE), 0) // dh == _iota((width, LANE), 1)).astype(BF16)
    et = (_iota((LANE, width), 1) // dh == _iota((LANE, width), 0)).astype(BF16)
    return e, et


def _head_lane_masks(rows, width):
    dh = width // N_HEADS
    lane = _iota((rows, width), 1)
    return [(lane >= h * dh) & (lane < (h + 1) * dh) for h in range(N_HEADS)]


def _tril(n):
    return _iota((n, n), 0) >= _iota((n, n), 1)


def _keep_heads(x):
    return jnp.where(_iota(x.shape, 1) < N_HEADS, x, 0.0)


def _pad_rows(x, rows):
    if x.shape[0] == rows:
        return x
    return jnp.concatenate([x, jnp.zeros((rows - x.shape[0], x.shape[1]), x.dtype)], axis=0)


def _rmsnorm_rows(x, g):
    return x * lax.rsqrt(jnp.mean(x * x, axis=-1, keepdims=True) + EPS) * g


def _ada_kernel(c_ref, w_ref, b_ref, o_ref):
    c = c_ref[...]
    h = (c * jax.nn.sigmoid(c)).astype(BF16)
    o_ref[0] = jnp.dot(h, w_ref[0].astype(BF16), preferred_element_type=F32) + b_ref[0]


def _ada(c_all, w_ada, b_ada):
    depth, d, n6 = w_ada.shape
    r = c_all.shape[0]
    tn = d
    return pl.pallas_call(
        _ada_kernel,
        out_shape=jax.ShapeDtypeStruct((depth, r, n6), F32),
        grid=(depth, n6 // tn),
        in_specs=[pl.BlockSpec((r, d), lambda l, j: (0, 0)),
                  pl.BlockSpec((1, d, tn), lambda l, j: (l, 0, j)),
                  pl.BlockSpec((1, 1, tn), lambda l, j: (l, 0, j))],
        out_specs=pl.BlockSpec((1, r, tn), lambda l, j: (l, 0, j)),
        name="ada_mod",
        compiler_params=_params(("parallel", "parallel"), 4 * (2 * d * tn * 4 + 2 * r * tn * 4 + r * d * 4)),
    )(c_all, w_ada, b_ada.reshape(depth, 1, n6))


def _bias_kernel(rb_ref, o_ref, *, lo, hi):
    nh, lq, lk = o_ref.shape
    idx = jnp.clip(A_WIN + _iota((lq, lk), 0) - _iota((lq, lk), 1), -REL_CLIP, REL_CLIP) + REL_CLIP
    for h in range(nh):
        def body(r, acc, h=h):
            return jnp.where(idx == r, rb_ref[h, r], acc)
        o_ref[h] = lax.fori_loop(lo, hi + 1, body, jnp.zeros((lq, lk), F32))


def _bias_table(rel_bias):
    nh = rel_bias.shape[0]
    band = (N_BAND + 1) * CHUNK
    lo = max(A_WIN - (band - 1), -REL_CLIP) + REL_CLIP
    hi = min(A_WIN + CHUNK - 1, REL_CLIP) + REL_CLIP
    return pl.pallas_call(
        functools.partial(_bias_kernel, lo=lo, hi=hi),
        out_shape=jax.ShapeDtypeStruct((nh, CHUNK, band), F32),
        in_specs=[pl.BlockSpec(memory_space=pltpu.SMEM)],
        out_specs=pl.BlockSpec(memory_space=pltpu.VMEM),
        name="rel_bias_table",
    )(rel_bias)


def _in_kernel(x_ref, mod_ref, g_ref, w_ref, *o_refs, col_starts):
    bb, tt, d = x_ref.shape
    x = x_ref[...]
    mod = mod_ref[...]
    h = _rmsnorm_rows(x, g_ref[...]) * (1.0 + mod[:, :, d:2 * d]) + mod[:, :, 0:d]
    hb = h.reshape(bb * tt, d).astype(BF16)
    for o_ref, (a, b) in zip(o_refs, col_starts):
        o_ref[...] = jnp.dot(hb, w_ref[:, a:b], preferred_element_type=F32).reshape(bb, tt, b - a)


def _in_proj(x, mod, g, w, widths, bb, tt):
    bsz, t, d = x.shape
    ncols = w.shape[1]
    starts, a = [], 0
    for wd in widths:
        starts.append((a, a + wd))
        a += wd
    m = bb * tt
    est = 2 * m * d * 4 + 2 * d * ncols * 2 + 2 * m * ncols * 4 + 3 * m * d * 4 + m * max(widths) * 4
    return pl.pallas_call(
        functools.partial(_in_kernel, col_starts=tuple(starts)),
        out_shape=[jax.ShapeDtypeStruct((bsz, t, wd), F32) for wd in widths],
        grid=(bsz // bb, t // tt),
        in_specs=[pl.BlockSpec((bb, tt, d), lambda i, j: (i, j, 0)),
                  pl.BlockSpec((bb, 1, mod.shape[2]), lambda i, j: (i, 0, 0)),
                  _const_spec((1, 1, d)),
                  _const_spec((d, ncols))],
        out_specs=[pl.BlockSpec((bb, tt, wd), lambda i, j: (i, j, 0)) for wd in widths],
        name="in_proj",
        compiler_params=_params(("parallel", "parallel"), est),
    )(x, mod, g.reshape(1, 1, d), w)


def _attend_chunk(qc, kb, vb, bias, kvalid, masks_q, masks_o):
    lq = qc.shape[0]
    qs = jnp.concatenate([jnp.where(mk, qc, 0.0) for mk in masks_q], axis=0).astype(BF16)
    s = _dot_nt(qs, kb) + bias
    if kvalid is not None:
        s = jnp.where(kvalid, s, NEG)
    e = jnp.exp(s - jnp.max(s, axis=-1, keepdims=True))
    o = jnp.dot(e.astype(BF16), vb, preferred_element_type=F32)
    o = o * (1.0 / jnp.sum(e, axis=-1, keepdims=True))
    out = jnp.where(masks_o[0], o[0:lq], 0.0)
    for h in range(1, N_HEADS):
        out = out + jnp.where(masks_o[h], o[h * lq:(h + 1) * lq], 0.0)
    return out


def _attn_kernel(q_ref, k_ref, v_ref, kh_ref, vh_ref, bias_ref, y_ref, kk_ref, vv_ref, *, hist_is_cache, t_valid):
    tq, w = q_ref.shape[1], q_ref.shape[2]
    tpad = kk_ref.shape[0] - A_WIN
    scale = (w // N_HEADS) ** -0.5
    kk_ref[0:A_WIN, :] = kh_ref[0].astype(BF16)
    vv_ref[0:A_WIN, :] = vh_ref[0].astype(BF16)
    kk_ref[A_WIN:A_WIN + tpad, :] = _pad_rows(k_ref[0], tpad).astype(BF16)
    vv_ref[A_WIN:A_WIN + tpad, :] = _pad_rows(v_ref[0], tpad).astype(BF16)
    masks = _head_lane_masks(CHUNK, w)
    bias = bias_ref[...]
    band = (N_BAND + 1) * CHUNK
    slot = _iota((1, band), 1)
    first = A_WIN if not hist_is_cache else 0
    hist_ok = jnp.logical_or(pl.program_id(1) > 0, hist_is_cache)
    q_all = _pad_rows(q_ref[0], tpad) * scale
    for i in range(tpad // CHUNK):
        base = i * CHUNK
        pos = slot + base
        kvalid = (pos < A_WIN + t_valid) & ((pos >= A_WIN) | hist_ok)
        out = _attend_chunk(q_all[base:base + CHUNK], kk_ref[base:base + band, :], vv_ref[base:base + band, :],
                            bias, kvalid, masks, masks)
        rows = min(CHUNK, tq - base)
        y_ref[0, base:base + rows, :] = out[0:rows]


def _attention(qkv, k_hist, v_hist, bias_tab, tq):
    bsz, t, w3 = qkv.shape
    w = w3 // 3
    hist_is_cache = k_hist is not None
    nt = t // tq
    tpad = -(-tq // CHUNK) * CHUNK
    band = (N_BAND + 1) * CHUNK
    if hist_is_cache:
        assert nt == 1
        hist_specs = [pl.BlockSpec((1, A_WIN, w), lambda b, j: (b, 0, 0))] * 2
        hist_args = (k_hist, v_hist)
    else:
        assert tq == A_WIN
        hist_specs = [pl.BlockSpec((1, tq, w), lambda b, j: (b, jnp.maximum(j - 1, 0), 1)),
                      pl.BlockSpec((1, tq, w), lambda b, j: (b, jnp.maximum(j - 1, 0), 2))]
        hist_args = (qkv, qkv)
    est = 2 * 5 * tpad * w * 4 + 2 * tq * w * 4 + 2 * (A_WIN + tpad) * w * 2 + 12 * N_HEADS * CHUNK * band * 4
    return pl.pallas_call(
        functools.partial(_attn_kernel, hist_is_cache=hist_is_cache, t_valid=tq if nt == 1 else tpad),
        out_shape=jax.ShapeDtypeStruct((bsz, t, w), F32),
        grid=(bsz, nt),
        in_specs=[pl.BlockSpec((1, tq, w), lambda b, j: (b, j, 0)),
                  pl.BlockSpec((1, tq, w), lambda b, j: (b, j, 1)),
                  pl.BlockSpec((1, tq, w), lambda b, j: (b, j, 2)),
                  *hist_specs,
                  _const_spec((N_HEADS * CHUNK, band))],
        out_specs=pl.BlockSpec((1, tq, w), lambda b, j: (b, j, 0)),
        scratch_shapes=[pltpu.VMEM((A_WIN + tpad, w), BF16), pltpu.VMEM((A_WIN + tpad, w), BF16)],
        name="band_attention",
        compiler_params=_params(("parallel", "parallel"), est),
    )(qkv, qkv, qkv, *hist_args, bias_tab.reshape(N_HEADS * CHUNK, band))


def _mlstm_chunk(q, k, v, og, gi, gf, normg, c_ref, n_ref, m_ref, n_valid):
    L, w = q.shape
    dh = w // N_HEADS
    e_sel, et_sel = _head_selectors(w)
    lmask = _head_lane_masks(L, w)
    tril = _tril(L)
    tril_b = tril.astype(BF16)
    lane_g = _iota((L, LANE), 1)

    lf = jax.nn.log_sigmoid(gf)
    b = _split_dot_left(tril_b, lf)
    u = gi - b
    u_t = u.T
    m_prev = m_ref[0]
    cm = jnp.zeros((L, LANE), F32)
    for h in range(N_HEADS):
        cmh = jnp.max(jnp.where(tril, u_t[h:h + 1, :], NEG), axis=1, keepdims=True)
        cm = jnp.where(lane_g == h, cmh, cm)
    mx = jnp.maximum(m_prev, cm)
    m_t = b + mx
    inter = jnp.exp(m_prev - mx)
    a_col = -mx

    qb16 = q.astype(BF16)
    kb16 = k.astype(BF16)
    vb16 = v.astype(BF16)
    num = jnp.zeros((L, w), F32)
    rs = jnp.zeros((L, LANE), F32)
    for h in range(N_HEADS):
        qk = _dot_nt(jnp.where(lmask[h], q, 0.0).astype(BF16), kb16)
        arg = jnp.where(tril, a_col[:, h:h + 1] + u_t[h:h + 1, :], NEG)
        wqk = jnp.exp(arg) * qk
        rs = jnp.where(lane_g == h, jnp.sum(wqk, axis=1, keepdims=True), rs)
        nv = jnp.dot(wqk.astype(BF16), vb16, preferred_element_type=F32)
        num = num + jnp.where(lmask[h], nv, 0.0)

    c_old = c_ref[0]
    n_old = n_ref[0]
    q_c = jnp.dot(qb16, c_old.astype(BF16), preferred_element_type=F32)
    q_n = _split_dot(q * n_old, e_sel)
    den = inter * q_n + rs
    inv = 1.0 / jnp.maximum(jnp.abs(den), jnp.exp(-m_t))
    hout = (_split_dot(_keep_heads(inter), et_sel) * q_c + num) * _split_dot(_keep_heads(inv), et_sel)

    mu = _split_dot(_split_dot(hout, e_sel) * (1.0 / dh), et_sel)
    xc = hout - mu
    var = _split_dot(xc * xc, e_sel) * (1.0 / dh)
    rstd = _split_dot(_keep_heads(lax.rsqrt(var + EPS)), et_sel)
    y = jax.nn.sigmoid(og) * (xc * rstd * normg)

    last = n_valid - 1
    m_last = m_t[last:last + 1, :]
    b_last = b[last:last + 1, :]
    decay = jnp.exp(b_last + m_prev - m_last)
    ws = jnp.exp(u + (b_last - m_last))
    if n_valid < L:
        ws = jnp.where(_iota((L, LANE), 0) < n_valid, ws, 0.0)
    ws_x = _split_dot(_keep_heads(ws), et_sel)
    decay_x = _split_dot(jnp.broadcast_to(_keep_heads(decay), (SUBLANE, LANE)), et_sel)[0:1]
    kw = k * ws_x
    upd = _dot_tn(kw.astype(BF16), vb16)
    blockdiag = (_iota((w, w), 0) // dh) == (_iota((w, w), 1) // dh)
    c_ref[0] = decay_x * c_old + jnp.where(blockdiag, upd, 0.0)
    n_ref[0] = decay_x * n_old + jnp.sum(kw, axis=0, keepdims=True)
    m_ref[0] = m_last
    return y


def _mlstm_kernel(q_ref, k_ref, v_ref, o_ref, gi_ref, gf_ref, bi_ref, bf_ref, ng_ref,
                  c0_ref, n0_ref, m0_ref, y_ref, c_ref, n_ref, m_ref):
    tt, w = q_ref.shape[1], q_ref.shape[2]
    kscale = (w // N_HEADS) ** -0.5

    @pl.when(pl.program_id(1) == 0)
    def _():
        c_ref[...] = c0_ref[...]
        n_ref[...] = n0_ref[...]
        m_ref[...] = m0_ref[...]

    bi = bi_ref[...]
    bf = bf_ref[...]
    ng = ng_ref[...]

    def chunk(rows, n_valid):
        pad = lambda r: _pad_rows(r[0, rows, :], SPAT)
        return _mlstm_chunk(pad(q_ref), pad(k_ref) * kscale, pad(v_ref), pad(o_ref),
                            pad(gi_ref) + bi, pad(gf_ref) + bf, ng, c_ref, n_ref, m_ref, n_valid)

    if tt < SPAT:
        y_ref[0] = chunk(slice(None), tt)[0:tt]
    else:
        def body(i, carry):
            rows = pl.ds(pl.multiple_of(i * SPAT, SPAT), SPAT)
            y_ref[0, rows, :] = chunk(rows, SPAT)
            return carry
        lax.fori_loop(0, tt // SPAT, body, 0)


def _mlstm(qkvo, gates, bi, bf, normg, c0, n0, m0, tt):
    bsz, t, w4 = qkvo.shape
    w = w4 // 4
    blk = lambda c: pl.BlockSpec((1, tt, w), lambda b, j: (b, j, c))
    gblk = lambda c: pl.BlockSpec((1, tt, LANE), lambda b, j: (b, j, c))
    st = lambda shape: pl.BlockSpec((1,) + shape, lambda b, j: (b, 0, 0))
    est = 2 * 5 * tt * w * 4 + 4 * tt * LANE * 4 + 6 * w * w * 4 + 40 * SPAT * w * 4 + 24 * SPAT * SPAT * 4
    return pl.pallas_call(
        _mlstm_kernel,
        out_shape=[jax.ShapeDtypeStruct((bsz, t, w), F32),
                   jax.ShapeDtypeStruct((bsz, w, w), F32),
                   jax.ShapeDtypeStruct((bsz, 1, w), F32),
                   jax.ShapeDtypeStruct((bsz, 1, LANE), F32)],
        grid=(bsz, t // tt),
        in_specs=[blk(0), blk(1), blk(2), blk(3), gblk(0), gblk(1),
                  _const_spec((1, LANE)), _const_spec((1, LANE)), _const_spec((1, w)),
                  st((w, w)), st((1, w)), st((1, LANE))],
        out_specs=[pl.BlockSpec((1, tt, w), lambda b, j: (b, j, 0)), st((w, w)), st((1, w)), st((1, LANE))],
        name="mlstm",
        compiler_params=_params(("parallel", "arbitrary"), est),
    )(qkvo, qkvo, qkvo, qkvo, gates, gates, bi, bf, normg, c0, n0, m0)


def _ssd_chunk(xs, bm, cm, z, dt, a_row, dskip, normg, s_ref):
    L, w = xs.shape
    ng = s_ref.shape[1]
    gw = w // ng
    _, et_sel = _head_selectors(w)
    lmask = _head_lane_masks(L, w)
    tril = _tril(L)
    da = dt * a_row
    cs = _split_dot_left(tril.astype(BF16), da)
    cs_t = cs.T
    cs_last = cs[L - 1:L, :]
    wl = jnp.exp(cs_last - cs) * dt
    xdt = (xs * _split_dot(dt, et_sel)).astype(BF16)
    cmb = cm.astype(BF16)
    bmb = bm.astype(BF16)
    cb = [_dot_nt(cmb[:, g * gw:(g + 1) * gw], bmb[:, g * gw:(g + 1) * gw]) for g in range(ng)]
    y = jnp.zeros((L, w), F32)
    for h in range(N_HEADS):
        dec = jnp.exp(jnp.where(tril, cs[:, h:h + 1] - cs_t[h:h + 1, :], NEG))
        mh = (cb[h * ng // N_HEADS] * dec).astype(BF16)
        y = y + jnp.where(lmask[h], jnp.dot(mh, xdt, preferred_element_type=F32), 0.0)
    s_old = [s_ref[0, g] for g in range(ng)]
    y_in = jnp.concatenate([jnp.dot(cmb[:, g * gw:(g + 1) * gw], s_old[g].astype(BF16),
                                    preferred_element_type=F32) for g in range(ng)], axis=1)
    y = y + _split_dot(_keep_heads(jnp.exp(cs)), et_sel) * y_in
    wx = (xs * _split_dot(_keep_heads(wl), et_sel)).astype(BF16)
    dec_x = _split_dot(jnp.broadcast_to(_keep_heads(jnp.exp(cs_last)), (SUBLANE, LANE)), et_sel)[0:1]
    for g in range(ng):
        cols = slice(g * gw, (g + 1) * gw)
        s_ref[0, g] = dec_x[:, cols] * s_old[g] + _dot_tn(bmb[:, cols], wx[:, cols])
    yc = y + dskip * xs
    return _rmsnorm_rows(yc * (z * jax.nn.sigmoid(z)), normg)


def _ssd_kernel(xbc_ref, z_ref, gd_ref, cw_ref, cb_ref, dtb_ref, a_ref, dskip_ref, ng_ref, conv0_ref, s0_ref,
                y_ref, conv_ref, s_ref, ext_ref):
    tt, w = z_ref.shape[1], z_ref.shape[2]
    hist = CONV_C - 1
    off = SUBLANE

    @pl.when(pl.program_id(1) == 0)
    def _():
        s_ref[...] = s0_ref[...]
        ext_ref[off - hist:off, :] = conv0_ref[0]

    ext_ref[off:off + tt, :] = xbc_ref[0]
    cw = cw_ref[...]
    acc = cb_ref[...] + cw[CONV_C - 1:CONV_C, :] * ext_ref[off:off + tt, :]
    for j in range(CONV_C - 1):
        sh = CONV_C - 1 - j
        acc = acc + cw[j:j + 1, :] * ext_ref[off - sh:off - sh + tt, :]
    tail = ext_ref[off + tt - hist:off + tt, :]
    ext_ref[off - hist:off, :] = tail
    conv_ref[0] = tail
    xbc = acc * jax.nn.sigmoid(acc)
    ext_ref[off:off + tt, :] = xbc

    dtb = dtb_ref[...]
    a_row = a_ref[...]
    dskip = dskip_ref[...]
    ng = ng_ref[...]

    def chunk(ext_rows, rows, n_valid):
        act = _pad_rows(ext_ref[ext_rows, :], SPAT)
        dt = _keep_heads(jax.nn.softplus(_pad_rows(gd_ref[0, rows, :], SPAT) + dtb))
        if n_valid < SPAT:
            dt = jnp.where(_iota((SPAT, LANE), 0) < n_valid, dt, 0.0)
        return _ssd_chunk(act[:, 0:w], act[:, w:2 * w], act[:, 2 * w:3 * w], _pad_rows(z_ref[0, rows, :], SPAT),
                          dt, a_row, dskip, ng, s_ref)

    if tt < SPAT:
        y_ref[0] = chunk(slice(off, off + tt), slice(None), tt)[0:tt]
    else:
        def body(i, carry):
            start = pl.multiple_of(i * SPAT, SPAT)
            y_ref[0, pl.ds(start, SPAT), :] = chunk(pl.ds(pl.multiple_of(start + off, SUBLANE), SPAT), pl.ds(start, SPAT), SPAT)
            return carry
        lax.fori_loop(0, tt // SPAT, body, 0)


def _ssd(cin, gates, conv_w, conv_b, dtb, a_row, dskip, normg, conv0, s0, tt):
    bsz, t, wtot = cin.shape
    w = dskip.shape[1]
    xw = wtot - w
    ng, n_c, gp = s0.shape[1:]
    st = lambda shape: pl.BlockSpec((1,) + shape, lambda b, j: (b,) + (0,) * len(shape))
    est = 2 * 2 * tt * wtot * 4 + 2 * tt * w * 4 + (tt + SUBLANE) * xw * 4 + 3 * tt * xw * 4 + 40 * SPAT * w * 4
    return pl.pallas_call(
        _ssd_kernel,
        out_shape=[jax.ShapeDtypeStruct((bsz, t, w), F32),
                   jax.ShapeDtypeStruct((bsz, CONV_C - 1, xw), F32),
                   jax.ShapeDtypeStruct(s0.shape, F32)],
        grid=(bsz, t // tt),
        in_specs=[pl.BlockSpec((1, tt, xw), lambda b, j: (b, j, 0)),
                  pl.BlockSpec((1, tt, w), lambda b, j: (b, j, xw // w)),
                  pl.BlockSpec((1, tt, LANE), lambda b, j: (b, j, 2)),
                  _const_spec((CONV_C, xw)), _const_spec((1, xw)), _const_spec((1, LANE)), _const_spec((1, LANE)),
                  _const_spec((1, w)), _const_spec((1, w)),
                  st((CONV_C - 1, xw)), st((ng, n_c, gp))],
        out_specs=[pl.BlockSpec((1, tt, w), lambda b, j: (b, j, 0)), st((CONV_C - 1, xw)), st((ng, n_c, gp))],
        scratch_shapes=[pltpu.VMEM((tt + SUBLANE, xw), F32)],
        name="ssd",
        compiler_params=_params(("parallel", "arbitrary"), est),
    )(cin, cin, gates, conv_w, conv_b, dtb, a_row, dskip, normg, conv0, s0)


def _gmlp_kernel(u_ref, v_ref, ng_ref, ws_ref, bs_ref, y_ref, vn_ref):
    tt, w = u_ref.shape[1], u_ref.shape[2]
    u = jax.nn.gelu(u_ref[0])
    vr = jax.nn.gelu(v_ref[0])
    mu = jnp.mean(vr, axis=-1, keepdims=True)
    xc = vr - mu
    vn = xc * lax.rsqrt(jnp.mean(xc * xc, axis=-1, keepdims=True) + EPS) * ng_ref[...]
    vn_ref[0] = vn
    rows = _iota((N_HEADS * SPAT, SPAT), 0) % SPAT
    wst = jnp.where(rows >= _iota((N_HEADS * SPAT, SPAT), 1), ws_ref[...], 0.0).astype(BF16)
    lmask = _head_lane_masks(SPAT, w)
    bias = bs_ref[...]
    vpad = _pad_rows(vn, -(-tt // SPAT) * SPAT).astype(BF16)
    for i in range(vpad.shape[0] // SPAT):
        fs = jnp.dot(wst, vpad[i * SPAT:(i + 1) * SPAT], preferred_element_type=F32)
        f = bias
        for g in range(N_HEADS):
            f = f + jnp.where(lmask[g], fs[g * SPAT:(g + 1) * SPAT], 0.0)
        n = min(SPAT, tt - i * SPAT)
        y_ref[0, i * SPAT:i * SPAT + n, :] = u[i * SPAT:i * SPAT + n] * f[0:n]


def _gmlp(din, normg, w_s, b_x, tt):
    bsz, t, w2 = din.shape
    w = w2 // 2
    est = 2 * 4 * tt * w * 4 + 8 * tt * w * 4 + 4 * N_HEADS * SPAT * (SPAT + w) * 4
    return pl.pallas_call(
        _gmlp_kernel,
        out_shape=[jax.ShapeDtypeStruct((bsz, t, w), F32), jax.ShapeDtypeStruct((bsz, t, w), F32)],
        grid=(bsz, t // tt),
        in_specs=[pl.BlockSpec((1, tt, w), lambda b, j: (b, j, 0)),
                  pl.BlockSpec((1, tt, w), lambda b, j: (b, j, 1)),
                  _const_spec((1, w)), _const_spec((N_HEADS * SPAT, SPAT)), _const_spec((SPAT, w))],
        out_specs=[pl.BlockSpec((1, tt, w), lambda b, j: (b, j, 0))] * 2,
        name="spatial_gate",
        compiler_params=_params(("parallel", "parallel"), est),
    )(din, din, normg, w_s, b_x)


def _post_kernel(x_ref, ya_ref, yb_ref, yc_ref, yd_ref, mod_ref, g_ref, wo_ref, wg_ref, wu_ref, wd_ref,
                 cw_ref, cb_ref, f0_ref, gfin_ref, o_ref, fc_ref, ext_ref, *, final):
    bb, tt, d = x_ref.shape
    m = bb * tt
    gw = ya_ref.shape[2]
    f = wg_ref.shape[1]
    hist = CONV_F - 1
    off = SUBLANE
    mod = mod_ref[...]
    mix = None
    for j, y_ref in enumerate((ya_ref, yb_ref, yc_ref, yd_ref)):
        p = jnp.dot(y_ref[...].reshape(m, gw).astype(BF16), wo_ref[j * gw:(j + 1) * gw, :],
                    preferred_element_type=F32)
        mix = p if mix is None else mix + p
    x1 = x_ref[...] + mod[:, :, 2 * d:3 * d] * mix.reshape(bb, tt, d)
    h2 = _rmsnorm_rows(x1, g_ref[...]) * (1.0 + mod[:, :, 4 * d:5 * d]) + mod[:, :, 3 * d:4 * d]
    hb = h2.reshape(m, d).astype(BF16)
    g = jnp.dot(hb, wg_ref[...], preferred_element_type=F32)
    u = jnp.dot(hb, wu_ref[...], preferred_element_type=F32)

    @pl.when(pl.program_id(1) == 0)
    def _():
        ext_ref[:, off - hist:off, :] = f0_ref[...]

    ext_ref[:, off:off + tt, :] = g.reshape(bb, tt, f)
    cw = cw_ref[...]
    acc = cb_ref[...] + cw[CONV_F - 1:CONV_F, :] * ext_ref[:, off:off + tt, :]
    for j in range(CONV_F - 1):
        sh = CONV_F - 1 - j
        acc = acc + cw[j:j + 1, :] * ext_ref[:, off - sh:off - sh + tt, :]
    tail = ext_ref[:, off + tt - hist:off + tt, :]
    ext_ref[:, off - hist:off, :] = tail
    fc_ref[...] = tail
    act = (acc * jax.nn.sigmoid(acc)).reshape(m, f) * u
    down = jnp.dot(act.astype(BF16), wd_ref[...], preferred_element_type=F32)
    x2 = x1 + mod[:, :, 5 * d:6 * d] * down.reshape(bb, tt, d)
    if final:
        x2 = _rmsnorm_rows(x2, gfin_ref[...])
    o_ref[...] = x2


def _post(x, ys, mod, g_ffn, w_out, w_gate, w_up, w_down, conv_w, conv_b, f0, g_final, bb, tt, final):
    bsz, t, d = x.shape
    gw = ys[0].shape[2]
    f = w_gate.shape[1]
    m = bb * tt
    tok = lambda wd: pl.BlockSpec((bb, tt, wd), lambda i, j: (i, j, 0))
    est = (2 * 2 * m * d * 4 + 2 * 4 * m * gw * 4 + 2 * (d * d + 3 * d * f) * 2 + bb * (tt + SUBLANE) * f * 4
           + 5 * m * f * 4 + 4 * m * d * 4)
    return pl.pallas_call(
        functools.partial(_post_kernel, final=final),
        out_shape=[jax.ShapeDtypeStruct((bsz, t, d), F32), jax.ShapeDtypeStruct((bsz, CONV_F - 1, f), F32)],
        grid=(bsz // bb, t // tt),
        in_specs=[tok(d), tok(gw), tok(gw), tok(gw), tok(gw),
                  pl.BlockSpec((bb, 1, mod.shape[2]), lambda i, j: (i, 0, 0)),
                  _const_spec((1, 1, d)), _const_spec((d, d)), _const_spec((d, f)), _const_spec((d, f)),
                  _const_spec((f, d)), _const_spec((CONV_F, f)), _const_spec((1, f)),
                  pl.BlockSpec((bb, CONV_F - 1, f), lambda i, j: (i, 0, 0)),
                  _const_spec((1, 1, d))],
        out_specs=[tok(d), pl.BlockSpec((bb, CONV_F - 1, f), lambda i, j: (i, 0, 0))],
        scratch_shapes=[pltpu.VMEM((bb, tt + SUBLANE, f), F32)],
        name="post_ffn",
        compiler_params=_params(("parallel", "arbitrary"), est),
    )(x, *ys, mod, g_ffn.reshape(1, 1, d), w_out, w_gate, w_up, w_down, conv_w, conv_b, f0, g_final.reshape(1, 1, d))


def _lane_row(vals):
    return jnp.zeros((1, LANE), F32).at[0, :vals.shape[0]].set(vals.astype(F32))


def _layer(x, mod, st, p, tiles, final, g_final):
    bsz, t, d = x.shape
    gw = d // 4
    dh = gw // N_HEADS
    bb, tt, tmix, tpost = tiles
    mod3 = mod.reshape(bsz, 1, mod.shape[1])
    qkv_a, qkvo_b, c_in, d_in, gates = _in_proj(x, mod3, p["g_mix"], p["w_in"], p["widths"], bb, tt)

    y_a = _attention(qkv_a, st["a_k"], st["a_v"], p["bias_tab"], tmix)
    y_b, c_new, n_new, m_new = _mlstm(qkvo_b, gates, p["bi"], p["bf"], p["b_norm_g"], st["b_c"], st["b_n"], st["b_m"], tmix)
    y_c, conv_new, ssm_new = _ssd(c_in, gates, p["c_conv_w"], p["c_conv_b"], p["dtb"], p["a_row"], p["dskip"],
                                  p["c_norm_g"], st["c_conv"], st["c_ssm"], tmix)
    y_d, vn = _gmlp(d_in, p["d_norm_g"], p["d_w_s"], p["d_b_x"], tmix)
    x_new, fconv_new = _post(x, (y_a, y_b, y_c, y_d), mod3, p["g_ffn"], p["w_out"], p["f_w_gate"], p["f_w_up"],
                             p["f_w_down"], p["f_conv_w"], p["f_conv_b"], st["f_conv"], g_final, bb, tpost, final)

    keep = min(A_WIN, t)
    new_k = qkv_a[:, t - keep:, gw:2 * gw].reshape(bsz, keep, N_HEADS, dh)
    new_v = qkv_a[:, t - keep:, 2 * gw:3 * gw].reshape(bsz, keep, N_HEADS, dh)
    idx = jnp.arange(N_HEADS)
    c_heads = c_new.reshape(bsz, N_HEADS, dh, N_HEADS, dh)[:, idx, :, idx, :]
    c_heads = jnp.moveaxis(c_heads, 0, 1)
    n_heads = n_new.reshape(bsz, N_HEADS, dh)
    m_heads = m_new[:, 0, :N_HEADS]
    ng = ssm_new.shape[1]
    ssm = jnp.swapaxes(ssm_new, 2, 3).reshape(bsz, N_HEADS, gw // N_HEADS, ssm_new.shape[2])
    outs = (new_k, new_v, c_heads, n_heads, m_heads, ssm, conv_new, fconv_new, vn)
    return x_new, outs


def _prep_layer(l, w_in, w_out, a_rel_bias, b_i_bias, b_f_bias, b_norm_g, c_conv_w, c_conv_b, c_dt_bias, c_a_log,
                c_d_skip, c_norm_g, d_norm_g, d_w_s, d_b_s, f_w_gate, f_w_up, f_conv_w, f_conv_b, f_w_down,
                g_norm_mix, g_norm_ffn):
    d = w_in.shape[1]
    gw = d // 4
    nh = b_i_bias.shape[1]
    xbc_w = c_conv_w.shape[2]
    sizes = (gw,) * 7 + (nh, nh, gw, xbc_w, c_dt_bias.shape[1], gw, gw)
    offs = [0]
    for s in sizes:
        offs.append(offs[-1] + s)
    w = w_in[l]
    col = lambda i: w[:, offs[i]:offs[i + 1]]
    padg = lambda i: jnp.pad(col(i), ((0, 0), (0, LANE - sizes[i])))
    w_perm = jnp.concatenate([col(0), col(1), col(2), col(3), col(4), col(5), col(6), col(10), col(9),
                              col(12), col(13), padg(7), padg(8), padg(11)], axis=1).astype(BF16)
    widths = (3 * gw, 4 * gw, xbc_w + gw, 2 * gw, 3 * LANE)
    heads_x = lambda v: jnp.repeat(v.astype(F32), gw // nh)[None, :]
    return dict(
        w_in=w_perm, widths=widths, g_mix=g_norm_mix[l], g_ffn=g_norm_ffn[l],
        w_out=w_out[l].astype(BF16), bias_tab=_bias_table(a_rel_bias[l]),
        bi=_lane_row(b_i_bias[l]), bf=_lane_row(b_f_bias[l]), b_norm_g=b_norm_g[l][None, :],
        c_conv_w=c_conv_w[l], c_conv_b=c_conv_b[l][None, :], dtb=_lane_row(c_dt_bias[l]),
        a_row=_lane_row(-jnp.exp(c_a_log[l].astype(F32))), dskip=heads_x(c_d_skip[l]), c_norm_g=c_norm_g[l][None, :],
        d_norm_g=d_norm_g[l][None, :], d_w_s=d_w_s[l].reshape(-1, d_w_s.shape[-1]),
        d_b_x=jnp.repeat(d_b_s[l].T, gw // d_b_s.shape[1], axis=1),
        f_w_gate=f_w_gate[l].astype(BF16), f_w_up=f_w_up[l].astype(BF16), f_w_down=f_w_down[l].astype(BF16),
        f_conv_w=f_conv_w[l], f_conv_b=f_conv_b[l][None, :],
    )


def _blockdiag_state(c):
    bsz, nh, dh, _ = c.shape
    eye = jnp.eye(nh, dtype=c.dtype)
    return jnp.einsum("bhde,hg->bhdge", c, eye).reshape(bsz, nh * dh, nh * dh)


def kernel(x_prompt, x_sample, c_prompt, c_sample, cache_a_k, cache_a_v, state_b_c, state_b_n, state_b_m, state_c_ssm, state_c_conv, state_ffn_conv, w_ada, b_ada, g_norm_mix, g_norm_ffn, w_in, w_out, a_rel_bias, b_i_bias, b_f_bias, b_norm_g, c_conv_w, c_conv_b, c_dt_bias, c_a_log, c_d_skip, c_norm_g, d_norm_g, d_w_s, d_b_s, f_w_gate, f_w_up, f_conv_w, f_conv_b, f_w_down, g_final):
    depth = w_in.shape[0]
    bp, tp, d = x_prompt.shape
    bs, ts, _ = x_sample.shape
    gw = d // 4
    dh = gw // N_HEADS
    f = f_w_gate.shape[2]
    xbc_w = c_conv_w.shape[2]
    g_c = (xbc_w - gw) // 2 // (state_c_ssm.shape[-1])
    n_c = state_c_ssm.shape[-1]

    mod_all = _ada(jnp.concatenate([c_prompt, c_sample], axis=0), w_ada, b_ada)

    tmix_p = min(A_WIN, tp)
    tiles_p = (1, tmix_p, tmix_p, min(256, tp))
    tiles_s = (bs, ts, ts, ts)
    zeros = lambda *shape: jnp.zeros(shape, F32)

    xp, xs = x_prompt, x_sample
    p_states, s_states = [], []
    for l in range(depth):
        p = _prep_layer(l, w_in, w_out, a_rel_bias, b_i_bias, b_f_bias, b_norm_g, c_conv_w, c_conv_b, c_dt_bias,
                        c_a_log, c_d_skip, c_norm_g, d_norm_g, d_w_s, d_b_s, f_w_gate, f_w_up, f_conv_w, f_conv_b,
                        f_w_down, g_norm_mix, g_norm_ffn)
        final = l == depth - 1
        st_p = dict(a_k=None, a_v=None, b_c=zeros(bp, gw, gw), b_n=zeros(bp, 1, gw), b_m=zeros(bp, 1, LANE),
                    c_conv=zeros(bp, CONV_C - 1, xbc_w), c_ssm=zeros(bp, g_c, n_c, gw // g_c),
                    f_conv=zeros(bp, CONV_F - 1, f))
        xp, sp = _layer(xp, mod_all[l, :bp], st_p, p, tiles_p, final, g_final)
        st_s = dict(a_k=cache_a_k[l].reshape(bs, -1, gw), a_v=cache_a_v[l].reshape(bs, -1, gw),
                    b_c=_blockdiag_state(state_b_c[l]), b_n=state_b_n[l].reshape(bs, 1, gw),
                    b_m=jnp.pad(state_b_m[l], ((0, 0), (0, LANE - N_HEADS)))[:, None, :],
                    c_conv=state_c_conv[l],
                    c_ssm=jnp.swapaxes(state_c_ssm[l].reshape(bs, g_c, gw // g_c, n_c), 2, 3),
                    f_conv=state_ffn_conv[l])
        xs, ss = _layer(xs, mod_all[l, bp:], st_s, p, tiles_s, final, g_final)
        p_states.append(sp)
        s_states.append(ss)

    stack = lambda states, i: jnp.stack([s[i] for s in states])
    return (xp, xs,
            *(stack(p_states, i) for i in range(8)),
            *(stack(s_states, i) for i in range(9)))
```

```python
import functools

import jax
import jax.numpy as jnp
from jax import lax
from jax.experimental import pallas as pl
from jax.experimental.pallas import tpu as pltpu

F32 = jnp.float32
BF16 = jnp.bfloat16

EPS = 1e-6
NEG = -1e30

CHUNK = 64
N_BAND = 8
A_WIN = N_BAND * CHUNK
REL_CLIP = 128
N_HEADS = 4
SPAT = 128
CONV_C = 4
CONV_F = 3
LANE = 128
SUBLANE = 8
VMEM_CAP = 64 * 1024 * 1024


def _vmem_limit(nbytes):
    return int(min(max(nbytes, 16 * 1024 * 1024), VMEM_CAP - 8 * 1024 * 1024))


def _params(sem, nbytes):
    return pltpu.CompilerParams(dimension_semantics=sem, vmem_limit_bytes=_vmem_limit(nbytes))


def _const_spec(shape):
    nd = len(shape)
    return pl.BlockSpec(shape, lambda *_: (0,) * nd, pipeline_mode=pl.Buffered(1))


def _iota(shape, dim):
    return lax.broadcasted_iota(jnp.int32, shape, dim)


def _split_dot(x, e, parts):
    acc = None
    r = x
    for i in range(parts):
        hi = r.astype(BF16)
        d = jnp.dot(hi, e, preferred_element_type=F32)
        acc = d if acc is None else acc + d
        if i + 1 < parts:
            r = r - hi.astype(F32)
    return acc


def _split_dot_left(e, x, parts):
    acc = None
    r = x
    for i in range(parts):
        hi = r.astype(BF16)
        d = jnp.dot(e, hi, preferred_element_type=F32)
        acc = d if acc is None else acc + d
        if i + 1 < parts:
            r = r - hi.astype(F32)
    return acc


def _dot_nt(a, b):
    return lax.dot_general(a, b, (((1,), (1,)), ((), ())), preferred_element_type=F32)


def _dot_tn(a, b):
    return lax.dot_general(a, b, (((0,), (0,)), ((), ())), preferred_element_type=F32)


def _head_selectors(width):
    dh = width // N_HEADS
    e = (_iota((width, LANE), 0) // dh == _iota((width, LANE), 1)).astype(BF16)
    et = (_iota((LANE, width), 1) // dh == _iota((LANE, width), 0)).astype(BF16)
    return e, et


def _head_lane_masks(rows, width):
    dh = width // N_HEADS
    lane = _iota((rows, width), 1)
    return [(lane >= h * dh) & (lane < (h + 1) * dh) for h in range(N_HEADS)]


def _tril(n):
    return _iota((n, n), 0) >= _iota((n, n), 1)


def _pad_rows(x, rows):
    if x.shape[0] == rows:
        return x
    return jnp.concatenate([x, jnp.zeros((rows - x.shape[0], x.shape[1]), x.dtype)], axis=0)


def _interleave(gens):
    results = [None] * len(gens)
    live = list(range(len(gens)))
    while live:
        for i in list(live):
            try:
                next(gens[i])
            except StopIteration as stop:
                results[i] = stop.value
                live.remove(i)
    return results


def _rmsnorm_rows(x, g):
    return x * lax.rsqrt(jnp.mean(x * x, axis=-1, keepdims=True) + EPS) * g


def _regroup_kernel(w_ref, o_ref, *, moves):
    w = w_ref[0]
    o_ref[...] = jnp.zeros(o_ref.shape, o_ref.dtype)
    for src, dst, n in moves:
        o_ref[:, dst:dst + n] = w[:, src:src + n].astype(o_ref.dtype)


def _regroup_cast(w_all, layer, moves, ncols, row_block=256):
    _, r, c = w_all.shape
    return pl.pallas_call(
        functools.partial(_regroup_kernel, moves=tuple(moves)),
        out_shape=jax.ShapeDtypeStruct((r, ncols), BF16),
        grid=(r // row_block,),
        in_specs=[pl.BlockSpec((1, row_block, c), lambda i: (layer, i, 0))],
        out_specs=pl.BlockSpec((row_block, ncols), lambda i: (i, 0)),
        name="regroup_cast",
        compiler_params=_params(("parallel",), 2 * row_block * (c * 4 + ncols * 2) + row_block * c * 4),
    )(w_all)


def _ada_kernel(c_ref, w_ref, b_ref, o_ref):
    c = c_ref[...]
    h = (c * jax.nn.sigmoid(c)).astype(BF16)
    o_ref[0] = jnp.dot(h, w_ref[0].astype(BF16), preferred_element_type=F32) + b_ref[0]


def _ada(c_all, w_ada, b_ada):
    depth, d, n6 = w_ada.shape
    r = c_all.shape[0]
    tn = d
    return pl.pallas_call(
        _ada_kernel,
        out_shape=jax.ShapeDtypeStruct((depth, r, n6), F32),
        grid=(depth, n6 // tn),
        in_specs=[pl.BlockSpec((r, d), lambda l, j: (0, 0)),
                  pl.BlockSpec((1, d, tn), lambda l, j: (l, 0, j)),
                  pl.BlockSpec((1, 1, tn), lambda l, j: (l, 0, j))],
        out_specs=pl.BlockSpec((1, r, tn), lambda l, j: (l, 0, j)),
        name="ada_mod",
        compiler_params=_params(("parallel", "parallel"), 4 * (2 * d * tn * 4 + 2 * r * tn * 4 + r * d * 4)),
    )(c_all, w_ada, b_ada.reshape(depth, 1, n6))


def _bias_kernel(rb_ref, o_ref, *, lo, hi):
    nh, lq, lk = o_ref.shape
    idx = jnp.clip(A_WIN + _iota((lq, lk), 0) - _iota((lq, lk), 1), -REL_CLIP, REL_CLIP) + REL_CLIP
    for h in range(nh):
        def body(r, acc, h=h):
            return jnp.where(idx == r, rb_ref[h, r], acc)
        o_ref[h] = lax.fori_loop(lo, hi + 1, body, jnp.zeros((lq, lk), F32))


def _bias_table(rel_bias):
    nh = rel_bias.shape[0]
    band = (N_BAND + 1) * CHUNK
    lo = max(A_WIN - (band - 1), -REL_CLIP) + REL_CLIP
    hi = min(A_WIN + CHUNK - 1, REL_CLIP) + REL_CLIP
    return pl.pallas_call(
        functools.partial(_bias_kernel, lo=lo, hi=hi),
        out_shape=jax.ShapeDtypeStruct((nh, CHUNK, band), F32),
        in_specs=[pl.BlockSpec(memory_space=pltpu.SMEM)],
        out_specs=pl.BlockSpec(memory_space=pltpu.VMEM),
        name="rel_bias_table",
    )(rel_bias)


def _in_kernel(x_ref, mod_ref, g_ref, w_ref, *o_refs, col_starts):
    bb, tt, d = x_ref.shape
    x = x_ref[...]
    mod = mod_ref[...]
    h = _rmsnorm_rows(x, g_ref[...]) * (1.0 + mod[:, :, d:2 * d]) + mod[:, :, 0:d]
    hb = h.reshape(bb * tt, d).astype(BF16)
    for o_ref, (a, b) in zip(o_refs, col_starts):
        o_ref[...] = jnp.dot(hb, w_ref[:, a:b], preferred_element_type=F32).reshape(bb, tt, b - a)


def _in_proj(x, mod, g, w, widths, bb, tt):
    bsz, t, d = x.shape
    ncols = w.shape[1]
    starts, a = [], 0
    for wd in widths:
        starts.append((a, a + wd))
        a += wd
    m = bb * tt
    est = 2 * m * d * 4 + 2 * d * ncols * 2 + 2 * m * ncols * 4 + 3 * m * d * 4 + m * max(widths) * 4
    return pl.pallas_call(
        functools.partial(_in_kernel, col_starts=tuple(starts)),
        out_shape=[jax.ShapeDtypeStruct((bsz, t, wd), F32) for wd in widths],
        grid=(bsz // bb, t // tt),
        in_specs=[pl.BlockSpec((bb, tt, d), lambda i, j: (i, j, 0)),
                  pl.BlockSpec((bb, 1, mod.shape[2]), lambda i, j: (i, 0, 0)),
                  _const_spec((1, 1, d)),
                  _const_spec((d, ncols))],
        out_specs=[pl.BlockSpec((bb, tt, wd), lambda i, j: (i, j, 0)) for wd in widths],
        name="in_proj",
        compiler_params=_params(("parallel", "parallel"), est),
    )(x, mod, g.reshape(1, 1, d), w)


def _attend_chunk(qc, kb, vb, bias, kvalid, masks_q, masks_o):
    lq = qc.shape[0]
    qs = jnp.concatenate([jnp.where(mk, qc, 0.0) for mk in masks_q], axis=0).astype(BF16)
    s = _dot_nt(qs, kb) + bias
    if kvalid is not None:
        s = jnp.where(kvalid, s, NEG)
    e = jnp.exp(s - jnp.max(s, axis=-1, keepdims=True))
    o = jnp.dot(e.astype(BF16), vb, preferred_element_type=F32)
    o = o * (1.0 / jnp.sum(e, axis=-1, keepdims=True))
    out = jnp.where(masks_o[0], o[0:lq], 0.0)
    for h in range(1, N_HEADS):
        out = out + jnp.where(masks_o[h], o[h * lq:(h + 1) * lq], 0.0)
    return out


def _attn_kernel(q_ref, k_ref, v_ref, kh_ref, vh_ref, bias_ref, y_ref, kk_ref, vv_ref, *, hist_is_cache, t_valid):
    tq, w = q_ref.shape[1], q_ref.shape[2]
    tpad = kk_ref.shape[0] - A_WIN
    scale = (w // N_HEADS) ** -0.5
    kk_ref[0:A_WIN, :] = kh_ref[0].astype(BF16)
    vv_ref[0:A_WIN, :] = vh_ref[0].astype(BF16)
    kk_ref[A_WIN:A_WIN + tpad, :] = _pad_rows(k_ref[0], tpad).astype(BF16)
    vv_ref[A_WIN:A_WIN + tpad, :] = _pad_rows(v_ref[0], tpad).astype(BF16)
    masks = _head_lane_masks(CHUNK, w)
    bias = bias_ref[...]
    band = (N_BAND + 1) * CHUNK
    slot = _iota((1, band), 1)
    first = A_WIN if not hist_is_cache else 0
    hist_ok = jnp.logical_or(pl.program_id(1) > 0, hist_is_cache)
    q_all = _pad_rows(q_ref[0], tpad) * scale
    for i in range(tpad // CHUNK):
        base = i * CHUNK
        pos = slot + base
        kvalid = (pos < A_WIN + t_valid) & ((pos >= A_WIN) | hist_ok)
        out = _attend_chunk(q_all[base:base + CHUNK], kk_ref[base:base + band, :], vv_ref[base:base + band, :],
                            bias, kvalid, masks, masks)
        rows = min(CHUNK, tq - base)
        y_ref[0, base:base + rows, :] = out[0:rows].astype(y_ref.dtype)


def _attention(qkv, k_hist, v_hist, bias_tab, tq):
    bsz, t, w3 = qkv.shape
    w = w3 // 3
    hist_is_cache = k_hist is not None
    nt = t // tq
    tpad = -(-tq // CHUNK) * CHUNK
    band = (N_BAND + 1) * CHUNK
    if hist_is_cache:
        assert nt == 1
        hist_specs = [pl.BlockSpec((1, A_WIN, w), lambda b, j: (b, 0, 0))] * 2
        hist_args = (k_hist, v_hist)
    else:
        assert tq == A_WIN
        hist_specs = [pl.BlockSpec((1, tq, w), lambda b, j: (b, jnp.maximum(j - 1, 0), 1)),
                      pl.BlockSpec((1, tq, w), lambda b, j: (b, jnp.maximum(j - 1, 0), 2))]
        hist_args = (qkv, qkv)
    est = 2 * 5 * tpad * w * 4 + 2 * tq * w * 4 + 2 * (A_WIN + tpad) * w * 2 + 12 * N_HEADS * CHUNK * band * 4
    return pl.pallas_call(
        functools.partial(_attn_kernel, hist_is_cache=hist_is_cache, t_valid=tq if nt == 1 else tpad),
        out_shape=jax.ShapeDtypeStruct((bsz, t, w), BF16),
        grid=(bsz, nt),
        in_specs=[pl.BlockSpec((1, tq, w), lambda b, j: (b, j, 0)),
                  pl.BlockSpec((1, tq, w), lambda b, j: (b, j, 1)),
                  pl.BlockSpec((1, tq, w), lambda b, j: (b, j, 2)),
                  *hist_specs,
                  _const_spec((N_HEADS * CHUNK, band))],
        out_specs=pl.BlockSpec((1, tq, w), lambda b, j: (b, j, 0)),
        scratch_shapes=[pltpu.VMEM((A_WIN + tpad, w), BF16), pltpu.VMEM((A_WIN + tpad, w), BF16)],
        name="band_attention",
        compiler_params=_params(("parallel", "parallel"), est),
    )(qkv, qkv, qkv, *hist_args, bias_tab.reshape(N_HEADS * CHUNK, band))


def _mlstm_chunk(q, k, v, og, gi, gf, normg, c_ref, n_ref, m_ref, bi, n_valid):
    L, w = q.shape
    dh = w // N_HEADS
    lmask = _head_lane_masks(L, w)
    tril = _tril(L)
    blockdiag = (_iota((w, w), 0) // dh) == (_iota((w, w), 1) // dh)
    bd = blockdiag.astype(BF16)

    m_prev = m_ref[bi]
    c_old = c_ref[bi]
    n_old = n_ref[bi]
    qb16 = q.astype(BF16)
    kb16 = k.astype(BF16)
    vb16 = v.astype(BF16)
    lf = jax.nn.log_sigmoid(gf)
    yield
    b = _split_dot_left(tril.astype(BF16), lf, 3)
    q_c = jnp.dot(qb16, c_old.astype(BF16), preferred_element_type=F32)
    q_n = _split_dot(q * n_old, bd, 2)
    yield
    qk = [_dot_nt(jnp.where(lmask[h], q, 0.0).astype(BF16), kb16) for h in range(N_HEADS)]
    u = gi - b
    u_t = u.T
    yield
    cm = jnp.zeros((L, w), F32)
    for h in range(N_HEADS):
        cmh = jnp.max(jnp.where(tril, u_t[h * dh:h * dh + 1, :], NEG), axis=1, keepdims=True)
        cm = jnp.where(lmask[h], cmh, cm)
    mx = jnp.maximum(m_prev, cm)
    m_t = b + mx
    inter = jnp.exp(m_prev - mx)
    yield
    num = jnp.zeros((L, w), F32)
    rs = jnp.zeros((L, w), F32)
    for h in range(N_HEADS):
        arg = jnp.where(tril, u_t[h * dh:h * dh + 1, :] - mx[:, h * dh:h * dh + 1], NEG)
        wqk = jnp.exp(arg) * qk[h]
        rs = jnp.where(lmask[h], jnp.sum(wqk, axis=1, keepdims=True), rs)
        nv = jnp.dot(wqk.astype(BF16), vb16, preferred_element_type=F32)
        num = num + jnp.where(lmask[h], nv, 0.0)
        yield

    last = n_valid - 1
    m_last = m_t[last:last + 1, :]
    b_last = b[last:last + 1, :]
    decay = jnp.exp(b_last + m_prev - m_last)
    ws = jnp.exp(u + (b_last - m_last))
    if n_valid < L:
        ws = jnp.where(_iota((L, w), 0) < n_valid, ws, 0.0)
    kw = k * ws
    upd = _dot_tn(kw.astype(BF16), vb16)
    yield
    den = inter * q_n + rs
    hout = (inter * q_c + num) / jnp.maximum(jnp.abs(den), jnp.exp(-m_t))
    mu = _split_dot(hout, bd, 2) * (1.0 / dh)
    yield
    xc = hout - mu
    var = _split_dot(xc * xc, bd, 2) * (1.0 / dh)
    yield
    y = jax.nn.sigmoid(og) * (xc * lax.rsqrt(var + EPS) * normg)
    c_ref[bi] = decay * c_old + jnp.where(blockdiag, upd, 0.0)
    n_ref[bi] = decay * n_old + jnp.sum(kw, axis=0, keepdims=True)
    m_ref[bi] = m_last
    return y


def _mlstm_kernel(q_ref, k_ref, v_ref, o_ref, gi_ref, gf_ref, bi_ref, bf_ref, ng_ref,
                  c0_ref, n0_ref, m0_ref, y_ref, c_ref, n_ref, m_ref):
    nb, tt, w = q_ref.shape
    kscale = (w // N_HEADS) ** -0.5

    @pl.when(pl.program_id(1) == 0)
    def _():
        c_ref[...] = c0_ref[...]
        n_ref[...] = n0_ref[...]
        m_ref[...] = m0_ref[...]

    bias_i = bi_ref[...]
    bias_f = bf_ref[...]
    ng = ng_ref[...]
    _, et_sel = _head_selectors(w)

    def chunk(bi, rows, n_valid):
        pad = lambda r: _pad_rows(r[bi, rows, :], SPAT)
        gate = lambda r, bias: _split_dot(pad(r), et_sel, 3) + bias
        return _mlstm_chunk(pad(q_ref), pad(k_ref) * kscale, pad(v_ref), pad(o_ref),
                            gate(gi_ref, bias_i), gate(gf_ref, bias_f), ng, c_ref, n_ref, m_ref, bi, n_valid)

    if tt < SPAT:
        for bi, y in enumerate(_interleave([chunk(bi, slice(None), tt) for bi in range(nb)])):
            y_ref[bi] = y[0:tt].astype(y_ref.dtype)
    else:
        def body(i, carry):
            rows = pl.ds(pl.multiple_of(i * SPAT, SPAT), SPAT)
            for bi, y in enumerate(_interleave([chunk(bi, rows, SPAT) for bi in range(nb)])):
                y_ref[bi, rows, :] = y.astype(y_ref.dtype)
            return carry
        lax.fori_loop(0, tt // SPAT, body, 0)


def _mlstm(qkvo, gates, bias_i, bias_f, normg, c0, n0, m0, nb, tt):
    bsz, t, w4 = qkvo.shape
    w = w4 // 4
    blk = lambda c: pl.BlockSpec((nb, tt, w), lambda b, j: (b, j, c))
    gblk = lambda c: pl.BlockSpec((nb, tt, LANE), lambda b, j: (b, j, c))
    st = lambda shape: pl.BlockSpec((nb,) + shape, lambda b, j: (b, 0, 0))
    est = nb * (2 * 5 * tt * w * 4 + 4 * tt * LANE * 4 + 6 * w * w * 4 + 40 * SPAT * w * 4 + 24 * SPAT * SPAT * 4)
    return pl.pallas_call(
        _mlstm_kernel,
        out_shape=[jax.ShapeDtypeStruct((bsz, t, w), BF16),
                   jax.ShapeDtypeStruct((bsz, w, w), F32),
                   jax.ShapeDtypeStruct((bsz, 1, w), F32),
                   jax.ShapeDtypeStruct((bsz, 1, w), F32)],
        grid=(bsz // nb, t // tt),
        in_specs=[blk(0), blk(1), blk(2), blk(3), gblk(0), gblk(1),
                  _const_spec((1, w)), _const_spec((1, w)), _const_spec((1, w)),
                  st((w, w)), st((1, w)), st((1, w))],
        out_specs=[pl.BlockSpec((nb, tt, w), lambda b, j: (b, j, 0)), st((w, w)), st((1, w)), st((1, w))],
        name="mlstm",
        compiler_params=_params(("parallel", "arbitrary"), est),
    )(qkvo, qkvo, qkvo, qkvo, gates, gates, bias_i, bias_f, normg, c0, n0, m0)


def _ssd_chunk(xs, bm, cm, z, dt, a_x, dskip, normg, s_ref, bi):
    L, w = xs.shape
    dh = w // N_HEADS
    ng = s_ref.shape[1]
    gw = w // ng
    lmask = _head_lane_masks(L, w)
    tril = _tril(L)
    s_old = [s_ref[bi, g] for g in range(ng)]
    cmb = cm.astype(BF16)
    bmb = bm.astype(BF16)
    xdt = (xs * dt).astype(BF16)
    yield
    cs = _split_dot_left(tril.astype(BF16), dt * a_x, 3)
    cb = [_dot_nt(cmb[:, g * gw:(g + 1) * gw], bmb[:, g * gw:(g + 1) * gw]) for g in range(ng)]
    y_in = jnp.concatenate([jnp.dot(cmb[:, g * gw:(g + 1) * gw], s_old[g].astype(BF16),
                                    preferred_element_type=F32) for g in range(ng)], axis=1)
    yield
    cs_t = cs.T
    cs_last = cs[L - 1:L, :]
    wl = jnp.exp(cs_last - cs) * dt
    wx = (xs * wl).astype(BF16)
    yield
    y = jnp.exp(cs) * y_in
    for h in range(N_HEADS):
        dec = jnp.exp(jnp.where(tril, cs[:, h * dh:h * dh + 1] - cs_t[h * dh:h * dh + 1, :], NEG))
        mh = (cb[h * ng // N_HEADS] * dec).astype(BF16)
        y = y + jnp.where(lmask[h], jnp.dot(mh, xdt, preferred_element_type=F32), 0.0)
        yield
    dec_x = jnp.exp(cs_last)
    s_new = [dec_x[:, g * gw:(g + 1) * gw] * s_old[g] + _dot_tn(bmb[:, g * gw:(g + 1) * gw], wx[:, g * gw:(g + 1) * gw])
             for g in range(ng)]
    yield
    yc = y + dskip * xs
    out = _rmsnorm_rows(yc * (z * jax.nn.sigmoid(z)), normg)
    for g in range(ng):
        s_ref[bi, g] = s_new[g]
    return out


def _ssd_kernel(xbc_ref, z_ref, gd_ref, cw_ref, cb_ref, dtb_ref, a_ref, dskip_ref, ng_ref, conv0_ref, s0_ref,
                y_ref, conv_ref, s_ref, ext_ref):
    nb, tt, w = z_ref.shape
    hist = CONV_C - 1
    off = SUBLANE

    @pl.when(pl.program_id(1) == 0)
    def _():
        s_ref[...] = s0_ref[...]
        ext_ref[:, off - hist:off, :] = conv0_ref[...]

    cw = cw_ref[...]
    for bi in range(nb):
        ext_ref[bi, off:off + tt, :] = xbc_ref[bi]
        acc = cb_ref[...] + cw[CONV_C - 1:CONV_C, :] * ext_ref[bi, off:off + tt, :]
        for j in range(CONV_C - 1):
            sh = CONV_C - 1 - j
            acc = acc + cw[j:j + 1, :] * ext_ref[bi, off - sh:off - sh + tt, :]
        tail = ext_ref[bi, off + tt - hist:off + tt, :]
        ext_ref[bi, off - hist:off, :] = tail
        conv_ref[bi] = tail
        ext_ref[bi, off:off + tt, :] = acc * jax.nn.sigmoid(acc)

    dtb = dtb_ref[...]
    a_x = -jnp.exp(a_ref[...])
    dskip = dskip_ref[...]
    ng = ng_ref[...]
    _, et_sel = _head_selectors(w)

    def chunk(bi, ext_rows, rows, n_valid):
        act = _pad_rows(ext_ref[bi, ext_rows, :], SPAT)
        dt = jax.nn.softplus(_split_dot(_pad_rows(gd_ref[bi, rows, :], SPAT), et_sel, 3) + dtb)
        if n_valid < SPAT:
            dt = jnp.where(_iota((SPAT, w), 0) < n_valid, dt, 0.0)
        return _ssd_chunk(act[:, 0:w], act[:, w:2 * w], act[:, 2 * w:3 * w], _pad_rows(z_ref[bi, rows, :], SPAT),
                          dt, a_x, dskip, ng, s_ref, bi)

    if tt < SPAT:
        gens = [chunk(bi, slice(off, off + tt), slice(None), tt) for bi in range(nb)]
        for bi, y in enumerate(_interleave(gens)):
            y_ref[bi] = y[0:tt].astype(y_ref.dtype)
    else:
        def body(i, carry):
            start = pl.multiple_of(i * SPAT, SPAT)
            ext_rows = pl.ds(pl.multiple_of(start + off, SUBLANE), SPAT)
            gens = [chunk(bi, ext_rows, pl.ds(start, SPAT), SPAT) for bi in range(nb)]
            for bi, y in enumerate(_interleave(gens)):
                y_ref[bi, pl.ds(start, SPAT), :] = y.astype(y_ref.dtype)
            return carry
        lax.fori_loop(0, tt // SPAT, body, 0)


def _ssd(cin, gates, conv_w, conv_b, dtb, a_log_x, dskip, normg, conv0, s0, nb, tt):
    bsz, t, wtot = cin.shape
    w = dskip.shape[1]
    xw = wtot - w
    ng, n_c, gp = s0.shape[1:]
    st = lambda shape: pl.BlockSpec((nb,) + shape, lambda b, j: (b,) + (0,) * len(shape))
    est = nb * (2 * 2 * tt * wtot * 4 + 2 * tt * w * 4 + (tt + SUBLANE) * xw * 4 + 3 * tt * xw * 4 + 40 * SPAT * w * 4)
    return pl.pallas_call(
        _ssd_kernel,
        out_shape=[jax.ShapeDtypeStruct((bsz, t, w), BF16),
                   jax.ShapeDtypeStruct((bsz, CONV_C - 1, xw), F32),
                   jax.ShapeDtypeStruct(s0.shape, F32)],
        grid=(bsz // nb, t // tt),
        in_specs=[pl.BlockSpec((nb, tt, xw), lambda b, j: (b, j, 0)),
                  pl.BlockSpec((nb, tt, w), lambda b, j: (b, j, xw // w)),
                  pl.BlockSpec((nb, tt, LANE), lambda b, j: (b, j, 2)),
                  _const_spec((CONV_C, xw)), _const_spec((1, xw)), _const_spec((1, w)), _const_spec((1, w)),
                  _const_spec((1, w)), _const_spec((1, w)),
                  st((CONV_C - 1, xw)), st((ng, n_c, gp))],
        out_specs=[pl.BlockSpec((nb, tt, w), lambda b, j: (b, j, 0)), st((CONV_C - 1, xw)), st((ng, n_c, gp))],
        scratch_shapes=[pltpu.VMEM((nb, tt + SUBLANE, xw), F32)],
        name="ssd",
        compiler_params=_params(("parallel", "arbitrary"), est),
    )(cin, cin, gates, conv_w, conv_b, dtb, a_log_x, dskip, normg, conv0, s0)


def _gmlp_kernel(u_ref, v_ref, ng_ref, ws_ref, bs_ref, y_ref, *vn_refs):
    tt, w = u_ref.shape[1], u_ref.shape[2]
    u = jax.nn.gelu(u_ref[0])
    vr = jax.nn.gelu(v_ref[0])
    mu = jnp.mean(vr, axis=-1, keepdims=True)
    xc = vr - mu
    vn = xc * lax.rsqrt(jnp.mean(xc * xc, axis=-1, keepdims=True) + EPS) * ng_ref[...]
    for vn_ref in vn_refs:
        vn_ref[0] = vn
    rows = _iota((N_HEADS * SPAT, SPAT), 0) % SPAT
    wst = jnp.where(rows >= _iota((N_HEADS * SPAT, SPAT), 1), ws_ref[...], 0.0).astype(BF16)
    lmask = _head_lane_masks(SPAT, w)
    bias = bs_ref[...]
    vpad = _pad_rows(vn, -(-tt // SPAT) * SPAT).astype(BF16)
    for i in range(vpad.shape[0] // SPAT):
        fs = jnp.dot(wst, vpad[i * SPAT:(i + 1) * SPAT], preferred_element_type=F32)
        f = bias
        for g in range(N_HEADS):
            f = f + jnp.where(lmask[g], fs[g * SPAT:(g + 1) * SPAT], 0.0)
        n = min(SPAT, tt - i * SPAT)
        y_ref[0, i * SPAT:i * SPAT + n, :] = (u[i * SPAT:i * SPAT + n] * f[0:n]).astype(y_ref.dtype)


def _gmlp(din, normg, w_s, b_x, tt, emit_v):
    bsz, t, w2 = din.shape
    w = w2 // 2
    est = 2 * 4 * tt * w * 4 + 8 * tt * w * 4 + 4 * N_HEADS * SPAT * (SPAT + w) * 4
    out_shape = [jax.ShapeDtypeStruct((bsz, t, w), BF16)]
    if emit_v:
        out_shape.append(jax.ShapeDtypeStruct((bsz, t, w), F32))
    return pl.pallas_call(
        _gmlp_kernel,
        out_shape=out_shape,
        grid=(bsz, t // tt),
        in_specs=[pl.BlockSpec((1, tt, w), lambda b, j: (b, j, 0)),
                  pl.BlockSpec((1, tt, w), lambda b, j: (b, j, 1)),
                  _const_spec((1, w)), _const_spec((N_HEADS * SPAT, SPAT)), _const_spec((SPAT, w))],
        out_specs=[pl.BlockSpec((1, tt, w), lambda b, j: (b, j, 0))] * len(out_shape),
        name="spatial_gate",
        compiler_params=_params(("parallel", "parallel"), est),
    )(din, din, normg, w_s, b_x)


def _post_kernel(x_ref, ya_ref, yb_ref, yc_ref, yd_ref, mod_ref, g_ref, wo_ref, wg_ref, wu_ref, wd_ref,
                 cw_ref, cb_ref, f0_ref, gfin_ref, o_ref, fc_ref, ext_ref, *, final, fchunk):
    bb, tt, d = x_ref.shape
    m = bb * tt
    gw = ya_ref.shape[2]
    f = wg_ref.shape[1]
    hist = CONV_F - 1
    off = SUBLANE
    mod = mod_ref[...]
    mix = None
    for j, y_ref in enumerate((ya_ref, yb_ref, yc_ref, yd_ref)):
        p = jnp.dot(y_ref[...].reshape(m, gw), wo_ref[j * gw:(j + 1) * gw, :], preferred_element_type=F32)
        mix = p if mix is None else mix + p
    x1 = x_ref[...] + mod[:, :, 2 * d:3 * d] * mix.reshape(bb, tt, d)
    h2 = _rmsnorm_rows(x1, g_ref[...]) * (1.0 + mod[:, :, 4 * d:5 * d]) + mod[:, :, 3 * d:4 * d]
    hb = h2.reshape(m, d).astype(BF16)

    @pl.when(pl.program_id(1) == 0)
    def _():
        ext_ref[:, off - hist:off, :] = f0_ref[...]

    down = None
    for c0 in range(0, f, fchunk):
        cols = slice(c0, c0 + fchunk)
        g = jnp.dot(hb, wg_ref[:, cols], preferred_element_type=F32).reshape(bb, tt, fchunk)
        u = jnp.dot(hb, wu_ref[:, cols], preferred_element_type=F32)
        ext_ref[:, off:off + tt, cols] = g
        acc = cb_ref[:, cols] + cw_ref[CONV_F - 1:CONV_F, cols] * g
        for j in range(CONV_F - 1):
            sh = CONV_F - 1 - j
            acc = acc + cw_ref[j:j + 1, cols] * ext_ref[:, off - sh:off - sh + tt, cols]
        act = (acc * jax.nn.sigmoid(acc)).reshape(m, fchunk) * u
        p = jnp.dot(act.astype(BF16), wd_ref[cols, :], preferred_element_type=F32)
        down = p if down is None else down + p
    tail = ext_ref[:, off + tt - hist:off + tt, :]
    ext_ref[:, off - hist:off, :] = tail
    fc_ref[...] = tail
    x2 = x1 + mod[:, :, 5 * d:6 * d] * down.reshape(bb, tt, d)
    if final:
        x2 = _rmsnorm_rows(x2, gfin_ref[...])
    o_ref[...] = x2


def _post(x, ys, mod, g_ffn, w_out, w_gate, w_up, w_down, conv_w, conv_b, f0, g_final, bb, tt, final):
    bsz, t, d = x.shape
    gw = ys[0].shape[2]
    f = w_gate.shape[1]
    m = bb * tt
    tok = lambda wd: pl.BlockSpec((bb, tt, wd), lambda i, j: (i, j, 0))
    fchunk = 2 * LANE
    assert f % fchunk == 0
    est = (2 * 2 * m * d * 4 + 2 * 4 * m * gw * 2 + (d * d + 3 * d * f) * 2 + bb * (tt + SUBLANE) * f * 4
           + 8 * m * fchunk * 4 + 6 * m * d * 4)
    return pl.pallas_call(
        functools.partial(_post_kernel, final=final, fchunk=fchunk),
        out_shape=[jax.ShapeDtypeStruct((bsz, t, d), F32), jax.ShapeDtypeStruct((bsz, CONV_F - 1, f), F32)],
        grid=(bsz // bb, t // tt),
        in_specs=[tok(d), tok(gw), tok(gw), tok(gw), tok(gw),
                  pl.BlockSpec((bb, 1, mod.shape[2]), lambda i, j: (i, 0, 0)),
                  _const_spec((1, 1, d)), _const_spec((d, d)), _const_spec((d, f)), _const_spec((d, f)),
                  _const_spec((f, d)), _const_spec((CONV_F, f)), _const_spec((1, f)),
                  pl.BlockSpec((bb, CONV_F - 1, f), lambda i, j: (i, 0, 0)),
                  _const_spec((1, 1, d))],
        out_specs=[tok(d), pl.BlockSpec((bb, CONV_F - 1, f), lambda i, j: (i, 0, 0))],
        scratch_shapes=[pltpu.VMEM((bb, tt + SUBLANE, f), F32)],
        name="post_ffn",
        compiler_params=_params(("parallel", "arbitrary"), est),
    )(x, *ys, mod, g_ffn.reshape(1, 1, d), w_out, w_gate, w_up, w_down, conv_w, conv_b, f0, g_final.reshape(1, 1, d))


def _layer(x, mod, st, p, tiles, final, g_final, emit_v):
    bsz, t, d = x.shape
    gw = d // 4
    dh = gw // N_HEADS
    bb, tt, tmix, tpost, nb = tiles
    mod3 = mod.reshape(bsz, 1, mod.shape[1])
    qkv_a, qkvo_b, c_in, d_in, gates = _in_proj(x, mod3, p["g_mix"], p["w_in"], p["widths"], bb, tt)

    y_a = _attention(qkv_a, st["a_k"], st["a_v"], p["bias_tab"], tmix)
    y_b, c_new, n_new, m_new = _mlstm(qkvo_b, gates, p["bi"], p["bf"], p["b_norm_g"], st["b_c"], st["b_n"],
                                      st["b_m"], nb, tmix)
    y_c, conv_new, ssm_new = _ssd(c_in, gates, p["c_conv_w"], p["c_conv_b"], p["dtb"], p["a_log_x"], p["dskip"],
                                  p["c_norm_g"], st["c_conv"], st["c_ssm"], nb, tmix)
    y_d, *vn = _gmlp(d_in, p["d_norm_g"], p["d_w_s"], p["d_b_x"], tmix, emit_v)
    x_new, fconv_new = _post(x, (y_a, y_b, y_c, y_d), mod3, p["g_ffn"], p["w_out"], p["f_w_gate"], p["f_w_up"],
                             p["f_w_down"], p["f_conv_w"], p["f_conv_b"], st["f_conv"], g_final, bb, tpost, final)

    keep = min(A_WIN, t)
    new_k = qkv_a[:, t - keep:, gw:2 * gw].reshape(bsz, keep, N_HEADS, dh)
    new_v = qkv_a[:, t - keep:, 2 * gw:3 * gw].reshape(bsz, keep, N_HEADS, dh)
    idx = jnp.arange(N_HEADS)
    c_heads = c_new.reshape(bsz, N_HEADS, dh, N_HEADS, dh)[:, idx, :, idx, :]
    c_heads = jnp.moveaxis(c_heads, 0, 1)
    n_heads = n_new.reshape(bsz, N_HEADS, dh)
    m_heads = m_new[:, 0, ::dh]
    ssm = jnp.swapaxes(ssm_new, 2, 3).reshape(bsz, N_HEADS, gw // N_HEADS, ssm_new.shape[2])
    outs = (new_k, new_v, c_heads, n_heads, m_heads, ssm, conv_new, fconv_new, *vn)
    return x_new, outs


def _prep_layer(l, w_in, w_out, a_rel_bias, b_i_bias, b_f_bias, b_norm_g, c_conv_w, c_conv_b, c_dt_bias, c_a_log,
                c_d_skip, c_norm_g, d_norm_g, d_w_s, d_b_s, f_w_gate, f_w_up, f_conv_w, f_conv_b, f_w_down,
                g_norm_mix, g_norm_ffn):
    d = w_in.shape[1]
    gw = d // 4
    nh = b_i_bias.shape[1]
    xbc_w = c_conv_w.shape[2]
    sizes = (gw,) * 7 + (nh, nh, gw, xbc_w, c_dt_bias.shape[1], gw, gw)
    offs = [0]
    for s in sizes:
        offs.append(offs[-1] + s)
    moves, dst = [], 0
    for i in (0, 1, 2, 3, 4, 5, 6, 10, 9, 12, 13, 7, 8, 11):
        moves.append((offs[i], dst, sizes[i]))
        dst += -(-sizes[i] // LANE) * LANE
    widths = (3 * gw, 4 * gw, xbc_w + gw, 2 * gw, 3 * LANE)
    assert dst == sum(widths)
    heads_x = lambda v: jnp.repeat(v.astype(F32), gw // nh)[None, :]
    return dict(
        w_in=_regroup_cast(w_in, l, moves, dst), widths=widths, g_mix=g_norm_mix[l], g_ffn=g_norm_ffn[l],
        w_out=w_out[l].astype(BF16), bias_tab=_bias_table(a_rel_bias[l]),
        bi=heads_x(b_i_bias[l]), bf=heads_x(b_f_bias[l]), b_norm_g=b_norm_g[l][None, :],
        c_conv_w=c_conv_w[l], c_conv_b=c_conv_b[l][None, :], dtb=heads_x(c_dt_bias[l]),
        a_log_x=heads_x(c_a_log[l]), dskip=heads_x(c_d_skip[l]), c_norm_g=c_norm_g[l][None, :],
        d_norm_g=d_norm_g[l][None, :], d_w_s=d_w_s[l].reshape(-1, d_w_s.shape[-1]),
        d_b_x=jnp.repeat(d_b_s[l].T, gw // d_b_s.shape[1], axis=1),
        f_w_gate=f_w_gate[l].astype(BF16), f_w_up=f_w_up[l].astype(BF16), f_w_down=f_w_down[l].astype(BF16),
        f_conv_w=f_conv_w[l], f_conv_b=f_conv_b[l][None, :],
    )


def _blockdiag_state(c):
    bsz, nh, dh, _ = c.shape
    eye = jnp.eye(nh, dtype=c.dtype)
    return jnp.einsum("bhde,hg->bhdge", c, eye).reshape(bsz, nh * dh, nh * dh)


def kernel(x_prompt, x_sample, c_prompt, c_sample, cache_a_k, cache_a_v, state_b_c, state_b_n, state_b_m, state_c_ssm, state_c_conv, state_ffn_conv, w_ada, b_ada, g_norm_mix, g_norm_ffn, w_in, w_out, a_rel_bias, b_i_bias, b_f_bias, b_norm_g, c_conv_w, c_conv_b, c_dt_bias, c_a_log, c_d_skip, c_norm_g, d_norm_g, d_w_s, d_b_s, f_w_gate, f_w_up, f_conv_w, f_conv_b, f_w_down, g_final):
    depth = w_in.shape[0]
    bp, tp, d = x_prompt.shape
    bs, ts, _ = x_sample.shape
    gw = d // 4
    dh = gw // N_HEADS
    f = f_w_gate.shape[2]
    xbc_w = c_conv_w.shape[2]
    g_c = (xbc_w - gw) // 2 // (state_c_ssm.shape[-1])
    n_c = state_c_ssm.shape[-1]

    mod_all = _ada(jnp.concatenate([c_prompt, c_sample], axis=0), w_ada, b_ada)

    tmix_p = min(A_WIN, tp)
    nb = max(n for n in (4, 2, 1) if bp % n == 0 and bs % n == 0)
    tiles_p = (1, tmix_p, tmix_p, tmix_p, nb)
    tiles_s = (bs, ts, ts, ts, nb)
    zeros = lambda *shape: jnp.zeros(shape, F32)

    xp, xs = x_prompt, x_sample
    p_states, s_states = [], []
    for l in range(depth):
        p = _prep_layer(l, w_in, w_out, a_rel_bias, b_i_bias, b_f_bias, b_norm_g, c_conv_w, c_conv_b, c_dt_bias,
                        c_a_log, c_d_skip, c_norm_g, d_norm_g, d_w_s, d_b_s, f_w_gate, f_w_up, f_conv_w, f_conv_b,
                        f_w_down, g_norm_mix, g_norm_ffn)
        final = l == depth - 1
        st_p = dict(a_k=None, a_v=None, b_c=zeros(bp, gw, gw), b_n=zeros(bp, 1, gw), b_m=zeros(bp, 1, gw),
                    c_conv=zeros(bp, CONV_C - 1, xbc_w), c_ssm=zeros(bp, g_c, n_c, gw // g_c),
                    f_conv=zeros(bp, CONV_F - 1, f))
        xp, sp = _layer(xp, mod_all[l, :bp], st_p, p, tiles_p, final, g_final, False)
        st_s = dict(a_k=cache_a_k[l].reshape(bs, -1, gw), a_v=cache_a_v[l].reshape(bs, -1, gw),
                    b_c=_blockdiag_state(state_b_c[l]), b_n=state_b_n[l].reshape(bs, 1, gw),
                    b_m=jnp.repeat(state_b_m[l], dh, axis=1)[:, None, :],
                    c_conv=state_c_conv[l],
                    c_ssm=jnp.swapaxes(state_c_ssm[l].reshape(bs, g_c, gw // g_c, n_c), 2, 3),
                    f_conv=state_ffn_conv[l])
        xs, ss = _layer(xs, mod_all[l, bp:], st_s, p, tiles_s, final, g_final, True)
        p_states.append(sp)
        s_states.append(ss)

    stack = lambda states, i: jnp.stack([s[i] for s in states])
    return (xp, xs,
            *(stack(p_states, i) for i in range(8)),
            *(stack(s_states, i) for i in range(9)))
```

```python
import functools

import jax
import jax.numpy as jnp
from jax import lax
from jax.experimental import pallas as pl
from jax.experimental.pallas import tpu as pltpu

F32 = jnp.float32
BF16 = jnp.bfloat16

EPS = 1e-6
NEG = -1e30

CHUNK = 64
N_BAND = 8
A_WIN = N_BAND * CHUNK
REL_CLIP = 128
N_HEADS = 4
SPAT = 128
CONV_C = 4
CONV_F = 3
LANE = 128
GATE_I, GATE_F, GATE_DT = 0, LANE, 2 * LANE
GATE_W = 3 * LANE
SUBLANE = 8
VMEM_CAP = 64 * 1024 * 1024


def _vmem_limit(nbytes):
    return int(min(max(nbytes, 16 * 1024 * 1024), VMEM_CAP - 8 * 1024 * 1024))


def _params(sem, nbytes):
    return pltpu.CompilerParams(dimension_semantics=sem, vmem_limit_bytes=_vmem_limit(nbytes))


def _const_spec(shape):
    nd = len(shape)
    return pl.BlockSpec(shape, lambda *_: (0,) * nd, pipeline_mode=pl.Buffered(1))


def _iota(shape, dim):
    return lax.broadcasted_iota(jnp.int32, shape, dim)


def _split_dot(x, e, parts):
    acc = None
    r = x
    for i in range(parts):
        hi = r.astype(BF16)
        d = jnp.dot(hi, e, preferred_element_type=F32)
        acc = d if acc is None else acc + d
        if i + 1 < parts:
            r = r - hi.astype(F32)
    return acc


def _split_dot_left(e, x, parts):
    acc = None
    r = x
    for i in range(parts):
        hi = r.astype(BF16)
        d = jnp.dot(e, hi, preferred_element_type=F32)
        acc = d if acc is None else acc + d
        if i + 1 < parts:
            r = r - hi.astype(F32)
    return acc


def _dot_nt(a, b):
    return lax.dot_general(a, b, (((1,), (1,)), ((), ())), preferred_element_type=F32)


def _dot_tn(a, b):
    return lax.dot_general(a, b, (((0,), (0,)), ((), ())), preferred_element_type=F32)


def _head_expander(width, first):
    dh = width // N_HEADS
    return (_iota((LANE, width), 1) // dh == _iota((LANE, width), 0) - first).astype(BF16)


def _head_lane_masks(rows, width):
    dh = width // N_HEADS
    lane = _iota((rows, width), 1)
    return [(lane >= h * dh) & (lane < (h + 1) * dh) for h in range(N_HEADS)]


def _tril(n):
    return _iota((n, n), 0) >= _iota((n, n), 1)


def _pad_rows(x, rows):
    if x.shape[0] == rows:
        return x
    return jnp.concatenate([x, jnp.zeros((rows - x.shape[0], x.shape[1]), x.dtype)], axis=0)


def _interleave(gens):
    results = [None] * len(gens)
    live = list(range(len(gens)))
    while live:
        for i in list(live):
            try:
                next(gens[i])
            except StopIteration as stop:
                results[i] = stop.value
                live.remove(i)
    return results


def _rmsnorm_rows(x, g):
    return x * lax.rsqrt(jnp.mean(x * x, axis=-1, keepdims=True) + EPS) * g


def _regroup_kernel(w_ref, o_ref, *, moves, packed):
    w = w_ref[0]
    rows = w.shape[0]
    for src, dst, n in moves:
        o_ref[:, dst:dst + n] = w[:, src:src + n].astype(o_ref.dtype)
    dst, pieces = packed
    parts, lane = [], 0
    for src, first, n in pieces:
        if first > lane:
            parts.append(jnp.zeros((rows, first - lane), F32))
        parts.append(w[:, src:src + n])
        lane = first + n
    parts.append(jnp.zeros((rows, GATE_W - lane), F32))
    o_ref[:, dst:dst + GATE_W] = jnp.concatenate(parts, axis=1).astype(o_ref.dtype)


def _regroup_cast(w_all, layer, moves, packed, ncols, row_block=256):
    _, r, c = w_all.shape
    return pl.pallas_call(
        functools.partial(_regroup_kernel, moves=tuple(moves), packed=packed),
        out_shape=jax.ShapeDtypeStruct((r, ncols), BF16),
        grid=(r // row_block,),
        in_specs=[pl.BlockSpec((1, row_block, c), lambda i: (layer, i, 0))],
        out_specs=pl.BlockSpec((row_block, ncols), lambda i: (i, 0)),
        name="regroup_cast",
        compiler_params=_params(("parallel",), 2 * row_block * (c * 4 + ncols * 2) + row_block * c * 4),
    )(w_all)


def _ada_kernel(c_ref, w_ref, b_ref, o_ref):
    c = c_ref[...]
    h = (c * jax.nn.sigmoid(c)).astype(BF16)
    o_ref[0] = jnp.dot(h, w_ref[0].astype(BF16), preferred_element_type=F32) + b_ref[0]


def _ada(c_all, w_ada, b_ada):
    depth, d, n6 = w_ada.shape
    r = c_all.shape[0]
    tn = d
    return pl.pallas_call(
        _ada_kernel,
        out_shape=jax.ShapeDtypeStruct((depth, r, n6), F32),
        grid=(depth, n6 // tn),
        in_specs=[pl.BlockSpec((r, d), lambda l, j: (0, 0)),
                  pl.BlockSpec((1, d, tn), lambda l, j: (l, 0, j)),
                  pl.BlockSpec((1, 1, tn), lambda l, j: (l, 0, j))],
        out_specs=pl.BlockSpec((1, r, tn), lambda l, j: (l, 0, j)),
        name="ada_mod",
        compiler_params=_params(("parallel", "parallel"), 4 * (2 * d * tn * 4 + 2 * r * tn * 4 + r * d * 4)),
    )(c_all, w_ada, b_ada.reshape(depth, 1, n6))


def _bias_kernel(rb_ref, o_ref, *, lo, hi):
    nh, lq, lk = o_ref.shape
    idx = jnp.clip(A_WIN + _iota((lq, lk), 0) - _iota((lq, lk), 1), -REL_CLIP, REL_CLIP) + REL_CLIP
    for h in range(nh):
        def body(r, acc, h=h):
            return jnp.where(idx == r, rb_ref[h, r], acc)
        o_ref[h] = lax.fori_loop(lo, hi + 1, body, jnp.zeros((lq, lk), F32))


def _bias_table(rel_bias):
    nh = rel_bias.shape[0]
    band = (N_BAND + 1) * CHUNK
    lo = max(A_WIN - (band - 1), -REL_CLIP) + REL_CLIP
    hi = min(A_WIN + CHUNK - 1, REL_CLIP) + REL_CLIP
    return pl.pallas_call(
        functools.partial(_bias_kernel, lo=lo, hi=hi),
        out_shape=jax.ShapeDtypeStruct((nh, CHUNK, band), F32),
        in_specs=[pl.BlockSpec(memory_space=pltpu.SMEM)],
        out_specs=pl.BlockSpec(memory_space=pltpu.VMEM),
        name="rel_bias_table",
    )(rel_bias)


def _in_kernel(x_ref, mod_ref, g_ref, w_ref, *o_refs, col_starts):
    bb, tt, d = x_ref.shape
    x = x_ref[...]
    mod = mod_ref[...]
    h = _rmsnorm_rows(x, g_ref[...]) * (1.0 + mod[:, :, d:2 * d]) + mod[:, :, 0:d]
    hb = h.reshape(bb * tt, d).astype(BF16)
    for o_ref, (a, b) in zip(o_refs, col_starts):
        o_ref[...] = jnp.dot(hb, w_ref[:, a:b], preferred_element_type=F32).reshape(bb, tt, b - a)


def _in_proj(x, mod, g, w, widths, bb, tt):
    bsz, t, d = x.shape
    ncols = w.shape[1]
    starts, a = [], 0
    for wd in widths:
        starts.append((a, a + wd))
        a += wd
    m = bb * tt
    est = 2 * m * d * 4 + 2 * d * ncols * 2 + 2 * m * ncols * 4 + 3 * m * d * 4 + m * max(widths) * 4
    return pl.pallas_call(
        functools.partial(_in_kernel, col_starts=tuple(starts)),
        out_shape=[jax.ShapeDtypeStruct((bsz, t, wd), F32) for wd in widths],
        grid=(bsz // bb, t // tt),
        in_specs=[pl.BlockSpec((bb, tt, d), lambda i, j: (i, j, 0)),
                  pl.BlockSpec((bb, 1, mod.shape[2]), lambda i, j: (i, 0, 0)),
                  _const_spec((1, 1, d)),
                  _const_spec((d, ncols))],
        out_specs=[pl.BlockSpec((bb, tt, wd), lambda i, j: (i, j, 0)) for wd in widths],
        name="in_proj",
        compiler_params=_params(("parallel", "parallel"), est),
    )(x, mod, g.reshape(1, 1, d), w)


def _attend_chunk(qc, kb, vb, bias, kvalid, masks_q, masks_o):
    lq = qc.shape[0]
    qs = jnp.concatenate([jnp.where(mk, qc, 0.0) for mk in masks_q], axis=0).astype(BF16)
    s = _dot_nt(qs, kb) + bias
    yield
    if kvalid is not None:
        s = jnp.where(kvalid, s, NEG)
    e = jnp.exp(s - jnp.max(s, axis=-1, keepdims=True))
    yield
    o = jnp.dot(e.astype(BF16), vb, preferred_element_type=F32)
    yield
    o = o * (1.0 / jnp.sum(e, axis=-1, keepdims=True))
    out = jnp.where(masks_o[0], o[0:lq], 0.0)
    for h in range(1, N_HEADS):
        out = out + jnp.where(masks_o[h], o[h * lq:(h + 1) * lq], 0.0)
    return out


def _attn_kernel(q_ref, k_ref, v_ref, kh_ref, vh_ref, bias_ref, y_ref, kk_ref, vv_ref, *, hist_is_cache, t_valid):
    tq, w = q_ref.shape[1], q_ref.shape[2]
    tpad = kk_ref.shape[0] - A_WIN
    scale = (w // N_HEADS) ** -0.5
    kk_ref[0:A_WIN, :] = kh_ref[0].astype(BF16)
    vv_ref[0:A_WIN, :] = vh_ref[0].astype(BF16)
    kk_ref[A_WIN:A_WIN + tpad, :] = _pad_rows(k_ref[0], tpad).astype(BF16)
    vv_ref[A_WIN:A_WIN + tpad, :] = _pad_rows(v_ref[0], tpad).astype(BF16)
    masks = _head_lane_masks(CHUNK, w)
    bias = bias_ref[...]
    band = (N_BAND + 1) * CHUNK
    slot = _iota((1, band), 1)
    hist_ok = jnp.logical_or(pl.program_id(1) > 0, hist_is_cache)
    q_all = _pad_rows(q_ref[0], tpad) * scale
    gens = []
    for i in range(tpad // CHUNK):
        base = i * CHUNK
        pos = slot + base
        kvalid = (pos < A_WIN + t_valid) & ((pos >= A_WIN) | hist_ok)
        gens.append(_attend_chunk(q_all[base:base + CHUNK], kk_ref[base:base + band, :], vv_ref[base:base + band, :],
                                  bias, kvalid, masks, masks))
    for i, out in enumerate(_interleave(gens)):
        base = i * CHUNK
        rows = min(CHUNK, tq - base)
        y_ref[0, base:base + rows, :] = out[0:rows].astype(y_ref.dtype)


def _attention(qkv, k_hist, v_hist, bias_tab, tq):
    bsz, t, w3 = qkv.shape
    w = w3 // 3
    hist_is_cache = k_hist is not None
    nt = t // tq
    tpad = -(-tq // CHUNK) * CHUNK
    band = (N_BAND + 1) * CHUNK
    if hist_is_cache:
        assert nt == 1
        hist_specs = [pl.BlockSpec((1, A_WIN, w), lambda b, j: (b, 0, 0))] * 2
        hist_args = (k_hist, v_hist)
    else:
        assert tq == A_WIN
        hist_specs = [pl.BlockSpec((1, tq, w), lambda b, j: (b, jnp.maximum(j - 1, 0), 1)),
                      pl.BlockSpec((1, tq, w), lambda b, j: (b, jnp.maximum(j - 1, 0), 2))]
        hist_args = (qkv, qkv)
    est = 2 * 5 * tpad * w * 4 + 2 * tq * w * 4 + 2 * (A_WIN + tpad) * w * 2 + 12 * N_HEADS * CHUNK * band * 4
    return pl.pallas_call(
        functools.partial(_attn_kernel, hist_is_cache=hist_is_cache, t_valid=tq if nt == 1 else tpad),
        out_shape=jax.ShapeDtypeStruct((bsz, t, w), BF16),
        grid=(bsz, nt),
        in_specs=[pl.BlockSpec((1, tq, w), lambda b, j: (b, j, 0)),
                  pl.BlockSpec((1, tq, w), lambda b, j: (b, j, 1)),
                  pl.BlockSpec((1, tq, w), lambda b, j: (b, j, 2)),
                  *hist_specs,
                  _const_spec((N_HEADS * CHUNK, band))],
        out_specs=pl.BlockSpec((1, tq, w), lambda b, j: (b, j, 0)),
        scratch_shapes=[pltpu.VMEM((A_WIN + tpad, w), BF16), pltpu.VMEM((A_WIN + tpad, w), BF16)],
        name="band_attention",
        compiler_params=_params(("parallel", "parallel"), est),
    )(qkv, qkv, qkv, *hist_args, bias_tab.reshape(N_HEADS * CHUNK, band))


def _mlstm_chunk(q, k, v, og, gi, gf, normg, c_ref, n_ref, m_ref, bi, n_valid, live=None):
    L, w = q.shape
    dh = w // N_HEADS
    lmask = _head_lane_masks(L, w)
    tril = _tril(L)
    blockdiag = (_iota((w, w), 0) // dh) == (_iota((w, w), 1) // dh)
    bd = blockdiag.astype(BF16)

    m_prev = m_ref[bi]
    c_old = c_ref[bi]
    n_old = n_ref[bi]
    qb16 = q.astype(BF16)
    kb16 = k.astype(BF16)
    vb16 = v.astype(BF16)
    lf = jax.nn.log_sigmoid(gf)
    yield
    b = _split_dot_left(tril.astype(BF16), lf, 3)
    q_c = jnp.dot(qb16, c_old.astype(BF16), preferred_element_type=F32)
    q_n = _split_dot(q * n_old, bd, 2)
    yield
    qk = [_dot_nt(jnp.where(lmask[h], q, 0.0).astype(BF16), kb16) for h in range(N_HEADS)]
    u = gi - b
    u_t = u.T
    yield
    cm = jnp.zeros((L, w), F32)
    for h in range(N_HEADS):
        cmh = jnp.max(jnp.where(tril, u_t[h * dh:h * dh + 1, :], NEG), axis=1, keepdims=True)
        cm = jnp.where(lmask[h], cmh, cm)
    mx = jnp.maximum(m_prev, cm)
    m_t = b + mx
    inter = jnp.exp(m_prev - mx)
    yield
    num = jnp.zeros((L, w), F32)
    rs = jnp.zeros((L, w), F32)
    for h in range(N_HEADS):
        arg = jnp.where(tril, u_t[h * dh:h * dh + 1, :] - mx[:, h * dh:h * dh + 1], NEG)
        wqk = jnp.exp(arg) * qk[h]
        rs = jnp.where(lmask[h], jnp.sum(wqk, axis=1, keepdims=True), rs)
        nv = jnp.dot(wqk.astype(BF16), vb16, preferred_element_type=F32)
        num = num + jnp.where(lmask[h], nv, 0.0)
        yield

    last = n_valid - 1
    m_last = m_t[last:last + 1, :]
    b_last = b[last:last + 1, :]
    decay = jnp.exp(b_last + m_prev - m_last)
    ws = jnp.exp(u + (b_last - m_last))
    if n_valid < L:
        ws = jnp.where(_iota((L, w), 0) < n_valid, ws, 0.0)
    kw = k * ws
    upd = _dot_tn(kw.astype(BF16), vb16)
    yield
    den = inter * q_n + rs
    hout = (inter * q_c + num) / jnp.maximum(jnp.abs(den), jnp.exp(-m_t))
    mu = _split_dot(hout, bd, 2) * (1.0 / dh)
    yield
    xc = hout - mu
    var = _split_dot(xc * xc, bd, 2) * (1.0 / dh)
    yield
    y = jax.nn.sigmoid(og) * (xc * lax.rsqrt(var + EPS) * normg)
    keep = (lambda new, old: new) if live is None else (lambda new, old: jnp.where(live, new, old))
    c_ref[bi] = keep(decay * c_old + jnp.where(blockdiag, upd, 0.0), c_old)
    n_ref[bi] = keep(decay * n_old + jnp.sum(kw, axis=0, keepdims=True), n_old)
    m_ref[bi] = keep(m_last, m_prev)
    return y


def _mlstm_kernel(q_ref, k_ref, v_ref, o_ref, g_ref, bi_ref, bf_ref, ng_ref,
                  c0_ref, n0_ref, m0_ref, y_ref, c_ref, n_ref, m_ref):
    nb, tt, w = q_ref.shape
    kscale = (w // N_HEADS) ** -0.5

    @pl.when(pl.program_id(1) == 0)
    def _():
        c_ref[...] = c0_ref[...]
        n_ref[...] = n0_ref[...]
        m_ref[...] = m0_ref[...]

    bias_i = bi_ref[...]
    bias_f = bf_ref[...]
    ng = ng_ref[...]
    ex = _head_expander(w, 0)

    def chunk(bi, rows, n_valid):
        pad = lambda r: _pad_rows(r[bi, rows, :], SPAT)
        g = pad(g_ref)
        return _mlstm_chunk(pad(q_ref), pad(k_ref) * kscale, pad(v_ref), pad(o_ref),
                            _split_dot(g[:, GATE_I:GATE_I + LANE], ex, 3) + bias_i,
                            _split_dot(g[:, GATE_F:GATE_F + LANE], ex, 3) + bias_f,
                            ng, c_ref, n_ref, m_ref, bi, n_valid)

    if tt < SPAT:
        for bi, y in enumerate(_interleave([chunk(bi, slice(None), tt) for bi in range(nb)])):
            y_ref[bi] = y[0:tt].astype(y_ref.dtype)
    else:
        def body(i, carry):
            rows = pl.ds(pl.multiple_of(i * SPAT, SPAT), SPAT)
            for bi, y in enumerate(_interleave([chunk(bi, rows, SPAT) for bi in range(nb)])):
                y_ref[bi, rows, :] = y.astype(y_ref.dtype)
            return carry
        lax.fori_loop(0, tt // SPAT, body, 0)


def _mlstm(qkvo, gates, bias_i, bias_f, normg, c0, n0, m0, nb, tt):
    bsz, t, w4 = qkvo.shape
    w = w4 // 4
    blk = lambda c: pl.BlockSpec((nb, tt, w), lambda b, j: (b, j, c))
    st = lambda shape: pl.BlockSpec((nb,) + shape, lambda b, j: (b, 0, 0))
    est = nb * (2 * 5 * tt * w * 4 + 4 * tt * LANE * 4 + 6 * w * w * 4 + 40 * SPAT * w * 4 + 24 * SPAT * SPAT * 4)
    return pl.pallas_call(
        _mlstm_kernel,
        out_shape=[jax.ShapeDtypeStruct((bsz, t, w), BF16),
                   jax.ShapeDtypeStruct((bsz, w, w), F32),
                   jax.ShapeDtypeStruct((bsz, 1, w), F32),
                   jax.ShapeDtypeStruct((bsz, 1, w), F32)],
        grid=(bsz // nb, t // tt),
        in_specs=[blk(0), blk(1), blk(2), blk(3), pl.BlockSpec((nb, tt, GATE_W), lambda b, j: (b, j, 0)),
                  _const_spec((1, w)), _const_spec((1, w)), _const_spec((1, w)),
                  st((w, w)), st((1, w)), st((1, w))],
        out_specs=[pl.BlockSpec((nb, tt, w), lambda b, j: (b, j, 0)), st((w, w)), st((1, w)), st((1, w))],
        name="mlstm",
        compiler_params=_params(("parallel", "arbitrary"), est),
    )(qkvo, qkvo, qkvo, qkvo, gates, bias_i, bias_f, normg, c0, n0, m0)


def _ssd_chunk(xs, bm, cm, z, dt, a_x, dskip, normg, s_ref, bi, live=None):
    L, w = xs.shape
    dh = w // N_HEADS
    ng = s_ref.shape[1]
    gw = w // ng
    lmask = _head_lane_masks(L, w)
    tril = _tril(L)
    s_old = [s_ref[bi, g] for g in range(ng)]
    cmb = cm.astype(BF16)
    bmb = bm.astype(BF16)
    xdt = (xs * dt).astype(BF16)
    yield
    cs = _split_dot_left(tril.astype(BF16), dt * a_x, 3)
    cb = [_dot_nt(cmb[:, g * gw:(g + 1) * gw], bmb[:, g * gw:(g + 1) * gw]) for g in range(ng)]
    y_in = jnp.concatenate([jnp.dot(cmb[:, g * gw:(g + 1) * gw], s_old[g].astype(BF16),
                                    preferred_element_type=F32) for g in range(ng)], axis=1)
    yield
    cs_t = cs.T
    cs_last = cs[L - 1:L, :]
    wl = jnp.exp(cs_last - cs) * dt
    wx = (xs * wl).astype(BF16)
    yield
    y = jnp.exp(cs) * y_in
    for h in range(N_HEADS):
        dec = jnp.exp(jnp.where(tril, cs[:, h * dh:h * dh + 1] - cs_t[h * dh:h * dh + 1, :], NEG))
        mh = (cb[h * ng // N_HEADS] * dec).astype(BF16)
        y = y + jnp.where(lmask[h], jnp.dot(mh, xdt, preferred_element_type=F32), 0.0)
        yield
    dec_x = jnp.exp(cs_last)
    s_new = [dec_x[:, g * gw:(g + 1) * gw] * s_old[g] + _dot_tn(bmb[:, g * gw:(g + 1) * gw], wx[:, g * gw:(g + 1) * gw])
             for g in range(ng)]
    yield
    yc = y + dskip * xs
    out = _rmsnorm_rows(yc * (z * jax.nn.sigmoid(z)), normg)
    for g in range(ng):
        s_ref[bi, g] = s_new[g] if live is None else jnp.where(live, s_new[g], s_old[g])
    return out


def _ssd_kernel(xbc_ref, z_ref, gd_ref, cw_ref, cb_ref, dtb_ref, a_ref, dskip_ref, ng_ref, conv0_ref, s0_ref,
                y_ref, conv_ref, s_ref, ext_ref):
    nb, tt, w = z_ref.shape
    hist = CONV_C - 1
    off = SUBLANE

    @pl.when(pl.program_id(1) == 0)
    def _():
        s_ref[...] = s0_ref[...]
        ext_ref[:, off - hist:off, :] = conv0_ref[...]

    cw = cw_ref[...]
    for bi in range(nb):
        ext_ref[bi, off:off + tt, :] = xbc_ref[bi]
        acc = cb_ref[...] + cw[CONV_C - 1:CONV_C, :] * ext_ref[bi, off:off + tt, :]
        for j in range(CONV_C - 1):
            sh = CONV_C - 1 - j
            acc = acc + cw[j:j + 1, :] * ext_ref[bi, off - sh:off - sh + tt, :]
        tail = ext_ref[bi, off + tt - hist:off + tt, :]
        ext_ref[bi, off - hist:off, :] = tail
        conv_ref[bi] = tail
        ext_ref[bi, off:off + tt, :] = acc * jax.nn.sigmoid(acc)

    dtb = dtb_ref[...]
    a_x = -jnp.exp(a_ref[...])
    dskip = dskip_ref[...]
    ng = ng_ref[...]
    ex = _head_expander(w, 0)

    def chunk(bi, ext_rows, rows, n_valid):
        act = _pad_rows(ext_ref[bi, ext_rows, :], SPAT)
        dt = jax.nn.softplus(_split_dot(_pad_rows(gd_ref[bi, rows, :], SPAT), ex, 3) + dtb)
        if n_valid < SPAT:
            dt = jnp.where(_iota((SPAT, w), 0) < n_valid, dt, 0.0)
        return _ssd_chunk(act[:, 0:w], act[:, w:2 * w], act[:, 2 * w:3 * w], _pad_rows(z_ref[bi, rows, :], SPAT),
                          dt, a_x, dskip, ng, s_ref, bi)

    if tt < SPAT:
        gens = [chunk(bi, slice(off, off + tt), slice(None), tt) for bi in range(nb)]
        for bi, y in enumerate(_interleave(gens)):
            y_ref[bi] = y[0:tt].astype(y_ref.dtype)
    else:
        def body(i, carry):
            start = pl.multiple_of(i * SPAT, SPAT)
            ext_rows = pl.ds(pl.multiple_of(start + off, SUBLANE), SPAT)
            gens = [chunk(bi, ext_rows, pl.ds(start, SPAT), SPAT) for bi in range(nb)]
            for bi, y in enumerate(_interleave(gens)):
                y_ref[bi, pl.ds(start, SPAT), :] = y.astype(y_ref.dtype)
            return carry
        lax.fori_loop(0, tt // SPAT, body, 0)


def _ssd(cin, gates, conv_w, conv_b, dtb, a_log_x, dskip, normg, conv0, s0, nb, tt):
    bsz, t, wtot = cin.shape
    w = dskip.shape[1]
    xw = wtot - w
    ng, n_c, gp = s0.shape[1:]
    st = lambda shape: pl.BlockSpec((nb,) + shape, lambda b, j: (b,) + (0,) * len(shape))
    est = nb * (2 * 2 * tt * wtot * 4 + 2 * tt * w * 4 + (tt + SUBLANE) * xw * 4 + 3 * tt * xw * 4 + 40 * SPAT * w * 4)
    return pl.pallas_call(
        _ssd_kernel,
        out_shape=[jax.ShapeDtypeStruct((bsz, t, w), BF16),
                   jax.ShapeDtypeStruct((bsz, CONV_C - 1, xw), F32),
                   jax.ShapeDtypeStruct(s0.shape, F32)],
        grid=(bsz // nb, t // tt),
        in_specs=[pl.BlockSpec((nb, tt, xw), lambda b, j: (b, j, 0)),
                  pl.BlockSpec((nb, tt, w), lambda b, j: (b, j, xw // w)),
                  pl.BlockSpec((nb, tt, LANE), lambda b, j: (b, j, GATE_DT // LANE)),
                  _const_spec((CONV_C, xw)), _const_spec((1, xw)), _const_spec((1, w)), _const_spec((1, w)),
                  _const_spec((1, w)), _const_spec((1, w)),
                  st((CONV_C - 1, xw)), st((ng, n_c, gp))],
        out_specs=[pl.BlockSpec((nb, tt, w), lambda b, j: (b, j, 0)), st((CONV_C - 1, xw)), st((ng, n_c, gp))],
        scratch_shapes=[pltpu.VMEM((nb, tt + SUBLANE, xw), F32)],
        name="ssd",
        compiler_params=_params(("parallel", "arbitrary"), est),
    )(cin, cin, gates, conv_w, conv_b, dtb, a_log_x, dskip, normg, conv0, s0)


def _gmlp_kernel(u_ref, v_ref, ng_ref, ws_ref, bs_ref, y_ref, *vn_refs):
    tt, w = u_ref.shape[1], u_ref.shape[2]
    u = jax.nn.gelu(u_ref[0])
    vr = jax.nn.gelu(v_ref[0])
    mu = jnp.mean(vr, axis=-1, keepdims=True)
    xc = vr - mu
    vn = xc * lax.rsqrt(jnp.mean(xc * xc, axis=-1, keepdims=True) + EPS) * ng_ref[...]
    for vn_ref in vn_refs:
        vn_ref[0] = vn
    rows = _iota((N_HEADS * SPAT, SPAT), 0) % SPAT
    wst = jnp.where(rows >= _iota((N_HEADS * SPAT, SPAT), 1), ws_ref[...], 0.0).astype(BF16)
    lmask = _head_lane_masks(SPAT, w)
    bias = bs_ref[...]
    vpad = _pad_rows(vn, -(-tt // SPAT) * SPAT).astype(BF16)
    for i in range(vpad.shape[0] // SPAT):
        fs = jnp.dot(wst, vpad[i * SPAT:(i + 1) * SPAT], preferred_element_type=F32)
        f = bias
        for g in range(N_HEADS):
            f = f + jnp.where(lmask[g], fs[g * SPAT:(g + 1) * SPAT], 0.0)
        n = min(SPAT, tt - i * SPAT)
        y_ref[0, i * SPAT:i * SPAT + n, :] = (u[i * SPAT:i * SPAT + n] * f[0:n]).astype(y_ref.dtype)


def _gmlp(din, normg, w_s, b_x, tt, emit_v):
    bsz, t, w2 = din.shape
    w = w2 // 2
    est = 2 * 4 * tt * w * 4 + 8 * tt * w * 4 + 4 * N_HEADS * SPAT * (SPAT + w) * 4
    out_shape = [jax.ShapeDtypeStruct((bsz, t, w), BF16)]
    if emit_v:
        out_shape.append(jax.ShapeDtypeStruct((bsz, t, w), F32))
    return pl.pallas_call(
        _gmlp_kernel,
        out_shape=out_shape,
        grid=(bsz, t // tt),
        in_specs=[pl.BlockSpec((1, tt, w), lambda b, j: (b, j, 0)),
                  pl.BlockSpec((1, tt, w), lambda b, j: (b, j, 1)),
                  _const_spec((1, w)), _const_spec((N_HEADS * SPAT, SPAT)), _const_spec((SPAT, w))],
        out_specs=[pl.BlockSpec((1, tt, w), lambda b, j: (b, j, 0))] * len(out_shape),
        name="spatial_gate",
        compiler_params=_params(("parallel", "parallel"), est),
    )(din, din, normg, w_s, b_x)


def _post_kernel(x_ref, ya_ref, yb_ref, yc_ref, yd_ref, mod_ref, g_ref, wo_ref, wg_ref, wu_ref, wd_ref,
                 cw_ref, cb_ref, f0_ref, gfin_ref, o_ref, fc_ref, ext_ref, *, final, fchunk):
    bb, tt, d = x_ref.shape
    m = bb * tt
    gw = ya_ref.shape[2]
    f = wg_ref.shape[1]
    hist = CONV_F - 1
    off = SUBLANE
    mod = mod_ref[...]
    mix = None
    for j, y_ref in enumerate((ya_ref, yb_ref, yc_ref, yd_ref)):
        p = jnp.dot(y_ref[...].reshape(m, gw), wo_ref[j * gw:(j + 1) * gw, :], preferred_element_type=F32)
        mix = p if mix is None else mix + p
    x1 = x_ref[...] + mod[:, :, 2 * d:3 * d] * mix.reshape(bb, tt, d)
    h2 = _rmsnorm_rows(x1, g_ref[...]) * (1.0 + mod[:, :, 4 * d:5 * d]) + mod[:, :, 3 * d:4 * d]
    hb = h2.reshape(m, d).astype(BF16)

    @pl.when(pl.program_id(1) == 0)
    def _():
        ext_ref[:, off - hist:off, :] = f0_ref[...]

    down = None
    for c0 in range(0, f, fchunk):
        cols = slice(c0, c0 + fchunk)
        g = jnp.dot(hb, wg_ref[:, cols], preferred_element_type=F32).reshape(bb, tt, fchunk)
        u = jnp.dot(hb, wu_ref[:, cols], preferred_element_type=F32)
        ext_ref[:, off:off + tt, cols] = g
        acc = cb_ref[:, cols] + cw_ref[CONV_F - 1:CONV_F, cols] * g
        for j in range(CONV_F - 1):
            sh = CONV_F - 1 - j
            acc = acc + cw_ref[j:j + 1, cols] * ext_ref[:, off - sh:off - sh + tt, cols]
        act = (acc * jax.nn.sigmoid(acc)).reshape(m, fchunk) * u
        p = jnp.dot(act.astype(BF16), wd_ref[cols, :], preferred_element_type=F32)
        down = p if down is None else down + p
    tail = ext_ref[:, off + tt - hist:off + tt, :]
    ext_ref[:, off - hist:off, :] = tail
    fc_ref[...] = tail
    x2 = x1 + mod[:, :, 5 * d:6 * d] * down.reshape(bb, tt, d)
    if final:
        x2 = _rmsnorm_rows(x2, gfin_ref[...])
    o_ref[...] = x2


def _post(x, ys, mod, g_ffn, w_out, w_gate, w_up, w_down, conv_w, conv_b, f0, g_final, bb, tt, final):
    bsz, t, d = x.shape
    gw = ys[0].shape[2]
    f = w_gate.shape[1]
    m = bb * tt
    tok = lambda wd: pl.BlockSpec((bb, tt, wd), lambda i, j: (i, j, 0))
    fchunk = 2 * LANE
    assert f % fchunk == 0
    est = (2 * 2 * m * d * 4 + 2 * 4 * m * gw * 2 + (d * d + 3 * d * f) * 2 + bb * (tt + SUBLANE) * f * 4
           + 8 * m * fchunk * 4 + 6 * m * d * 4)
    return pl.pallas_call(
        functools.partial(_post_kernel, final=final, fchunk=fchunk),
        out_shape=[jax.ShapeDtypeStruct((bsz, t, d), F32), jax.ShapeDtypeStruct((bsz, CONV_F - 1, f), F32)],
        grid=(bsz // bb, t // tt),
        in_specs=[tok(d), tok(gw), tok(gw), tok(gw), tok(gw),
                  pl.BlockSpec((bb, 1, mod.shape[2]), lambda i, j: (i, 0, 0)),
                  _const_spec((1, 1, d)), _const_spec((d, d)), _const_spec((d, f)), _const_spec((d, f)),
                  _const_spec((f, d)), _const_spec((CONV_F, f)), _const_spec((1, f)),
                  pl.BlockSpec((bb, CONV_F - 1, f), lambda i, j: (i, 0, 0)),
                  _const_spec((1, 1, d))],
        out_specs=[tok(d), pl.BlockSpec((bb, CONV_F - 1, f), lambda i, j: (i, 0, 0))],
        scratch_shapes=[pltpu.VMEM((bb, tt + SUBLANE, f), F32)],
        name="post_ffn",
        compiler_params=_params(("parallel", "arbitrary"), est),
    )(x, *ys, mod, g_ffn.reshape(1, 1, d), w_out, w_gate, w_up, w_down, conv_w, conv_b, f0, g_final.reshape(1, 1, d))


def _mixpost_kernel(qkv_ref, qkvo_ref, cin_ref, din_ref, gate_ref, x_ref, mod_ref, *refs, nt, final, fchunk, n_hist):
    hist_refs = refs[:2 * n_hist]
    (bias_ref, bi_ref, bf_ref, bng_ref, ccw_ref, ccb_ref, dtb_ref, alog_ref, dskip_ref, cng_ref,
     dng_ref, dws_ref, dbx_ref, gffn_ref, wo_ref, wg_ref, wu_ref, wd_ref, fcw_ref, fcb_ref, gfin_ref,
     o_ref, c_ref, n_ref, m_ref, conv_ref, s_ref, fc_ref,
     kk_ref, vv_ref, ybuf_ref, cext_ref, fext_ref) = refs[2 * n_hist:]
    tt, d = x_ref.shape[1], x_ref.shape[2]
    w = d // 4
    f = wg_ref.shape[1]
    band = (N_BAND + 1) * CHUNK
    off = SUBLANE
    s = pl.program_id(0)
    n_tiles = pl.num_programs(0) - 1
    live = s < n_tiles
    mj = jnp.minimum(s, n_tiles - 1) % nt
    pj = jnp.maximum(s - 1, 0) % nt
    wslot = s % 2
    rslot = 1 - wslot

    @pl.when(s == 0)
    def _():
        ybuf_ref[...] = jnp.zeros(ybuf_ref.shape, ybuf_ref.dtype)

    @pl.when(mj == 0)
    def _():
        c_ref[...] = jnp.zeros(c_ref.shape, F32)
        n_ref[...] = jnp.zeros(n_ref.shape, F32)
        m_ref[...] = jnp.zeros(m_ref.shape, F32)
        s_ref[...] = jnp.zeros(s_ref.shape, F32)
        cext_ref[0:off, :] = jnp.zeros((off, cext_ref.shape[1]), F32)

    @pl.when(pj == 0)
    def _():
        fext_ref[:, 0:off, :] = jnp.zeros((1, off, f), F32)

    ex = _head_expander(w, 0)

    def attention():
        scale = (w // N_HEADS) ** -0.5
        for i, (kh_ref, vh_ref) in enumerate(zip(hist_refs[0::2], hist_refs[1::2])):
            kk_ref[i * tt:(i + 1) * tt, :] = kh_ref[0].astype(BF16)
            vv_ref[i * tt:(i + 1) * tt, :] = vh_ref[0].astype(BF16)
        kk_ref[A_WIN:A_WIN + tt, :] = qkv_ref[0, :, w:2 * w].astype(BF16)
        vv_ref[A_WIN:A_WIN + tt, :] = qkv_ref[0, :, 2 * w:3 * w].astype(BF16)
        masks = _head_lane_masks(CHUNK, w)
        bias = bias_ref[...]
        slot = _iota((1, band), 1)
        q_all = qkv_ref[0, :, 0:w] * scale
        yield
        for i in range(tt // CHUNK):
            base = i * CHUNK
            pos = slot + base
            kvalid = pos >= A_WIN
            for back in range(1, A_WIN // tt + 1):
                kvalid = kvalid | ((pos >= A_WIN - back * tt) & (mj >= back))
            out = yield from _attend_chunk(q_all[base:base + CHUNK], kk_ref[base:base + band, :],
                                           vv_ref[base:base + band, :], bias, kvalid, masks, masks)
            ybuf_ref[wslot, 0, base:base + CHUNK, :] = out.astype(BF16)
            yield

    def mlstm():
        kscale = (w // N_HEADS) ** -0.5
        bias_i, bias_f, ng = bi_ref[...], bf_ref[...], bng_ref[...]
        for c in range(tt // SPAT):
            rows = slice(c * SPAT, (c + 1) * SPAT)
            y = yield from _mlstm_chunk(
                qkvo_ref[0, rows, 0:w], qkvo_ref[0, rows, w:2 * w] * kscale, qkvo_ref[0, rows, 2 * w:3 * w],
                qkvo_ref[0, rows, 3 * w:4 * w],
                _split_dot(gate_ref[0, rows, GATE_I:GATE_I + LANE], ex, 3) + bias_i,
                _split_dot(gate_ref[0, rows, GATE_F:GATE_F + LANE], ex, 3) + bias_f,
                ng, c_ref, n_ref, m_ref, 0, SPAT, live)
            ybuf_ref[wslot, 1, rows, :] = y.astype(BF16)
            yield

    def ssd():
        xw = cext_ref.shape[1]
        hist = CONV_C - 1
        cext_ref[off:off + tt, :] = cin_ref[0, :, 0:xw]
        acc = ccb_ref[...] + ccw_ref[CONV_C - 1:CONV_C, :] * cext_ref[off:off + tt, :]
        for j in range(CONV_C - 1):
            sh = CONV_C - 1 - j
            acc = acc + ccw_ref[j:j + 1, :] * cext_ref[off - sh:off - sh + tt, :]
            yield
        tail = cext_ref[off + tt - hist:off + tt, :]
        cext_ref[off - hist:off, :] = tail
        conv_ref[0] = tail
        cext_ref[off:off + tt, :] = acc * jax.nn.sigmoid(acc)
        yield
        a_x = -jnp.exp(alog_ref[...])
        dtb, dskip, ng = dtb_ref[...], dskip_ref[...], cng_ref[...]
        for c in range(tt // SPAT):
            rows = slice(c * SPAT, (c + 1) * SPAT)
            erows = slice(off + c * SPAT, off + (c + 1) * SPAT)
            dt = jax.nn.softplus(_split_dot(gate_ref[0, rows, GATE_DT:GATE_DT + LANE], ex, 3) + dtb)
            y = yield from _ssd_chunk(cext_ref[erows, 0:w], cext_ref[erows, w:2 * w], cext_ref[erows, 2 * w:3 * w],
                                      cin_ref[0, rows, xw:xw + w], dt, a_x, dskip, ng, s_ref, 0, live)
            ybuf_ref[wslot, 2, rows, :] = y.astype(BF16)
            yield

    def gmlp():
        u = jax.nn.gelu(din_ref[0, :, 0:w])
        vr = jax.nn.gelu(din_ref[0, :, w:2 * w])
        yield
        xc = vr - jnp.mean(vr, axis=-1, keepdims=True)
        vn = (xc * lax.rsqrt(jnp.mean(xc * xc, axis=-1, keepdims=True) + EPS) * dng_ref[...]).astype(BF16)
        rows = _iota((N_HEADS * SPAT, SPAT), 0) % SPAT
        wst = jnp.where(rows >= _iota((N_HEADS * SPAT, SPAT), 1), dws_ref[...], 0.0).astype(BF16)
        lmask = _head_lane_masks(SPAT, w)
        yield
        for i in range(tt // SPAT):
            fs = jnp.dot(wst, vn[i * SPAT:(i + 1) * SPAT], preferred_element_type=F32)
            yield
            fgate = dbx_ref[...]
            for g in range(N_HEADS):
                fgate = fgate + jnp.where(lmask[g], fs[g * SPAT:(g + 1) * SPAT], 0.0)
            ybuf_ref[wslot, 3, i * SPAT:(i + 1) * SPAT, :] = (u[i * SPAT:(i + 1) * SPAT] * fgate).astype(BF16)
            yield

    def post():
        hist = CONV_F - 1
        mod = mod_ref[0]
        mix = None
        for j in range(4):
            p = jnp.dot(ybuf_ref[rslot, j], wo_ref[j * w:(j + 1) * w, :], preferred_element_type=F32)
            mix = p if mix is None else mix + p
            yield
        x1 = x_ref[0] + mod[:, 2 * d:3 * d] * mix
        h2 = _rmsnorm_rows(x1, gffn_ref[...]) * (1.0 + mod[:, 4 * d:5 * d]) + mod[:, 3 * d:4 * d]
        hb = h2.astype(BF16)
        yield
        down = None
        for c0 in range(0, f, fchunk):
            cols = slice(c0, c0 + fchunk)
            g = jnp.dot(hb, wg_ref[:, cols], preferred_element_type=F32)
            u = jnp.dot(hb, wu_ref[:, cols], preferred_element_type=F32)
            yield
            fext_ref[0, off:off + tt, cols] = g
            acc = fcb_ref[:, cols] + fcw_ref[CONV_F - 1:CONV_F, cols] * g
            for j in range(CONV_F - 1):
                sh = CONV_F - 1 - j
                acc = acc + fcw_ref[j:j + 1, cols] * fext_ref[0, off - sh:off - sh + tt, cols]
            act = (acc * jax.nn.sigmoid(acc)) * u
            p = jnp.dot(act.astype(BF16), wd_ref[cols, :], preferred_element_type=F32)
            down = p if down is None else down + p
            yield
        tail = fext_ref[:, off + tt - hist:off + tt, :]
        fext_ref[:, off - hist:off, :] = tail
        fc_ref[...] = tail
        x2 = x1 + mod[:, 5 * d:6 * d] * down
        if final:
            x2 = _rmsnorm_rows(x2, gfin_ref[...])
        o_ref[0] = x2

    _interleave([post(), mlstm(), ssd(), attention(), gmlp()])


def _mix_post(qkv_a, qkvo_b, c_in, d_in, gates, x, mod, p, g_final, tt, final):
    bsz, t, d = x.shape
    w = d // 4
    nt = t // tt
    n_tiles = bsz * nt
    f = p["f_w_gate"].shape[1]
    xw = c_in.shape[2] - w
    ng, n_c, gp = 2, LANE, w // 2
    band = (N_BAND + 1) * CHUNK
    fchunk = 2 * LANE
    assert f % fchunk == 0 and tt % SPAT == 0 and t % tt == 0
    mix_tile = lambda width: pl.BlockSpec(
        (1, tt, width), lambda s: (jnp.minimum(s, n_tiles - 1) // nt, jnp.minimum(s, n_tiles - 1) % nt, 0))
    post_tile = lambda width: pl.BlockSpec(
        (1, tt, width), lambda s: (jnp.maximum(s - 1, 0) // nt, jnp.maximum(s - 1, 0) % nt, 0))
    mix_state = lambda shape: pl.BlockSpec(
        (1,) + shape, lambda s: (jnp.minimum(s, n_tiles - 1) // nt,) + (0,) * len(shape))
    post_state = lambda shape: pl.BlockSpec(
        (1,) + shape, lambda s: (jnp.maximum(s - 1, 0) // nt,) + (0,) * len(shape))
    consts = [p["bias_tab"].reshape(N_HEADS * CHUNK, band), p["bi"], p["bf"], p["b_norm_g"], p["c_conv_w"],
              p["c_conv_b"], p["dtb"], p["a_log_x"], p["dskip"], p["c_norm_g"], p["d_norm_g"], p["d_w_s"], p["d_b_x"],
              p["g_ffn"].reshape(1, d), p["w_out"], p["f_w_gate"], p["f_w_up"], p["f_w_down"], p["f_conv_w"],
              p["f_conv_b"], g_final.reshape(1, d)]
    in_widths = (qkv_a.shape[2], qkvo_b.shape[2], c_in.shape[2], d_in.shape[2], gates.shape[2])
    n_hist = A_WIN // tt
    assert n_hist * tt == A_WIN

    def hist_tile(back, col):
        def index(s):
            m = jnp.minimum(s, n_tiles - 1)
            return (m // nt, jnp.maximum(m % nt - back, 0), col)
        return pl.BlockSpec((1, tt, w), index)

    hist_specs = [hist_tile(back, col) for back in range(n_hist, 0, -1) for col in (1, 2)]
    est = ((d * d + 3 * d * f) * 2 + 2 * tt * sum(in_widths) * 4 + 4 * tt * d * 4 + (tt + SUBLANE) * (f + xw) * 4
           + 2 * (A_WIN + tt) * w * 2 + 8 * tt * w * 2 + 16 * tt * d * 4 + 6 * w * w * 4 + 4 * n_hist * tt * w * 4)
    return pl.pallas_call(
        functools.partial(_mixpost_kernel, nt=nt, final=final, fchunk=fchunk, n_hist=n_hist),
        out_shape=[jax.ShapeDtypeStruct((bsz, t, d), F32),
                   jax.ShapeDtypeStruct((bsz, w, w), F32),
                   jax.ShapeDtypeStruct((bsz, 1, w), F32),
                   jax.ShapeDtypeStruct((bsz, 1, w), F32),
                   jax.ShapeDtypeStruct((bsz, CONV_C - 1, xw), F32),
                   jax.ShapeDtypeStruct((bsz, ng, n_c, gp), F32),
                   jax.ShapeDtypeStruct((bsz, CONV_F - 1, f), F32)],
        grid=(n_tiles + 1,),
        in_specs=[mix_tile(wd) for wd in in_widths] + [post_tile(d), post_state((1, mod.shape[2]))] + hist_specs
                 + [_const_spec(c.shape) for c in consts],
        out_specs=[post_tile(d), mix_state((w, w)), mix_state((1, w)), mix_state((1, w)),
                   mix_state((CONV_C - 1, xw)), mix_state((ng, n_c, gp)), post_state((CONV_F - 1, f))],
        scratch_shapes=[pltpu.VMEM((A_WIN + tt, w), BF16), pltpu.VMEM((A_WIN + tt, w), BF16),
                        pltpu.VMEM((2, 4, tt, w), BF16), pltpu.VMEM((tt + SUBLANE, xw), F32),
                        pltpu.VMEM((1, tt + SUBLANE, f), F32)],
        name="mix_post",
        compiler_params=_params(("arbitrary",), est),
    )(qkv_a, qkvo_b, c_in, d_in, gates, x, mod, *([qkv_a] * len(hist_specs)), *consts)


def _layer(x, mod, st, p, tiles, final, g_final):
    bsz, t, d = x.shape
    gw = d // 4
    dh = gw // N_HEADS
    bb, tt, tmix, nb = tiles
    mod3 = mod.reshape(bsz, 1, mod.shape[1])
    qkv_a, qkvo_b, c_in, d_in, gates = _in_proj(x, mod3, p["g_mix"], p["w_in"], p["widths"], bb, tt)

    if st is None:
        x_new, c_new, n_new, m_new, conv_new, ssm_new, fconv_new = _mix_post(
            qkv_a, qkvo_b, c_in, d_in, gates, x, mod3, p, g_final, tmix, final)
        vn = []
    else:
        y_a = _attention(qkv_a, st["a_k"], st["a_v"], p["bias_tab"], tmix)
        y_b, c_new, n_new, m_new = _mlstm(qkvo_b, gates, p["bi"], p["bf"], p["b_norm_g"], st["b_c"], st["b_n"],
                                          st["b_m"], nb, tmix)
        y_c, conv_new, ssm_new = _ssd(c_in, gates, p["c_conv_w"], p["c_conv_b"], p["dtb"], p["a_log_x"], p["dskip"],
                                      p["c_norm_g"], st["c_conv"], st["c_ssm"], nb, tmix)
        y_d, *vn = _gmlp(d_in, p["d_norm_g"], p["d_w_s"], p["d_b_x"], tmix, True)
        x_new, fconv_new = _post(x, (y_a, y_b, y_c, y_d), mod3, p["g_ffn"], p["w_out"], p["f_w_gate"], p["f_w_up"],
                                 p["f_w_down"], p["f_conv_w"], p["f_conv_b"], st["f_conv"], g_final, bb, tt, final)

    keep = min(A_WIN, t)
    new_k = qkv_a[:, t - keep:, gw:2 * gw].reshape(bsz, keep, N_HEADS, dh)
    new_v = qkv_a[:, t - keep:, 2 * gw:3 * gw].reshape(bsz, keep, N_HEADS, dh)
    idx = jnp.arange(N_HEADS)
    c_heads = c_new.reshape(bsz, N_HEADS, dh, N_HEADS, dh)[:, idx, :, idx, :]
    c_heads = jnp.moveaxis(c_heads, 0, 1)
    n_heads = n_new.reshape(bsz, N_HEADS, dh)
    m_heads = m_new[:, 0, ::dh]
    ssm = jnp.swapaxes(ssm_new, 2, 3).reshape(bsz, N_HEADS, gw // N_HEADS, ssm_new.shape[2])
    outs = (new_k, new_v, c_heads, n_heads, m_heads, ssm, conv_new, fconv_new, *vn)
    return x_new, outs


def _prep_layer(l, w_in, w_out, a_rel_bias, b_i_bias, b_f_bias, b_norm_g, c_conv_w, c_conv_b, c_dt_bias, c_a_log,
                c_d_skip, c_norm_g, d_norm_g, d_w_s, d_b_s, f_w_gate, f_w_up, f_conv_w, f_conv_b, f_w_down,
                g_norm_mix, g_norm_ffn):
    d = w_in.shape[1]
    gw = d // 4
    nh = b_i_bias.shape[1]
    xbc_w = c_conv_w.shape[2]
    sizes = (gw,) * 7 + (nh, nh, gw, xbc_w, c_dt_bias.shape[1], gw, gw)
    offs = [0]
    for s in sizes:
        offs.append(offs[-1] + s)
    moves, dst = [], 0
    for i in (0, 1, 2, 3, 4, 5, 6, 10, 9, 12, 13):
        moves.append((offs[i], dst, sizes[i]))
        dst += sizes[i]
    assert sizes[7] == sizes[8] == sizes[11] == N_HEADS
    packed = (dst, ((offs[7], GATE_I, N_HEADS), (offs[8], GATE_F, N_HEADS), (offs[11], GATE_DT, N_HEADS)))
    widths = (3 * gw, 4 * gw, xbc_w + gw, 2 * gw, GATE_W)
    dst += GATE_W
    assert dst == sum(widths)
    heads_x = lambda v: jnp.repeat(v.astype(F32), gw // nh)[None, :]
    return dict(
        w_in=_regroup_cast(w_in, l, moves, packed, dst), widths=widths, g_mix=g_norm_mix[l], g_ffn=g_norm_ffn[l],
        w_out=w_out[l].astype(BF16), bias_tab=_bias_table(a_rel_bias[l]),
        bi=heads_x(b_i_bias[l]), bf=heads_x(b_f_bias[l]), b_norm_g=b_norm_g[l][None, :],
        c_conv_w=c_conv_w[l], c_conv_b=c_conv_b[l][None, :], dtb=heads_x(c_dt_bias[l]),
        a_log_x=heads_x(c_a_log[l]), dskip=heads_x(c_d_skip[l]), c_norm_g=c_norm_g[l][None, :],
        d_norm_g=d_norm_g[l][None, :], d_w_s=d_w_s[l].reshape(-1, d_w_s.shape[-1]),
        d_b_x=jnp.repeat(d_b_s[l].T, gw // d_b_s.shape[1], axis=1),
        f_w_gate=f_w_gate[l].astype(BF16), f_w_up=f_w_up[l].astype(BF16), f_w_down=f_w_down[l].astype(BF16),
        f_conv_w=f_conv_w[l], f_conv_b=f_conv_b[l][None, :],
    )


def _blockdiag_state(c):
    bsz, nh, dh, _ = c.shape
    eye = jnp.eye(nh, dtype=c.dtype)
    return jnp.einsum("bhde,hg->bhdge", c, eye).reshape(bsz, nh * dh, nh * dh)


def kernel(x_prompt, x_sample, c_prompt, c_sample, cache_a_k, cache_a_v, state_b_c, state_b_n, state_b_m, state_c_ssm, state_c_conv, state_ffn_conv, w_ada, b_ada, g_norm_mix, g_norm_ffn, w_in, w_out, a_rel_bias, b_i_bias, b_f_bias, b_norm_g, c_conv_w, c_conv_b, c_dt_bias, c_a_log, c_d_skip, c_norm_g, d_norm_g, d_w_s, d_b_s, f_w_gate, f_w_up, f_conv_w, f_conv_b, f_w_down, g_final):
    depth = w_in.shape[0]
    bp, tp, d = x_prompt.shape
    bs, ts, _ = x_sample.shape
    gw = d // 4
    dh = gw // N_HEADS
    xbc_w = c_conv_w.shape[2]
    n_c = state_c_ssm.shape[-1]
    g_c = (xbc_w - gw) // 2 // n_c

    mod_all = _ada(jnp.concatenate([c_prompt, c_sample], axis=0), w_ada, b_ada)

    nb = max(n for n in (4, 2, 1) if bs % n == 0)
    tiles_p = (1, min(A_WIN, tp), min(2 * SPAT, tp), 1)
    tiles_s = (bs, ts, ts, nb)

    xp, xs = x_prompt, x_sample
    p_states, s_states = [], []
    for l in range(depth):
        p = _prep_layer(l, w_in, w_out, a_rel_bias, b_i_bias, b_f_bias, b_norm_g, c_conv_w, c_conv_b, c_dt_bias,
                        c_a_log, c_d_skip, c_norm_g, d_norm_g, d_w_s, d_b_s, f_w_gate, f_w_up, f_conv_w, f_conv_b,
                        f_w_down, g_norm_mix, g_norm_ffn)
        final = l == depth - 1
        xp, sp = _layer(xp, mod_all[l, :bp], None, p, tiles_p, final, g_final)
        st_s = dict(a_k=cache_a_k[l].reshape(bs, -1, gw), a_v=cache_a_v[l].reshape(bs, -1, gw),
                    b_c=_blockdiag_state(state_b_c[l]), b_n=state_b_n[l].reshape(bs, 1, gw),
                    b_m=jnp.repeat(state_b_m[l], dh, axis=1)[:, None, :],
                    c_conv=state_c_conv[l],
                    c_ssm=jnp.swapaxes(state_c_ssm[l].reshape(bs, g_c, gw // g_c, n_c), 2, 3),
                    f_conv=state_ffn_conv[l])
        xs, ss = _layer(xs, mod_all[l, bp:], st_s, p, tiles_s, final, g_final)
        p_states.append(sp)
        s_states.append(ss)

    stack = lambda states, i: jnp.stack([s[i] for s in states])
    return (xp, xs,
            *(stack(p_states, i) for i in range(8)),
            *(stack(s_states, i) for i in range(9)))
```

```python
import functools

import jax
import jax.numpy as jnp
from jax import lax
from jax.experimental import pallas as pl
from jax.experimental.pallas import tpu as pltpu

F32 = jnp.float32
BF16 = jnp.bfloat16

EPS = 1e-6
NEG = -1e30

CHUNK = 64
N_BAND = 8
A_WIN = N_BAND * CHUNK
REL_CLIP = 128
N_HEADS = 4
SPAT = 128
CONV_C = 4
CONV_F = 3
LANE = 128
GATE_I, GATE_F, GATE_DT = 0, LANE, 2 * LANE
GATE_W = 3 * LANE
SUBLANE = 8
VMEM_CAP = 64 * 1024 * 1024


def _vmem_limit(nbytes):
    return int(min(max(nbytes, 16 * 1024 * 1024), VMEM_CAP - 8 * 1024 * 1024))


def _params(sem, nbytes):
    return pltpu.CompilerParams(dimension_semantics=sem, vmem_limit_bytes=_vmem_limit(nbytes))


def _const_spec(shape):
    nd = len(shape)
    return pl.BlockSpec(shape, lambda *_: (0,) * nd, pipeline_mode=pl.Buffered(1))


def _iota(shape, dim):
    return lax.broadcasted_iota(jnp.int32, shape, dim)


def _bf16_pieces(x, parts):
    out = []
    r = x
    for i in range(parts):
        hi = r.astype(BF16)
        out.append(hi)
        if i + 1 < parts:
            r = r - hi.astype(F32)
    return out


def _dot_pieces(pieces, e):
    acc = None
    for piece in pieces:
        d = jnp.dot(piece, e, preferred_element_type=F32)
        acc = d if acc is None else acc + d
    return acc


def _dot_pieces_left(e, pieces):
    acc = None
    for piece in pieces:
        d = jnp.dot(e, piece, preferred_element_type=F32)
        acc = d if acc is None else acc + d
    return acc


def _split_dot(x, e, parts):
    return _dot_pieces(_bf16_pieces(x, parts), e)


def _dot_nt(a, b):
    return lax.dot_general(a, b, (((1,), (1,)), ((), ())), preferred_element_type=F32)


def _dot_tn(a, b):
    return lax.dot_general(a, b, (((0,), (0,)), ((), ())), preferred_element_type=F32)


def _head_expander(width, first):
    dh = width // N_HEADS
    return (_iota((LANE, width), 1) // dh == _iota((LANE, width), 0) - first).astype(BF16)


def _head_lane_masks(rows, width):
    dh = width // N_HEADS
    lane = _iota((rows, width), 1)
    return [(lane >= h * dh) & (lane < (h + 1) * dh) for h in range(N_HEADS)]


def _tril(n):
    return _iota((n, n), 0) >= _iota((n, n), 1)


def _pad_rows(x, rows):
    if x.shape[0] == rows:
        return x
    return jnp.concatenate([x, jnp.zeros((rows - x.shape[0], x.shape[1]), x.dtype)], axis=0)


def _interleave(gens):
    results = [None] * len(gens)
    live = list(range(len(gens)))
    while live:
        for i in list(live):
            try:
                next(gens[i])
            except StopIteration as stop:
                results[i] = stop.value
                live.remove(i)
    return results


def _rmsnorm_rows(x, g):
    return x * lax.rsqrt(jnp.mean(x * x, axis=-1, keepdims=True) + EPS) * g


def _regroup_kernel(w_ref, o_ref, *, moves, packed):
    w = w_ref[0]
    rows = w.shape[0]
    for src, dst, n in moves:
        o_ref[:, dst:dst + n] = w[:, src:src + n].astype(o_ref.dtype)
    dst, pieces = packed
    parts, lane = [], 0
    for src, first, n in pieces:
        if first > lane:
            parts.append(jnp.zeros((rows, first - lane), F32))
        parts.append(w[:, src:src + n])
        lane = first + n
    parts.append(jnp.zeros((rows, GATE_W - lane), F32))
    o_ref[:, dst:dst + GATE_W] = jnp.concatenate(parts, axis=1).astype(o_ref.dtype)


def _regroup_cast(w_all, layer, moves, packed, ncols, row_block=256):
    _, r, c = w_all.shape
    return pl.pallas_call(
        functools.partial(_regroup_kernel, moves=tuple(moves), packed=packed),
        out_shape=jax.ShapeDtypeStruct((r, ncols), BF16),
        grid=(r // row_block,),
        in_specs=[pl.BlockSpec((1, row_block, c), lambda i: (layer, i, 0))],
        out_specs=pl.BlockSpec((row_block, ncols), lambda i: (i, 0)),
        name="regroup_cast",
        compiler_params=_params(("parallel",), 2 * row_block * (c * 4 + ncols * 2) + row_block * c * 4),
    )(w_all)


def _ada_kernel(c_ref, w_ref, b_ref, o_ref):
    c = c_ref[...]
    h = (c * jax.nn.sigmoid(c)).astype(BF16)
    o_ref[0] = jnp.dot(h, w_ref[0].astype(BF16), preferred_element_type=F32) + b_ref[0]


def _ada(c_all, w_ada, b_ada):
    depth, d, n6 = w_ada.shape
    r = c_all.shape[0]
    tn = d
    return pl.pallas_call(
        _ada_kernel,
        out_shape=jax.ShapeDtypeStruct((depth, r, n6), F32),
        grid=(depth, n6 // tn),
        in_specs=[pl.BlockSpec((r, d), lambda l, j: (0, 0)),
                  pl.BlockSpec((1, d, tn), lambda l, j: (l, 0, j)),
                  pl.BlockSpec((1, 1, tn), lambda l, j: (l, 0, j))],
        out_specs=pl.BlockSpec((1, r, tn), lambda l, j: (l, 0, j)),
        name="ada_mod",
        compiler_params=_params(("parallel", "parallel"), 4 * (2 * d * tn * 4 + 2 * r * tn * 4 + r * d * 4)),
    )(c_all, w_ada, b_ada.reshape(depth, 1, n6))


def _bias_kernel(rb_ref, o_ref, *, lo, hi):
    nh, lq, lk = o_ref.shape
    idx = jnp.clip(A_WIN + _iota((lq, lk), 0) - _iota((lq, lk), 1), -REL_CLIP, REL_CLIP) + REL_CLIP
    for h in range(nh):
        def body(r, acc, h=h):
            return jnp.where(idx == r, rb_ref[h, r], acc)
        o_ref[h] = lax.fori_loop(lo, hi + 1, body, jnp.zeros((lq, lk), F32))


def _bias_table(rel_bias):
    nh = rel_bias.shape[0]
    band = (N_BAND + 1) * CHUNK
    lo = max(A_WIN - (band - 1), -REL_CLIP) + REL_CLIP
    hi = min(A_WIN + CHUNK - 1, REL_CLIP) + REL_CLIP
    return pl.pallas_call(
        functools.partial(_bias_kernel, lo=lo, hi=hi),
        out_shape=jax.ShapeDtypeStruct((nh, CHUNK, band), F32),
        in_specs=[pl.BlockSpec(memory_space=pltpu.SMEM)],
        out_specs=pl.BlockSpec(memory_space=pltpu.VMEM),
        name="rel_bias_table",
    )(rel_bias)


def _in_kernel(x_ref, mod_ref, g_ref, w_ref, *o_refs, col_starts):
    bb, tt, d = x_ref.shape
    x = x_ref[...]
    mod = mod_ref[...]
    h = _rmsnorm_rows(x, g_ref[...]) * (1.0 + mod[:, :, d:2 * d]) + mod[:, :, 0:d]
    hb = h.reshape(bb * tt, d).astype(BF16)
    for o_ref, (a, b) in zip(o_refs, col_starts):
        o_ref[...] = jnp.dot(hb, w_ref[:, a:b], preferred_element_type=F32).reshape(bb, tt, b - a).astype(o_ref.dtype)


def _in_proj(x, mod, g, w, widths, dtypes, bb, tt):
    bsz, t, d = x.shape
    ncols = w.shape[1]
    starts, a = [], 0
    for wd in widths:
        starts.append((a, a + wd))
        a += wd
    m = bb * tt
    est = 2 * m * d * 4 + 2 * d * ncols * 2 + 2 * m * ncols * 4 + 3 * m * d * 4 + m * max(widths) * 4
    return pl.pallas_call(
        functools.partial(_in_kernel, col_starts=tuple(starts)),
        out_shape=[jax.ShapeDtypeStruct((bsz, t, wd), dt) for wd, dt in zip(widths, dtypes)],
        grid=(bsz // bb, t // tt),
        in_specs=[pl.BlockSpec((bb, tt, d), lambda i, j: (i, j, 0)),
                  pl.BlockSpec((bb, 1, mod.shape[2]), lambda i, j: (i, 0, 0)),
                  _const_spec((1, 1, d)),
                  _const_spec((d, ncols))],
        out_specs=[pl.BlockSpec((bb, tt, wd), lambda i, j: (i, j, 0)) for wd in widths],
        name="in_proj",
        compiler_params=_params(("parallel", "parallel"), est),
    )(x, mod, g.reshape(1, 1, d), w)


def _attend_chunk(qc, kb, vb, bias, kvalid, masks_q, masks_o):
    lq = qc.shape[0]
    qs = jnp.concatenate([jnp.where(mk, qc, 0.0) for mk in masks_q], axis=0).astype(BF16)
    yield
    s = _dot_nt(qs, kb) + bias
    yield
    if kvalid is not None:
        s = jnp.where(kvalid, s, NEG)
    e = jnp.exp(s - jnp.max(s, axis=-1, keepdims=True))
    eb = e.astype(BF16)
    rinv = 1.0 / jnp.sum(e, axis=-1, keepdims=True)
    yield
    o = jnp.dot(eb, vb, preferred_element_type=F32)
    yield
    o = o * rinv
    out = jnp.where(masks_o[0], o[0:lq], 0.0)
    for h in range(1, N_HEADS):
        out = out + jnp.where(masks_o[h], o[h * lq:(h + 1) * lq], 0.0)
    return out


def _attn_kernel(q_ref, k_ref, v_ref, kh_ref, vh_ref, bias_ref, y_ref, kk_ref, vv_ref, *, hist_is_cache, t_valid):
    tq, w = q_ref.shape[1], q_ref.shape[2]
    tpad = kk_ref.shape[0] - A_WIN
    scale = (w // N_HEADS) ** -0.5
    kk_ref[0:A_WIN, :] = kh_ref[0].astype(BF16)
    vv_ref[0:A_WIN, :] = vh_ref[0].astype(BF16)
    kk_ref[A_WIN:A_WIN + tpad, :] = _pad_rows(k_ref[0], tpad).astype(BF16)
    vv_ref[A_WIN:A_WIN + tpad, :] = _pad_rows(v_ref[0], tpad).astype(BF16)
    masks = _head_lane_masks(CHUNK, w)
    bias = bias_ref[...]
    band = (N_BAND + 1) * CHUNK
    slot = _iota((1, band), 1)
    hist_ok = jnp.logical_or(pl.program_id(1) > 0, hist_is_cache)
    q_all = _pad_rows(q_ref[0], tpad) * scale
    gens = []
    for i in range(tpad // CHUNK):
        base = i * CHUNK
        pos = slot + base
        kvalid = (pos < A_WIN + t_valid) & ((pos >= A_WIN) | hist_ok)
        gens.append(_attend_chunk(q_all[base:base + CHUNK], kk_ref[base:base + band, :], vv_ref[base:base + band, :],
                                  bias, kvalid, masks, masks))
    for i, out in enumerate(_interleave(gens)):
        base = i * CHUNK
        rows = min(CHUNK, tq - base)
        y_ref[0, base:base + rows, :] = out[0:rows].astype(y_ref.dtype)


def _attention(qkv, k_hist, v_hist, bias_tab, tq):
    bsz, t, w3 = qkv.shape
    w = w3 // 3
    hist_is_cache = k_hist is not None
    nt = t // tq
    tpad = -(-tq // CHUNK) * CHUNK
    band = (N_BAND + 1) * CHUNK
    if hist_is_cache:
        assert nt == 1
        hist_specs = [pl.BlockSpec((1, A_WIN, w), lambda b, j: (b, 0, 0))] * 2
        hist_args = (k_hist, v_hist)
    else:
        assert tq == A_WIN
        hist_specs = [pl.BlockSpec((1, tq, w), lambda b, j: (b, jnp.maximum(j - 1, 0), 1)),
                      pl.BlockSpec((1, tq, w), lambda b, j: (b, jnp.maximum(j - 1, 0), 2))]
        hist_args = (qkv, qkv)
    est = 2 * 5 * tpad * w * 4 + 2 * tq * w * 4 + 2 * (A_WIN + tpad) * w * 2 + 12 * N_HEADS * CHUNK * band * 4
    return pl.pallas_call(
        functools.partial(_attn_kernel, hist_is_cache=hist_is_cache, t_valid=tq if nt == 1 else tpad),
        out_shape=jax.ShapeDtypeStruct((bsz, t, w), BF16),
        grid=(bsz, nt),
        in_specs=[pl.BlockSpec((1, tq, w), lambda b, j: (b, j, 0)),
                  pl.BlockSpec((1, tq, w), lambda b, j: (b, j, 1)),
                  pl.BlockSpec((1, tq, w), lambda b, j: (b, j, 2)),
                  *hist_specs,
                  _const_spec((N_HEADS * CHUNK, band))],
        out_specs=pl.BlockSpec((1, tq, w), lambda b, j: (b, j, 0)),
        scratch_shapes=[pltpu.VMEM((A_WIN + tpad, w), BF16), pltpu.VMEM((A_WIN + tpad, w), BF16)],
        name="band_attention",
        compiler_params=_params(("parallel", "parallel"), est),
    )(qkv, qkv, qkv, *hist_args, bias_tab.reshape(N_HEADS * CHUNK, band))


def _mlstm_chunk(q, k, v, og, gi, gf, normg, c_ref, n_ref, m_ref, bi, n_valid, live=None):
    L, w = q.shape
    dh = w // N_HEADS
    lmask = _head_lane_masks(L, w)
    tril = _tril(L)
    blockdiag = (_iota((w, w), 0) // dh) == (_iota((w, w), 1) // dh)
    bd = blockdiag.astype(BF16)

    m_prev = m_ref[bi]
    c_old = c_ref[bi]
    n_old = n_ref[bi]
    qb16 = q.astype(BF16)
    kb16 = k.astype(BF16)
    vb16 = v.astype(BF16)
    cb16 = c_old.astype(BF16)
    qmask = [jnp.where(lmask[h], q, 0.0).astype(BF16) for h in range(N_HEADS)]
    lf_p = _bf16_pieces(jax.nn.log_sigmoid(gf), 3)
    qn_p = _bf16_pieces(q * n_old, 2)
    yield
    b = _dot_pieces_left(tril.astype(BF16), lf_p)
    q_c = jnp.dot(qb16, cb16, preferred_element_type=F32)
    q_n = _dot_pieces(qn_p, bd)
    yield
    qk = [_dot_nt(qmask[h], kb16) for h in range(N_HEADS)]
    u = gi - b
    u_t = u.T
    yield
    cm = jnp.zeros((L, w), F32)
    for h in range(N_HEADS):
        cmh = jnp.max(jnp.where(tril, u_t[h * dh:h * dh + 1, :], NEG), axis=1, keepdims=True)
        cm = jnp.where(lmask[h], cmh, cm)
    mx = jnp.maximum(m_prev, cm)
    m_t = b + mx
    inter = jnp.exp(m_prev - mx)
    last = n_valid - 1
    m_last = m_t[last:last + 1, :]
    b_last = b[last:last + 1, :]
    decay = jnp.exp(b_last + m_prev - m_last)
    ws = jnp.exp(u + (b_last - m_last))
    if n_valid < L:
        ws = jnp.where(_iota((L, w), 0) < n_valid, ws, 0.0)
    kw = k * ws
    kwb16 = kw.astype(BF16)
    yield
    num = jnp.zeros((L, w), F32)
    rs = jnp.zeros((L, w), F32)
    wqb = None
    for h in range(N_HEADS + 1):
        if h > 0:
            nv = jnp.dot(wqb, vb16, preferred_element_type=F32)
            num = num + jnp.where(lmask[h - 1], nv, 0.0)
        if h < N_HEADS:
            arg = jnp.where(tril, u_t[h * dh:h * dh + 1, :] - mx[:, h * dh:h * dh + 1], NEG)
            wqk = jnp.exp(arg) * qk[h]
            rs = jnp.where(lmask[h], jnp.sum(wqk, axis=1, keepdims=True), rs)
            wqb = wqk.astype(BF16)
        yield
    upd = _dot_tn(kwb16, vb16)
    den = inter * q_n + rs
    hout = (inter * q_c + num) / jnp.maximum(jnp.abs(den), jnp.exp(-m_t))
    mu_p = _bf16_pieces(hout, 2)
    yield
    xc = hout - _dot_pieces(mu_p, bd) * (1.0 / dh)
    var_p = _bf16_pieces(xc * xc, 2)
    yield
    var = _dot_pieces(var_p, bd) * (1.0 / dh)
    yield
    y = jax.nn.sigmoid(og) * (xc * lax.rsqrt(var + EPS) * normg)
    keep = (lambda new, old: new) if live is None else (lambda new, old: jnp.where(live, new, old))
    c_ref[bi] = keep(decay * c_old + jnp.where(blockdiag, upd, 0.0), c_old)
    n_ref[bi] = keep(decay * n_old + jnp.sum(kw, axis=0, keepdims=True), n_old)
    m_ref[bi] = keep(m_last, m_prev)
    return y


def _mlstm_kernel(q_ref, k_ref, v_ref, o_ref, g_ref, bi_ref, bf_ref, ng_ref,
                  c0_ref, n0_ref, m0_ref, y_ref, c_ref, n_ref, m_ref):
    nb, tt, w = q_ref.shape
    kscale = (w // N_HEADS) ** -0.5

    @pl.when(pl.program_id(1) == 0)
    def _():
        c_ref[...] = c0_ref[...]
        n_ref[...] = n0_ref[...]
        m_ref[...] = m0_ref[...]

    bias_i = bi_ref[...]
    bias_f = bf_ref[...]
    ng = ng_ref[...]
    ex = _head_expander(w, 0)

    def chunk(bi, rows, n_valid):
        pad = lambda r: _pad_rows(r[bi, rows, :], SPAT)
        g = pad(g_ref)
        return _mlstm_chunk(pad(q_ref), pad(k_ref) * kscale, pad(v_ref), pad(o_ref),
                            _split_dot(g[:, GATE_I:GATE_I + LANE], ex, 3) + bias_i,
                            _split_dot(g[:, GATE_F:GATE_F + LANE], ex, 3) + bias_f,
                            ng, c_ref, n_ref, m_ref, bi, n_valid)

    if tt < SPAT:
        for bi, y in enumerate(_interleave([chunk(bi, slice(None), tt) for bi in range(nb)])):
            y_ref[bi] = y[0:tt].astype(y_ref.dtype)
    else:
        def body(i, carry):
            rows = pl.ds(pl.multiple_of(i * SPAT, SPAT), SPAT)
            for bi, y in enumerate(_interleave([chunk(bi, rows, SPAT) for bi in range(nb)])):
                y_ref[bi, rows, :] = y.astype(y_ref.dtype)
            return carry
        lax.fori_loop(0, tt // SPAT, body, 0)


def _mlstm(qkvo, gates, bias_i, bias_f, normg, c0, n0, m0, nb, tt):
    bsz, t, w4 = qkvo.shape
    w = w4 // 4
    blk = lambda c: pl.BlockSpec((nb, tt, w), lambda b, j: (b, j, c))
    st = lambda shape: pl.BlockSpec((nb,) + shape, lambda b, j: (b, 0, 0))
    est = nb * (2 * 5 * tt * w * 4 + 4 * tt * LANE * 4 + 6 * w * w * 4 + 40 * SPAT * w * 4 + 24 * SPAT * SPAT * 4)
    return pl.pallas_call(
        _mlstm_kernel,
        out_shape=[jax.ShapeDtypeStruct((bsz, t, w), BF16),
                   jax.ShapeDtypeStruct((bsz, w, w), F32),
                   jax.ShapeDtypeStruct((bsz, 1, w), F32),
                   jax.ShapeDtypeStruct((bsz, 1, w), F32)],
        grid=(bsz // nb, t // tt),
        in_specs=[blk(0), blk(1), blk(2), blk(3), pl.BlockSpec((nb, tt, GATE_W), lambda b, j: (b, j, 0)),
                  _const_spec((1, w)), _const_spec((1, w)), _const_spec((1, w)),
                  st((w, w)), st((1, w)), st((1, w))],
        out_specs=[pl.BlockSpec((nb, tt, w), lambda b, j: (b, j, 0)), st((w, w)), st((1, w)), st((1, w))],
        name="mlstm",
        compiler_params=_params(("parallel", "arbitrary"), est),
    )(qkvo, qkvo, qkvo, qkvo, gates, bias_i, bias_f, normg, c0, n0, m0)


def _ssd_chunk(xs, bm, cm, z, dt, a_x, dskip, normg, s_ref, bi, live=None):
    L, w = xs.shape
    dh = w // N_HEADS
    ng = s_ref.shape[1]
    gw = w // ng
    lmask = _head_lane_masks(L, w)
    tril = _tril(L)
    s_old = [s_ref[bi, g] for g in range(ng)]
    sb16 = [s.astype(BF16) for s in s_old]
    cmb = cm.astype(BF16)
    bmb = bm.astype(BF16)
    xdt = (xs * dt).astype(BF16)
    da_p = _bf16_pieces(dt * a_x, 3)
    yield
    cs = _dot_pieces_left(tril.astype(BF16), da_p)
    cb = [_dot_nt(cmb[:, g * gw:(g + 1) * gw], bmb[:, g * gw:(g + 1) * gw]) for g in range(ng)]
    y_in = jnp.concatenate([jnp.dot(cmb[:, g * gw:(g + 1) * gw], sb16[g], preferred_element_type=F32)
                            for g in range(ng)], axis=1)
    yield
    cs_t = cs.T
    cs_last = cs[L - 1:L, :]
    wl = jnp.exp(cs_last - cs) * dt
    wx = (xs * wl).astype(BF16)
    yield
    y = jnp.exp(cs) * y_in
    mh = None
    for h in range(N_HEADS + 1):
        if h > 0:
            y = y + jnp.where(lmask[h - 1], jnp.dot(mh, xdt, preferred_element_type=F32), 0.0)
        if h < N_HEADS:
            dec = jnp.exp(jnp.where(tril, cs[:, h * dh:h * dh + 1] - cs_t[h * dh:h * dh + 1, :], NEG))
            mh = (cb[h * ng // N_HEADS] * dec).astype(BF16)
        yield
    dec_x = jnp.exp(cs_last)
    s_new = [dec_x[:, g * gw:(g + 1) * gw] * s_old[g] + _dot_tn(bmb[:, g * gw:(g + 1) * gw], wx[:, g * gw:(g + 1) * gw])
             for g in range(ng)]
    yield
    yc = y + dskip * xs
    out = _rmsnorm_rows(yc * (z * jax.nn.sigmoid(z)), normg)
    for g in range(ng):
        s_ref[bi, g] = s_new[g] if live is None else jnp.where(live, s_new[g], s_old[g])
    return out


def _ssd_kernel(xbc_ref, z_ref, gd_ref, cw_ref, cb_ref, dtb_ref, a_ref, dskip_ref, ng_ref, conv0_ref, s0_ref,
                y_ref, conv_ref, s_ref, ext_ref):
    nb, tt, w = z_ref.shape
    hist = CONV_C - 1
    off = SUBLANE

    @pl.when(pl.program_id(1) == 0)
    def _():
        s_ref[...] = s0_ref[...]
        ext_ref[:, off - hist:off, :] = conv0_ref[...]

    cw = cw_ref[...]
    for bi in range(nb):
        ext_ref[bi, off:off + tt, :] = xbc_ref[bi]
        acc = cb_ref[...] + cw[CONV_C - 1:CONV_C, :] * ext_ref[bi, off:off + tt, :]
        for j in range(CONV_C - 1):
            sh = CONV_C - 1 - j
            acc = acc + cw[j:j + 1, :] * ext_ref[bi, off - sh:off - sh + tt, :]
        tail = ext_ref[bi, off + tt - hist:off + tt, :]
        ext_ref[bi, off - hist:off, :] = tail
        conv_ref[bi] = tail
        ext_ref[bi, off:off + tt, :] = acc * jax.nn.sigmoid(acc)

    dtb = dtb_ref[...]
    a_x = -jnp.exp(a_ref[...])
    dskip = dskip_ref[...]
    ng = ng_ref[...]
    ex = _head_expander(w, 0)

    def chunk(bi, ext_rows, rows, n_valid):
        act = _pad_rows(ext_ref[bi, ext_rows, :], SPAT)
        dt = jax.nn.softplus(_split_dot(_pad_rows(gd_ref[bi, rows, :], SPAT), ex, 3) + dtb)
        if n_valid < SPAT:
            dt = jnp.where(_iota((SPAT, w), 0) < n_valid, dt, 0.0)
        return _ssd_chunk(act[:, 0:w], act[:, w:2 * w], act[:, 2 * w:3 * w], _pad_rows(z_ref[bi, rows, :], SPAT),
                          dt, a_x, dskip, ng, s_ref, bi)

    if tt < SPAT:
        gens = [chunk(bi, slice(off, off + tt), slice(None), tt) for bi in range(nb)]
        for bi, y in enumerate(_interleave(gens)):
            y_ref[bi] = y[0:tt].astype(y_ref.dtype)
    else:
        def body(i, carry):
            start = pl.multiple_of(i * SPAT, SPAT)
            ext_rows = pl.ds(pl.multiple_of(start + off, SUBLANE), SPAT)
            gens = [chunk(bi, ext_rows, pl.ds(start, SPAT), SPAT) for bi in range(nb)]
            for bi, y in enumerate(_interleave(gens)):
                y_ref[bi, pl.ds(start, SPAT), :] = y.astype(y_ref.dtype)
            return carry
        lax.fori_loop(0, tt // SPAT, body, 0)


def _ssd(cin, gates, conv_w, conv_b, dtb, a_log_x, dskip, normg, conv0, s0, nb, tt):
    bsz, t, wtot = cin.shape
    w = dskip.shape[1]
    xw = wtot - w
    ng, n_c, gp = s0.shape[1:]
    st = lambda shape: pl.BlockSpec((nb,) + shape, lambda b, j: (b,) + (0,) * len(shape))
    est = nb * (2 * 2 * tt * wtot * 4 + 2 * tt * w * 4 + (tt + SUBLANE) * xw * 4 + 3 * tt * xw * 4 + 40 * SPAT * w * 4)
    return pl.pallas_call(
        _ssd_kernel,
        out_shape=[jax.ShapeDtypeStruct((bsz, t, w), BF16),
                   jax.ShapeDtypeStruct((bsz, CONV_C - 1, xw), F32),
                   jax.ShapeDtypeStruct(s0.shape, F32)],
        grid=(bsz // nb, t // tt),
        in_specs=[pl.BlockSpec((nb, tt, xw), lambda b, j: (b, j, 0)),
                  pl.BlockSpec((nb, tt, w), lambda b, j: (b, j, xw // w)),
                  pl.BlockSpec((nb, tt, LANE), lambda b, j: (b, j, GATE_DT // LANE)),
                  _const_spec((CONV_C, xw)), _const_spec((1, xw)), _const_spec((1, w)), _const_spec((1, w)),
                  _const_spec((1, w)), _const_spec((1, w)),
                  st((CONV_C - 1, xw)), st((ng, n_c, gp))],
        out_specs=[pl.BlockSpec((nb, tt, w), lambda b, j: (b, j, 0)), st((CONV_C - 1, xw)), st((ng, n_c, gp))],
        scratch_shapes=[pltpu.VMEM((nb, tt + SUBLANE, xw), F32)],
        name="ssd",
        compiler_params=_params(("parallel", "arbitrary"), est),
    )(cin, cin, gates, conv_w, conv_b, dtb, a_log_x, dskip, normg, conv0, s0)


def _gmlp_kernel(u_ref, v_ref, ng_ref, ws_ref, bs_ref, y_ref, *vn_refs):
    tt, w = u_ref.shape[1], u_ref.shape[2]
    u = jax.nn.gelu(u_ref[0])
    vr = jax.nn.gelu(v_ref[0])
    mu = jnp.mean(vr, axis=-1, keepdims=True)
    xc = vr - mu
    vn = xc * lax.rsqrt(jnp.mean(xc * xc, axis=-1, keepdims=True) + EPS) * ng_ref[...]
    for vn_ref in vn_refs:
        vn_ref[0] = vn
    rows = _iota((N_HEADS * SPAT, SPAT), 0) % SPAT
    wst = jnp.where(rows >= _iota((N_HEADS * SPAT, SPAT), 1), ws_ref[...], 0.0).astype(BF16)
    lmask = _head_lane_masks(SPAT, w)
    bias = bs_ref[...]
    vpad = _pad_rows(vn, -(-tt // SPAT) * SPAT).astype(BF16)
    for i in range(vpad.shape[0] // SPAT):
        fs = jnp.dot(wst, vpad[i * SPAT:(i + 1) * SPAT], preferred_element_type=F32)
        f = bias
        for g in range(N_HEADS):
            f = f + jnp.where(lmask[g], fs[g * SPAT:(g + 1) * SPAT], 0.0)
        n = min(SPAT, tt - i * SPAT)
        y_ref[0, i * SPAT:i * SPAT + n, :] = (u[i * SPAT:i * SPAT + n] * f[0:n]).astype(y_ref.dtype)


def _gmlp(din, normg, w_s, b_x, tt, emit_v):
    bsz, t, w2 = din.shape
    w = w2 // 2
    est = 2 * 4 * tt * w * 4 + 8 * tt * w * 4 + 4 * N_HEADS * SPAT * (SPAT + w) * 4
    out_shape = [jax.ShapeDtypeStruct((bsz, t, w), BF16)]
    if emit_v:
        out_shape.append(jax.ShapeDtypeStruct((bsz, t, w), F32))
    return pl.pallas_call(
        _gmlp_kernel,
        out_shape=out_shape,
        grid=(bsz, t // tt),
        in_specs=[pl.BlockSpec((1, tt, w), lambda b, j: (b, j, 0)),
                  pl.BlockSpec((1, tt, w), lambda b, j: (b, j, 1)),
                  _const_spec((1, w)), _const_spec((N_HEADS * SPAT, SPAT)), _const_spec((SPAT, w))],
        out_specs=[pl.BlockSpec((1, tt, w), lambda b, j: (b, j, 0))] * len(out_shape),
        name="spatial_gate",
        compiler_params=_params(("parallel", "parallel"), est),
    )(din, din, normg, w_s, b_x)


def _post_kernel(x_ref, ya_ref, yb_ref, yc_ref, yd_ref, mod_ref, g_ref, wo_ref, wg_ref, wu_ref, wd_ref,
                 cw_ref, cb_ref, f0_ref, gfin_ref, o_ref, fc_ref, ext_ref, *, final, fchunk):
    bb, tt, d = x_ref.shape
    m = bb * tt
    gw = ya_ref.shape[2]
    f = wg_ref.shape[1]
    hist = CONV_F - 1
    off = SUBLANE
    mod = mod_ref[...]
    mix = None
    for j, y_ref in enumerate((ya_ref, yb_ref, yc_ref, yd_ref)):
        p = jnp.dot(y_ref[...].reshape(m, gw), wo_ref[j * gw:(j + 1) * gw, :], preferred_element_type=F32)
        mix = p if mix is None else mix + p
    x1 = x_ref[...] + mod[:, :, 2 * d:3 * d] * mix.reshape(bb, tt, d)
    h2 = _rmsnorm_rows(x1, g_ref[...]) * (1.0 + mod[:, :, 4 * d:5 * d]) + mod[:, :, 3 * d:4 * d]
    hb = h2.reshape(m, d).astype(BF16)

    @pl.when(pl.program_id(1) == 0)
    def _():
        ext_ref[:, off - hist:off, :] = f0_ref[...]

    down = None
    for c0 in range(0, f, fchunk):
        cols = slice(c0, c0 + fchunk)
        g = jnp.dot(hb, wg_ref[:, cols], preferred_element_type=F32).reshape(bb, tt, fchunk)
        u = jnp.dot(hb, wu_ref[:, cols], preferred_element_type=F32)
        ext_ref[:, off:off + tt, cols] = g
        acc = cb_ref[:, cols] + cw_ref[CONV_F - 1:CONV_F, cols] * g
        for j in range(CONV_F - 1):
            sh = CONV_F - 1 - j
            acc = acc + cw_ref[j:j + 1, cols] * ext_ref[:, off - sh:off - sh + tt, cols]
        act = (acc * jax.nn.sigmoid(acc)).reshape(m, fchunk) * u
        p = jnp.dot(act.astype(BF16), wd_ref[cols, :], preferred_element_type=F32)
        down = p if down is None else down + p
    tail = ext_ref[:, off + tt - hist:off + tt, :]
    ext_ref[:, off - hist:off, :] = tail
    fc_ref[...] = tail
    x2 = x1 + mod[:, :, 5 * d:6 * d] * down.reshape(bb, tt, d)
    if final:
        x2 = _rmsnorm_rows(x2, gfin_ref[...])
    o_ref[...] = x2


def _post(x, ys, mod, g_ffn, w_out, w_gate, w_up, w_down, conv_w, conv_b, f0, g_final, bb, tt, final):
    bsz, t, d = x.shape
    gw = ys[0].shape[2]
    f = w_gate.shape[1]
    m = bb * tt
    tok = lambda wd: pl.BlockSpec((bb, tt, wd), lambda i, j: (i, j, 0))
    fchunk = 2 * LANE
    assert f % fchunk == 0
    est = (2 * 2 * m * d * 4 + 2 * 4 * m * gw * 2 + (d * d + 3 * d * f) * 2 + bb * (tt + SUBLANE) * f * 4
           + 8 * m * fchunk * 4 + 6 * m * d * 4)
    return pl.pallas_call(
        functools.partial(_post_kernel, final=final, fchunk=fchunk),
        out_shape=[jax.ShapeDtypeStruct((bsz, t, d), F32), jax.ShapeDtypeStruct((bsz, CONV_F - 1, f), F32)],
        grid=(bsz // bb, t // tt),
        in_specs=[tok(d), tok(gw), tok(gw), tok(gw), tok(gw),
                  pl.BlockSpec((bb, 1, mod.shape[2]), lambda i, j: (i, 0, 0)),
                  _const_spec((1, 1, d)), _const_spec((d, d)), _const_spec((d, f)), _const_spec((d, f)),
                  _const_spec((f, d)), _const_spec((CONV_F, f)), _const_spec((1, f)),
                  pl.BlockSpec((bb, CONV_F - 1, f), lambda i, j: (i, 0, 0)),
                  _const_spec((1, 1, d))],
        out_specs=[tok(d), pl.BlockSpec((bb, CONV_F - 1, f), lambda i, j: (i, 0, 0))],
        scratch_shapes=[pltpu.VMEM((bb, tt + SUBLANE, f), F32)],
        name="post_ffn",
        compiler_params=_params(("parallel", "arbitrary"), est),
    )(x, *ys, mod, g_ffn.reshape(1, 1, d), w_out, w_gate, w_up, w_down, conv_w, conv_b, f0, g_final.reshape(1, 1, d))


def _mixpost_kernel(qkv_ref, qkvo_ref, cin_ref, din_ref, gate_ref, x_ref, mod_ref, *refs, nt, final, fchunk, n_hist):
    hist_refs = refs[:2 * n_hist]
    (bias_ref, bi_ref, bf_ref, bng_ref, ccw_ref, ccb_ref, dtb_ref, alog_ref, dskip_ref, cng_ref,
     dng_ref, dws_ref, dbx_ref, gffn_ref, wo_ref, wg_ref, wu_ref, wd_ref, fcw_ref, fcb_ref, gfin_ref,
     o_ref, c_ref, n_ref, m_ref, conv_ref, s_ref, fc_ref,
     kk_ref, vv_ref, ybuf_ref, cext_ref, fext_ref, fhist_ref) = refs[2 * n_hist:]
    tt, d = x_ref.shape[1], x_ref.shape[2]
    w = d // 4
    f = wg_ref.shape[1]
    band = (N_BAND + 1) * CHUNK
    off = SUBLANE
    s = pl.program_id(0)
    n_tiles = pl.num_programs(0) - 1
    live = s < n_tiles
    mj = jnp.minimum(s, n_tiles - 1) % nt
    pj = jnp.maximum(s - 1, 0) % nt
    wslot = s % 2
    rslot = 1 - wslot

    @pl.when(s == 0)
    def _():
        ybuf_ref[...] = jnp.zeros(ybuf_ref.shape, ybuf_ref.dtype)

    @pl.when(mj == 0)
    def _():
        c_ref[...] = jnp.zeros(c_ref.shape, F32)
        n_ref[...] = jnp.zeros(n_ref.shape, F32)
        m_ref[...] = jnp.zeros(m_ref.shape, F32)
        s_ref[...] = jnp.zeros(s_ref.shape, F32)
        cext_ref[0:off, :] = jnp.zeros((off, cext_ref.shape[1]), F32)

    @pl.when(pj == 0)
    def _():
        fhist_ref[...] = jnp.zeros(fhist_ref.shape, F32)

    ex = _head_expander(w, 0)

    def attention():
        scale = (w // N_HEADS) ** -0.5
        for i, (kh_ref, vh_ref) in enumerate(zip(hist_refs[0::2], hist_refs[1::2])):
            kk_ref[i * tt:(i + 1) * tt, :] = kh_ref[0].astype(BF16)
            vv_ref[i * tt:(i + 1) * tt, :] = vh_ref[0].astype(BF16)
        kk_ref[A_WIN:A_WIN + tt, :] = qkv_ref[0, :, w:2 * w].astype(BF16)
        vv_ref[A_WIN:A_WIN + tt, :] = qkv_ref[0, :, 2 * w:3 * w].astype(BF16)
        masks = _head_lane_masks(CHUNK, w)
        bias = bias_ref[...]
        slot = _iota((1, band), 1)
        q_all = qkv_ref[0, :, 0:w].astype(F32) * scale
        yield
        for i in range(tt // CHUNK):
            base = i * CHUNK
            pos = slot + base
            kvalid = pos >= A_WIN
            for back in range(1, A_WIN // tt + 1):
                kvalid = kvalid | ((pos >= A_WIN - back * tt) & (mj >= back))
            out = yield from _attend_chunk(q_all[base:base + CHUNK], kk_ref[base:base + band, :],
                                           vv_ref[base:base + band, :], bias, kvalid, masks, masks)
            ybuf_ref[wslot, 0, base:base + CHUNK, :] = out.astype(BF16)
            yield

    def mlstm():
        kscale = (w // N_HEADS) ** -0.5
        bias_i, bias_f, ng = bi_ref[...], bf_ref[...], bng_ref[...]
        for c in range(tt // SPAT):
            rows = slice(c * SPAT, (c + 1) * SPAT)
            col = lambda c: qkvo_ref[0, rows, c * w:(c + 1) * w].astype(F32)
            gi_p = _bf16_pieces(gate_ref[0, rows, GATE_I:GATE_I + LANE], 3)
            gf_p = _bf16_pieces(gate_ref[0, rows, GATE_F:GATE_F + LANE], 3)
            yield
            y = yield from _mlstm_chunk(
                col(0), col(1) * kscale, col(2), col(3),
                _dot_pieces(gi_p, ex) + bias_i, _dot_pieces(gf_p, ex) + bias_f,
                ng, c_ref, n_ref, m_ref, 0, SPAT, live)
            ybuf_ref[wslot, 1, rows, :] = y.astype(BF16)
            yield

    def ssd():
        xw = cext_ref.shape[1]
        hist = CONV_C - 1
        cext_ref[off:off + tt, :] = cin_ref[0, :, 0:xw].astype(F32)
        acc = ccb_ref[...] + ccw_ref[CONV_C - 1:CONV_C, :] * cext_ref[off:off + tt, :]
        for j in range(CONV_C - 1):
            sh = CONV_C - 1 - j
            acc = acc + ccw_ref[j:j + 1, :] * cext_ref[off - sh:off - sh + tt, :]
            yield
        tail = cext_ref[off + tt - hist:off + tt, :]
        cext_ref[off - hist:off, :] = tail
        conv_ref[0] = tail
        cext_ref[off:off + tt, :] = acc * jax.nn.sigmoid(acc)
        yield
        a_x = -jnp.exp(alog_ref[...])
        dtb, dskip, ng = dtb_ref[...], dskip_ref[...], cng_ref[...]
        for c in range(tt // SPAT):
            rows = slice(c * SPAT, (c + 1) * SPAT)
            erows = slice(off + c * SPAT, off + (c + 1) * SPAT)
            dt_p = _bf16_pieces(gate_ref[0, rows, GATE_DT:GATE_DT + LANE], 3)
            yield
            dt = jax.nn.softplus(_dot_pieces(dt_p, ex) + dtb)
            y = yield from _ssd_chunk(cext_ref[erows, 0:w], cext_ref[erows, w:2 * w], cext_ref[erows, 2 * w:3 * w],
                                      cin_ref[0, rows, xw:xw + w].astype(F32), dt, a_x, dskip, ng, s_ref, 0, live)
            ybuf_ref[wslot, 2, rows, :] = y.astype(BF16)
            yield

    def gmlp():
        u = jax.nn.gelu(din_ref[0, :, 0:w].astype(F32))
        vr = jax.nn.gelu(din_ref[0, :, w:2 * w].astype(F32))
        yield
        xc = vr - jnp.mean(vr, axis=-1, keepdims=True)
        vn = (xc * lax.rsqrt(jnp.mean(xc * xc, axis=-1, keepdims=True) + EPS) * dng_ref[...]).astype(BF16)
        rows = _iota((N_HEADS * SPAT, SPAT), 0) % SPAT
        wst = jnp.where(rows >= _iota((N_HEADS * SPAT, SPAT), 1), dws_ref[...], 0.0).astype(BF16)
        lmask = _head_lane_masks(SPAT, w)
        yield
        for i in range(tt // SPAT):
            fs = jnp.dot(wst, vn[i * SPAT:(i + 1) * SPAT], preferred_element_type=F32)
            yield
            fgate = dbx_ref[...]
            for g in range(N_HEADS):
                fgate = fgate + jnp.where(lmask[g], fs[g * SPAT:(g + 1) * SPAT], 0.0)
            ybuf_ref[wslot, 3, i * SPAT:(i + 1) * SPAT, :] = (u[i * SPAT:(i + 1) * SPAT] * fgate).astype(BF16)
            yield

    def post():
        hist = CONV_F - 1
        mod = mod_ref[0]
        mix = None
        for j in range(4):
            p = jnp.dot(ybuf_ref[rslot, j], wo_ref[j * w:(j + 1) * w, :], preferred_element_type=F32)
            mix = p if mix is None else mix + p
            yield
        x1 = x_ref[0] + mod[:, 2 * d:3 * d] * mix
        h2 = _rmsnorm_rows(x1, gffn_ref[...]) * (1.0 + mod[:, 4 * d:5 * d]) + mod[:, 3 * d:4 * d]
        hb = h2.astype(BF16)
        yield
        down = None
        act_b16, act_cols = None, None
        for ci, c0 in enumerate(range(0, f, fchunk)):
            cols = slice(c0, c0 + fchunk)
            g = jnp.dot(hb, wg_ref[:, cols], preferred_element_type=F32)
            u = jnp.dot(hb, wu_ref[:, cols], preferred_element_type=F32)
            if act_b16 is not None:
                p = jnp.dot(act_b16, wd_ref[act_cols, :], preferred_element_type=F32)
                down = p if down is None else down + p
            yield
            buf = ci % 2
            fext_ref[buf, off - hist:off, :] = fhist_ref[0, off - hist:off, cols]
            fext_ref[buf, off:off + tt, :] = g
            acc = fcb_ref[:, cols] + fcw_ref[CONV_F - 1:CONV_F, cols] * g
            for j in range(CONV_F - 1):
                sh = CONV_F - 1 - j
                acc = acc + fcw_ref[j:j + 1, cols] * fext_ref[buf, off - sh:off - sh + tt, :]
            fhist_ref[0, off - hist:off, cols] = g[tt - hist:tt]
            act_b16, act_cols = ((acc * jax.nn.sigmoid(acc)) * u).astype(BF16), cols
            yield
        down = down + jnp.dot(act_b16, wd_ref[act_cols, :], preferred_element_type=F32)
        fc_ref[...] = fhist_ref[:, off - hist:off, :]
        x2 = x1 + mod[:, 5 * d:6 * d] * down
        if final:
            x2 = _rmsnorm_rows(x2, gfin_ref[...])
        o_ref[0] = x2

    _interleave([post(), mlstm(), ssd(), attention(), gmlp()])


def _mix_post(qkv_a, qkvo_b, c_in, d_in, gates, x, mod, p, g_final, tt, final):
    bsz, t, d = x.shape
    w = d // 4
    nt = t // tt
    n_tiles = bsz * nt
    f = p["f_w_gate"].shape[1]
    xw = c_in.shape[2] - w
    ng, n_c, gp = 2, LANE, w // 2
    band = (N_BAND + 1) * CHUNK
    fchunk = 2 * LANE
    assert f % fchunk == 0 and tt % SPAT == 0 and t % tt == 0
    mix_tile = lambda width: pl.BlockSpec(
        (1, tt, width), lambda s: (jnp.minimum(s, n_tiles - 1) // nt, jnp.minimum(s, n_tiles - 1) % nt, 0))
    post_tile = lambda width: pl.BlockSpec(
        (1, tt, width), lambda s: (jnp.maximum(s - 1, 0) // nt, jnp.maximum(s - 1, 0) % nt, 0))
    mix_state = lambda shape: pl.BlockSpec(
        (1,) + shape, lambda s: (jnp.minimum(s, n_tiles - 1) // nt,) + (0,) * len(shape))
    post_state = lambda shape: pl.BlockSpec(
        (1,) + shape, lambda s: (jnp.maximum(s - 1, 0) // nt,) + (0,) * len(shape))
    consts = [p["bias_tab"].reshape(N_HEADS * CHUNK, band), p["bi"], p["bf"], p["b_norm_g"], p["c_conv_w"],
              p["c_conv_b"], p["dtb"], p["a_log_x"], p["dskip"], p["c_norm_g"], p["d_norm_g"], p["d_w_s"], p["d_b_x"],
              p["g_ffn"].reshape(1, d), p["w_out"], p["f_w_gate"], p["f_w_up"], p["f_w_down"], p["f_conv_w"],
              p["f_conv_b"], g_final.reshape(1, d)]
    in_widths = (qkv_a.shape[2], qkvo_b.shape[2], c_in.shape[2], d_in.shape[2], gates.shape[2])
    n_hist = A_WIN // tt
    assert n_hist * tt == A_WIN

    def hist_tile(back, col):
        def index(s):
            m = jnp.minimum(s, n_tiles - 1)
            return (m // nt, jnp.maximum(m % nt - back, 0), col)
        return pl.BlockSpec((1, tt, w), index)

    hist_specs = [hist_tile(back, col) for back in range(n_hist, 0, -1) for col in (1, 2)]
    in_bytes = sum(a.shape[2] * a.dtype.itemsize for a in (qkv_a, qkvo_b, c_in, d_in, gates))
    est = ((d * d + 3 * d * f) * 2 + 2 * tt * in_bytes + 4 * tt * d * 4 + (tt + SUBLANE) * (2 * fchunk + xw) * 4
           + 2 * (A_WIN + tt) * w * 2 + 8 * tt * w * 2 + 12 * tt * d * 4 + 6 * w * w * 4
           + 4 * n_hist * tt * w * qkv_a.dtype.itemsize)
    return pl.pallas_call(
        functools.partial(_mixpost_kernel, nt=nt, final=final, fchunk=fchunk, n_hist=n_hist),
        out_shape=[jax.ShapeDtypeStruct((bsz, t, d), F32),
                   jax.ShapeDtypeStruct((bsz, w, w), F32),
                   jax.ShapeDtypeStruct((bsz, 1, w), F32),
                   jax.ShapeDtypeStruct((bsz, 1, w), F32),
                   jax.ShapeDtypeStruct((bsz, CONV_C - 1, xw), F32),
                   jax.ShapeDtypeStruct((bsz, ng, n_c, gp), F32),
                   jax.ShapeDtypeStruct((bsz, CONV_F - 1, f), F32)],
        grid=(n_tiles + 1,),
        in_specs=[mix_tile(wd) for wd in in_widths] + [post_tile(d), post_state((1, mod.shape[2]))] + hist_specs
                 + [_const_spec(c.shape) for c in consts],
        out_specs=[post_tile(d), mix_state((w, w)), mix_state((1, w)), mix_state((1, w)),
                   mix_state((CONV_C - 1, xw)), mix_state((ng, n_c, gp)), post_state((CONV_F - 1, f))],
        scratch_shapes=[pltpu.VMEM((A_WIN + tt, w), BF16), pltpu.VMEM((A_WIN + tt, w), BF16),
                        pltpu.VMEM((2, 4, tt, w), BF16), pltpu.VMEM((tt + SUBLANE, xw), F32),
                        pltpu.VMEM((2, tt + SUBLANE, fchunk), F32), pltpu.VMEM((1, SUBLANE, f), F32)],
        name="mix_post",
        compiler_params=_params(("arbitrary",), est),
    )(qkv_a, qkvo_b, c_in, d_in, gates, x, mod, *([qkv_a] * len(hist_specs)), *consts)


def _layer(x, mod, st, p, tiles, final, g_final):
    bsz, t, d = x.shape
    gw = d // 4
    dh = gw // N_HEADS
    bb, tt, tmix, nb = tiles
    mod3 = mod.reshape(bsz, 1, mod.shape[1])
    dtypes = (BF16,) * 4 + (F32,) if st is None else (F32,) * 5
    qkv_a, qkvo_b, c_in, d_in, gates = _in_proj(x, mod3, p["g_mix"], p["w_in"], p["widths"], dtypes, bb, tt)

    if st is None:
        x_new, c_new, n_new, m_new, conv_new, ssm_new, fconv_new = _mix_post(
            qkv_a, qkvo_b, c_in, d_in, gates, x, mod3, p, g_final, tmix, final)
        vn = []
    else:
        y_a = _attention(qkv_a, st["a_k"], st["a_v"], p["bias_tab"], tmix)
        y_b, c_new, n_new, m_new = _mlstm(qkvo_b, gates, p["bi"], p["bf"], p["b_norm_g"], st["b_c"], st["b_n"],
                                          st["b_m"], nb, tmix)
        y_c, conv_new, ssm_new = _ssd(c_in, gates, p["c_conv_w"], p["c_conv_b"], p["dtb"], p["a_log_x"], p["dskip"],
                                      p["c_norm_g"], st["c_conv"], st["c_ssm"], nb, tmix)
        y_d, *vn = _gmlp(d_in, p["d_norm_g"], p["d_w_s"], p["d_b_x"], tmix, True)
        x_new, fconv_new = _post(x, (y_a, y_b, y_c, y_d), mod3, p["g_ffn"], p["w_out"], p["f_w_gate"], p["f_w_up"],
                                 p["f_w_down"], p["f_conv_w"], p["f_conv_b"], st["f_conv"], g_final, bb, tt, final)

    keep = min(A_WIN, t)
    new_k = qkv_a[:, t - keep:, gw:2 * gw].reshape(bsz, keep, N_HEADS, dh).astype(F32)
    new_v = qkv_a[:, t - keep:, 2 * gw:3 * gw].reshape(bsz, keep, N_HEADS, dh).astype(F32)
    idx = jnp.arange(N_HEADS)
    c_heads = c_new.reshape(bsz, N_HEADS, dh, N_HEADS, dh)[:, idx, :, idx, :]
    c_heads = jnp.moveaxis(c_heads, 0, 1)
    n_heads = n_new.reshape(bsz, N_HEADS, dh)
    m_heads = m_new[:, 0, ::dh]
    ssm = jnp.swapaxes(ssm_new, 2, 3).reshape(bsz, N_HEADS, gw // N_HEADS, ssm_new.shape[2])
    outs = (new_k, new_v, c_heads, n_heads, m_heads, ssm, conv_new, fconv_new, *vn)
    return x_new, outs


def _prep_layer(l, w_in, w_out, a_rel_bias, b_i_bias, b_f_bias, b_norm_g, c_conv_w, c_conv_b, c_dt_bias, c_a_log,
                c_d_skip, c_norm_g, d_norm_g, d_w_s, d_b_s, f_w_gate, f_w_up, f_conv_w, f_conv_b, f_w_down,
                g_norm_mix, g_norm_ffn):
    d = w_in.shape[1]
    gw = d // 4
    nh = b_i_bias.shape[1]
    xbc_w = c_conv_w.shape[2]
    sizes = (gw,) * 7 + (nh, nh, gw, xbc_w, c_dt_bias.shape[1], gw, gw)
    offs = [0]
    for s in sizes:
        offs.append(offs[-1] + s)
    moves, dst = [], 0
    for i in (0, 1, 2, 3, 4, 5, 6, 10, 9, 12, 13):
        moves.append((offs[i], dst, sizes[i]))
        dst += sizes[i]
    assert sizes[7] == sizes[8] == sizes[11] == N_HEADS
    packed = (dst, ((offs[7], GATE_I, N_HEADS), (offs[8], GATE_F, N_HEADS), (offs[11], GATE_DT, N_HEADS)))
    widths = (3 * gw, 4 * gw, xbc_w + gw, 2 * gw, GATE_W)
    dst += GATE_W
    assert dst == sum(widths)
    heads_x = lambda v: jnp.repeat(v.astype(F32), gw // nh)[None, :]
    return dict(
        w_in=_regroup_cast(w_in, l, moves, packed, dst), widths=widths, g_mix=g_norm_mix[l], g_ffn=g_norm_ffn[l],
        w_out=w_out[l].astype(BF16), bias_tab=_bias_table(a_rel_bias[l]),
        bi=heads_x(b_i_bias[l]), bf=heads_x(b_f_bias[l]), b_norm_g=b_norm_g[l][None, :],
        c_conv_w=c_conv_w[l], c_conv_b=c_conv_b[l][None, :], dtb=heads_x(c_dt_bias[l]),
        a_log_x=heads_x(c_a_log[l]), dskip=heads_x(c_d_skip[l]), c_norm_g=c_norm_g[l][None, :],
        d_norm_g=d_norm_g[l][None, :], d_w_s=d_w_s[l].reshape(-1, d_w_s.shape[-1]),
        d_b_x=jnp.repeat(d_b_s[l].T, gw // d_b_s.shape[1], axis=1),
        f_w_gate=f_w_gate[l].astype(BF16), f_w_up=f_w_up[l].astype(BF16), f_w_down=f_w_down[l].astype(BF16),
        f_conv_w=f_conv_w[l], f_conv_b=f_conv_b[l][None, :],
    )


def _blockdiag_state(c):
    bsz, nh, dh, _ = c.shape
    eye = jnp.eye(nh, dtype=c.dtype)
    return jnp.einsum("bhde,hg->bhdge", c, eye).reshape(bsz, nh * dh, nh * dh)


def kernel(x_prompt, x_sample, c_prompt, c_sample, cache_a_k, cache_a_v, state_b_c, state_b_n, state_b_m, state_c_ssm, state_c_conv, state_ffn_conv, w_ada, b_ada, g_norm_mix, g_norm_ffn, w_in, w_out, a_rel_bias, b_i_bias, b_f_bias, b_norm_g, c_conv_w, c_conv_b, c_dt_bias, c_a_log, c_d_skip, c_norm_g, d_norm_g, d_w_s, d_b_s, f_w_gate, f_w_up, f_conv_w, f_conv_b, f_w_down, g_final):
    depth = w_in.shape[0]
    bp, tp, d = x_prompt.shape
    bs, ts, _ = x_sample.shape
    gw = d // 4
    dh = gw // N_HEADS
    xbc_w = c_conv_w.shape[2]
    n_c = state_c_ssm.shape[-1]
    g_c = (xbc_w - gw) // 2 // n_c

    mod_all = _ada(jnp.concatenate([c_prompt, c_sample], axis=0), w_ada, b_ada)

    nb = max(n for n in (4, 2, 1) if bs % n == 0)
    tiles_p = (1, min(A_WIN, tp), min(A_WIN, tp), 1)
    tiles_s = (bs, ts, ts, nb)

    xp, xs = x_prompt, x_sample
    p_states, s_states = [], []
    for l in range(depth):
        p = _prep_layer(l, w_in, w_out, a_rel_bias, b_i_bias, b_f_bias, b_norm_g, c_conv_w, c_conv_b, c_dt_bias,
                        c_a_log, c_d_skip, c_norm_g, d_norm_g, d_w_s, d_b_s, f_w_gate, f_w_up, f_conv_w, f_conv_b,
                        f_w_down, g_norm_mix, g_norm_ffn)
        final = l == depth - 1
        xp, sp = _layer(xp, mod_all[l, :bp], None, p, tiles_p, final, g_final)
        st_s = dict(a_k=cache_a_k[l].reshape(bs, -1, gw), a_v=cache_a_v[l].reshape(bs, -1, gw),
                    b_c=_blockdiag_state(state_b_c[l]), b_n=state_b_n[l].reshape(bs, 1, gw),
                    b_m=jnp.repeat(state_b_m[l], dh, axis=1)[:, None, :],
                    c_conv=state_c_conv[l],
                    c_ssm=jnp.swapaxes(state_c_ssm[l].reshape(bs, g_c, gw // g_c, n_c), 2, 3),
                    f_conv=state_ffn_conv[l])
        xs, ss = _layer(xs, mod_all[l, bp:], st_s, p, tiles_s, final, g_final)
        p_states.append(sp)
        s_states.append(ss)

    stack = lambda states, i: jnp.stack([s[i] for s in states])
    return (xp, xs,
            *(stack(p_states, i) for i in range(8)),
            *(stack(s_states, i) for i in range(9)))
```

```python
import functools

import jax
import jax.numpy as jnp
from jax import lax
from jax.experimental import pallas as pl
from jax.experimental.pallas import tpu as pltpu

F32 = jnp.float32
BF16 = jnp.bfloat16

EPS = 1e-6
NEG = -1e30

CHUNK = 64
N_BAND = 8
A_WIN = N_BAND * CHUNK
REL_CLIP = 128
N_HEADS = 4
SPAT = 128
CONV_C = 4
CONV_F = 3
LANE = 128
GATE_I, GATE_F, GATE_DT = 0, LANE, 2 * LANE
GATE_W = 3 * LANE
GATE_PIECES = 2
SUBLANE = 8
VMEM_CAP = 64 * 1024 * 1024


def _vmem_limit(nbytes):
    return int(min(max(nbytes, 16 * 1024 * 1024), VMEM_CAP - 8 * 1024 * 1024))


def _params(sem, nbytes):
    return pltpu.CompilerParams(dimension_semantics=sem, vmem_limit_bytes=_vmem_limit(nbytes))


def _const_spec(shape):
    nd = len(shape)
    return pl.BlockSpec(shape, lambda *_: (0,) * nd, pipeline_mode=pl.Buffered(1))


def _iota(shape, dim):
    return lax.broadcasted_iota(jnp.int32, shape, dim)


def _bf16_pieces(x, parts):
    out = []
    r = x
    for i in range(parts):
        hi = r.astype(BF16)
        out.append(hi)
        if i + 1 < parts:
            r = r - hi.astype(F32)
    return out


def _dot_pieces(pieces, e):
    acc = None
    for piece in pieces:
        d = jnp.dot(piece, e, preferred_element_type=F32)
        acc = d if acc is None else acc + d
    return acc


def _dot_pieces_left(e, pieces):
    acc = None
    for piece in pieces:
        d = jnp.dot(e, piece, preferred_element_type=F32)
        acc = d if acc is None else acc + d
    return acc


def _split_dot(x, e, parts):
    return _dot_pieces(_bf16_pieces(x, parts), e)


def _dot_nt(a, b):
    return lax.dot_general(a, b, (((1,), (1,)), ((), ())), preferred_element_type=F32)


def _dot_tn(a, b):
    return lax.dot_general(a, b, (((0,), (0,)), ((), ())), preferred_element_type=F32)


def _head_expander(width, first):
    dh = width // N_HEADS
    return (_iota((LANE, width), 1) // dh == _iota((LANE, width), 0) - first).astype(BF16)


def _head_lane_masks(rows, width):
    dh = width // N_HEADS
    lane = _iota((rows, width), 1)
    return [(lane >= h * dh) & (lane < (h + 1) * dh) for h in range(N_HEADS)]


def _tril(n):
    return _iota((n, n), 0) >= _iota((n, n), 1)


def _pad_rows(x, rows):
    if x.shape[0] == rows:
        return x
    return jnp.concatenate([x, jnp.zeros((rows - x.shape[0], x.shape[1]), x.dtype)], axis=0)


def _interleave(gens):
    results = [None] * len(gens)
    live = list(range(len(gens)))
    while live:
        for i in list(live):
            try:
                next(gens[i])
            except StopIteration as stop:
                results[i] = stop.value
                live.remove(i)
    return results


def _rmsnorm_rows(x, g):
    return x * lax.rsqrt(jnp.mean(x * x, axis=-1, keepdims=True) + EPS) * g


def _regroup_kernel(w_ref, o_ref, *, moves, packed):
    w = w_ref[0]
    rows = w.shape[0]
    for src, dst, n in moves:
        o_ref[:, dst:dst + n] = w[:, src:src + n].astype(o_ref.dtype)
    dst, pieces = packed
    parts, lane = [], 0
    for src, first, n in pieces:
        if first > lane:
            parts.append(jnp.zeros((rows, first - lane), F32))
        parts.append(w[:, src:src + n])
        lane = first + n
    parts.append(jnp.zeros((rows, GATE_W - lane), F32))
    o_ref[:, dst:dst + GATE_W] = jnp.concatenate(parts, axis=1).astype(o_ref.dtype)


def _regroup_cast(w_all, layer, moves, packed, ncols, row_block=256):
    _, r, c = w_all.shape
    return pl.pallas_call(
        functools.partial(_regroup_kernel, moves=tuple(moves), packed=packed),
        out_shape=jax.ShapeDtypeStruct((r, ncols), BF16),
        grid=(r // row_block,),
        in_specs=[pl.BlockSpec((1, row_block, c), lambda i: (layer, i, 0))],
        out_specs=pl.BlockSpec((row_block, ncols), lambda i: (i, 0)),
        name="regroup_cast",
        compiler_params=_params(("parallel",), 2 * row_block * (c * 4 + ncols * 2) + row_block * c * 4),
    )(w_all)


def _ada_kernel(c_ref, w_ref, b_ref, o_ref):
    c = c_ref[...]
    h = (c * jax.nn.sigmoid(c)).astype(BF16)
    o_ref[0] = jnp.dot(h, w_ref[0].astype(BF16), preferred_element_type=F32) + b_ref[0]


def _ada(c_all, w_ada, b_ada):
    depth, d, n6 = w_ada.shape
    r = c_all.shape[0]
    tn = d
    return pl.pallas_call(
        _ada_kernel,
        out_shape=jax.ShapeDtypeStruct((depth, r, n6), F32),
        grid=(depth, n6 // tn),
        in_specs=[pl.BlockSpec((r, d), lambda l, j: (0, 0)),
                  pl.BlockSpec((1, d, tn), lambda l, j: (l, 0, j)),
                  pl.BlockSpec((1, 1, tn), lambda l, j: (l, 0, j))],
        out_specs=pl.BlockSpec((1, r, tn), lambda l, j: (l, 0, j)),
        name="ada_mod",
        compiler_params=_params(("parallel", "parallel"), 4 * (2 * d * tn * 4 + 2 * r * tn * 4 + r * d * 4)),
    )(c_all, w_ada, b_ada.reshape(depth, 1, n6))


def _bias_kernel(rb_ref, o_ref, *, lo, hi):
    nh, lq, lk = o_ref.shape
    idx = jnp.clip(A_WIN + _iota((lq, lk), 0) - _iota((lq, lk), 1), -REL_CLIP, REL_CLIP) + REL_CLIP
    for h in range(nh):
        def body(r, acc, h=h):
            return jnp.where(idx == r, rb_ref[h, r], acc)
        o_ref[h] = lax.fori_loop(lo, hi + 1, body, jnp.zeros((lq, lk), F32))


def _bias_table(rel_bias):
    nh = rel_bias.shape[0]
    band = (N_BAND + 1) * CHUNK
    lo = max(A_WIN - (band - 1), -REL_CLIP) + REL_CLIP
    hi = min(A_WIN + CHUNK - 1, REL_CLIP) + REL_CLIP
    return pl.pallas_call(
        functools.partial(_bias_kernel, lo=lo, hi=hi),
        out_shape=jax.ShapeDtypeStruct((nh, CHUNK, band), F32),
        in_specs=[pl.BlockSpec(memory_space=pltpu.SMEM)],
        out_specs=pl.BlockSpec(memory_space=pltpu.VMEM),
        name="rel_bias_table",
    )(rel_bias)


def _in_kernel(x_ref, mod_ref, g_ref, w_ref, *o_refs, col_starts):
    bb, tt, d = x_ref.shape
    x = x_ref[...]
    mod = mod_ref[...]
    h = _rmsnorm_rows(x, g_ref[...]) * (1.0 + mod[:, :, d:2 * d]) + mod[:, :, 0:d]
    hb = h.reshape(bb * tt, d).astype(BF16)
    for o_ref, (a, b) in zip(o_refs, col_starts):
        o_ref[...] = jnp.dot(hb, w_ref[:, a:b], preferred_element_type=F32).reshape(bb, tt, b - a).astype(o_ref.dtype)


def _in_proj(x, mod, g, w, widths, dtypes, bb, tt):
    bsz, t, d = x.shape
    ncols = w.shape[1]
    starts, a = [], 0
    for wd in widths:
        starts.append((a, a + wd))
        a += wd
    m = bb * tt
    est = 2 * m * d * 4 + 2 * d * ncols * 2 + 2 * m * ncols * 4 + 3 * m * d * 4 + m * max(widths) * 4
    return pl.pallas_call(
        functools.partial(_in_kernel, col_starts=tuple(starts)),
        out_shape=[jax.ShapeDtypeStruct((bsz, t, wd), dt) for wd, dt in zip(widths, dtypes)],
        grid=(bsz // bb, t // tt),
        in_specs=[pl.BlockSpec((bb, tt, d), lambda i, j: (i, j, 0)),
                  pl.BlockSpec((bb, 1, mod.shape[2]), lambda i, j: (i, 0, 0)),
                  _const_spec((1, 1, d)),
                  _const_spec((d, ncols))],
        out_specs=[pl.BlockSpec((bb, tt, wd), lambda i, j: (i, j, 0)) for wd in widths],
        name="in_proj",
        compiler_params=_params(("parallel", "parallel"), est),
    )(x, mod, g.reshape(1, 1, d), w)


def _attend_chunk(qc, kb, vb, bias, kvalid, masks_q, masks_o):
    lq = qc.shape[0]
    qs = jnp.concatenate([jnp.where(mk, qc, 0.0) for mk in masks_q], axis=0).astype(BF16)
    yield
    s = _dot_nt(qs, kb) + bias
    yield
    if kvalid is not None:
        s = jnp.where(kvalid, s, NEG)
    e = jnp.exp(s - jnp.max(s, axis=-1, keepdims=True))
    eb = e.astype(BF16)
    rinv = 1.0 / jnp.sum(e, axis=-1, keepdims=True)
    yield
    o = jnp.dot(eb, vb, preferred_element_type=F32)
    yield
    o = o * rinv
    out = jnp.where(masks_o[0], o[0:lq], 0.0)
    for h in range(1, N_HEADS):
        out = out + jnp.where(masks_o[h], o[h * lq:(h + 1) * lq], 0.0)
    return out


def _attn_kernel(q_ref, k_ref, v_ref, kh_ref, vh_ref, bias_ref, y_ref, kk_ref, vv_ref, *, hist_is_cache, t_valid):
    tq, w = q_ref.shape[1], q_ref.shape[2]
    tpad = kk_ref.shape[0] - A_WIN
    scale = (w // N_HEADS) ** -0.5
    kk_ref[0:A_WIN, :] = kh_ref[0].astype(BF16)
    vv_ref[0:A_WIN, :] = vh_ref[0].astype(BF16)
    kk_ref[A_WIN:A_WIN + tpad, :] = _pad_rows(k_ref[0], tpad).astype(BF16)
    vv_ref[A_WIN:A_WIN + tpad, :] = _pad_rows(v_ref[0], tpad).astype(BF16)
    masks = _head_lane_masks(CHUNK, w)
    bias = bias_ref[...]
    band = (N_BAND + 1) * CHUNK
    slot = _iota((1, band), 1)
    hist_ok = jnp.logical_or(pl.program_id(1) > 0, hist_is_cache)
    q_all = _pad_rows(q_ref[0], tpad) * scale
    gens = []
    for i in range(tpad // CHUNK):
        base = i * CHUNK
        pos = slot + base
        kvalid = (pos < A_WIN + t_valid) & ((pos >= A_WIN) | hist_ok)
        gens.append(_attend_chunk(q_all[base:base + CHUNK], kk_ref[base:base + band, :], vv_ref[base:base + band, :],
                                  bias, kvalid, masks, masks))
    for i, out in enumerate(_interleave(gens)):
        base = i * CHUNK
        rows = min(CHUNK, tq - base)
        y_ref[0, base:base + rows, :] = out[0:rows].astype(y_ref.dtype)


def _attention(qkv, k_hist, v_hist, bias_tab, tq):
    bsz, t, w3 = qkv.shape
    w = w3 // 3
    hist_is_cache = k_hist is not None
    nt = t // tq
    tpad = -(-tq // CHUNK) * CHUNK
    band = (N_BAND + 1) * CHUNK
    if hist_is_cache:
        assert nt == 1
        hist_specs = [pl.BlockSpec((1, A_WIN, w), lambda b, j: (b, 0, 0))] * 2
        hist_args = (k_hist, v_hist)
    else:
        assert tq == A_WIN
        hist_specs = [pl.BlockSpec((1, tq, w), lambda b, j: (b, jnp.maximum(j - 1, 0), 1)),
                      pl.BlockSpec((1, tq, w), lambda b, j: (b, jnp.maximum(j - 1, 0), 2))]
        hist_args = (qkv, qkv)
    est = 2 * 5 * tpad * w * 4 + 2 * tq * w * 4 + 2 * (A_WIN + tpad) * w * 2 + 12 * N_HEADS * CHUNK * band * 4
    return pl.pallas_call(
        functools.partial(_attn_kernel, hist_is_cache=hist_is_cache, t_valid=tq if nt == 1 else tpad),
        out_shape=jax.ShapeDtypeStruct((bsz, t, w), BF16),
        grid=(bsz, nt),
        in_specs=[pl.BlockSpec((1, tq, w), lambda b, j: (b, j, 0)),
                  pl.BlockSpec((1, tq, w), lambda b, j: (b, j, 1)),
                  pl.BlockSpec((1, tq, w), lambda b, j: (b, j, 2)),
                  *hist_specs,
                  _const_spec((N_HEADS * CHUNK, band))],
        out_specs=pl.BlockSpec((1, tq, w), lambda b, j: (b, j, 0)),
        scratch_shapes=[pltpu.VMEM((A_WIN + tpad, w), BF16), pltpu.VMEM((A_WIN + tpad, w), BF16)],
        name="band_attention",
        compiler_params=_params(("parallel", "parallel"), est),
    )(qkv, qkv, qkv, *hist_args, bias_tab.reshape(N_HEADS * CHUNK, band))


def _mlstm_chunk(q, k, v, og, gi, gf, normg, c_ref, n_ref, m_ref, bi, n_valid, live=None):
    L, w = q.shape
    dh = w // N_HEADS
    lmask = _head_lane_masks(L, w)
    tril = _tril(L)
    blockdiag = (_iota((w, w), 0) // dh) == (_iota((w, w), 1) // dh)
    bd = blockdiag.astype(BF16)

    m_prev = m_ref[bi]
    c_old = c_ref[bi]
    n_old = n_ref[bi]
    qb16 = q.astype(BF16)
    kb16 = k.astype(BF16)
    vb16 = v.astype(BF16)
    cb16 = c_old.astype(BF16)
    qmask = [jnp.where(lmask[h], q, 0.0).astype(BF16) for h in range(N_HEADS)]
    lf_p = _bf16_pieces(jax.nn.log_sigmoid(gf), 3)
    qn_p = _bf16_pieces(q * n_old, 2)
    yield
    b = _dot_pieces_left(tril.astype(BF16), lf_p)
    q_c = jnp.dot(qb16, cb16, preferred_element_type=F32)
    q_n = _dot_pieces(qn_p, bd)
    yield
    qk = [_dot_nt(qmask[h], kb16) for h in range(N_HEADS)]
    u = gi - b
    u_t = u.T
    yield
    cm = jnp.zeros((L, w), F32)
    for h in range(N_HEADS):
        cmh = jnp.max(jnp.where(tril, u_t[h * dh:h * dh + 1, :], NEG), axis=1, keepdims=True)
        cm = jnp.where(lmask[h], cmh, cm)
    mx = jnp.maximum(m_prev, cm)
    m_t = b + mx
    inter = jnp.exp(m_prev - mx)
    last = n_valid - 1
    m_last = m_t[last:last + 1, :]
    b_last = b[last:last + 1, :]
    decay = jnp.exp(b_last + m_prev - m_last)
    ws = jnp.exp(u + (b_last - m_last))
    if n_valid < L:
        ws = jnp.where(_iota((L, w), 0) < n_valid, ws, 0.0)
    kw = k * ws
    kwb16 = kw.astype(BF16)
    yield
    v_heads = jnp.concatenate([jnp.where(lmask[h], v, 0.0).astype(BF16) for h in range(N_HEADS)], axis=0)
    rs = jnp.zeros((L, w), F32)
    wq = []
    for h in range(N_HEADS):
        arg = jnp.where(tril, u_t[h * dh:h * dh + 1, :] - mx[:, h * dh:h * dh + 1], NEG)
        wqk = jnp.exp(arg) * qk[h]
        rs = jnp.where(lmask[h], jnp.sum(wqk, axis=1, keepdims=True), rs)
        wq.append(wqk.astype(BF16))
        yield
    num = jnp.dot(jnp.concatenate(wq, axis=1), v_heads, preferred_element_type=F32)
    upd = _dot_tn(kwb16, vb16)
    den = inter * q_n + rs
    hout = (inter * q_c + num) / jnp.maximum(jnp.abs(den), jnp.exp(-m_t))
    mu_p = _bf16_pieces(hout, 2)
    yield
    xc = hout - _dot_pieces(mu_p, bd) * (1.0 / dh)
    var_p = _bf16_pieces(xc * xc, 2)
    yield
    var = _dot_pieces(var_p, bd) * (1.0 / dh)
    yield
    y = jax.nn.sigmoid(og) * (xc * lax.rsqrt(var + EPS) * normg)
    keep = (lambda new, old: new) if live is None else (lambda new, old: jnp.where(live, new, old))
    c_ref[bi] = keep(decay * c_old + jnp.where(blockdiag, upd, 0.0), c_old)
    n_ref[bi] = keep(decay * n_old + jnp.sum(kw, axis=0, keepdims=True), n_old)
    m_ref[bi] = keep(m_last, m_prev)
    return y


def _mlstm_kernel(q_ref, k_ref, v_ref, o_ref, g_ref, bi_ref, bf_ref, ng_ref,
                  c0_ref, n0_ref, m0_ref, y_ref, c_ref, n_ref, m_ref):
    nb, tt, w = q_ref.shape
    kscale = (w // N_HEADS) ** -0.5

    @pl.when(pl.program_id(1) == 0)
    def _():
        c_ref[...] = c0_ref[...]
        n_ref[...] = n0_ref[...]
        m_ref[...] = m0_ref[...]

    bias_i = bi_ref[...]
    bias_f = bf_ref[...]
    ng = ng_ref[...]
    ex = _head_expander(w, 0)

    def chunk(bi, rows, n_valid):
        pad = lambda r: _pad_rows(r[bi, rows, :], SPAT)
        g = pad(g_ref)
        return _mlstm_chunk(pad(q_ref), pad(k_ref) * kscale, pad(v_ref), pad(o_ref),
                            _split_dot(g[:, GATE_I:GATE_I + LANE], ex, 3) + bias_i,
                            _split_dot(g[:, GATE_F:GATE_F + LANE], ex, 3) + bias_f,
                            ng, c_ref, n_ref, m_ref, bi, n_valid)

    if tt < SPAT:
        for bi, y in enumerate(_interleave([chunk(bi, slice(None), tt) for bi in range(nb)])):
            y_ref[bi] = y[0:tt].astype(y_ref.dtype)
    else:
        def body(i, carry):
            rows = pl.ds(pl.multiple_of(i * SPAT, SPAT), SPAT)
            for bi, y in enumerate(_interleave([chunk(bi, rows, SPAT) for bi in range(nb)])):
                y_ref[bi, rows, :] = y.astype(y_ref.dtype)
            return carry
        lax.fori_loop(0, tt // SPAT, body, 0)


def _mlstm(qkvo, gates, bias_i, bias_f, normg, c0, n0, m0, nb, tt):
    bsz, t, w4 = qkvo.shape
    w = w4 // 4
    blk = lambda c: pl.BlockSpec((nb, tt, w), lambda b, j: (b, j, c))
    st = lambda shape: pl.BlockSpec((nb,) + shape, lambda b, j: (b, 0, 0))
    est = nb * (2 * 5 * tt * w * 4 + 4 * tt * LANE * 4 + 6 * w * w * 4 + 40 * SPAT * w * 4 + 24 * SPAT * SPAT * 4)
    return pl.pallas_call(
        _mlstm_kernel,
        out_shape=[jax.ShapeDtypeStruct((bsz, t, w), BF16),
                   jax.ShapeDtypeStruct((bsz, w, w), F32),
                   jax.ShapeDtypeStruct((bsz, 1, w), F32),
                   jax.ShapeDtypeStruct((bsz, 1, w), F32)],
        grid=(bsz // nb, t // tt),
        in_specs=[blk(0), blk(1), blk(2), blk(3), pl.BlockSpec((nb, tt, GATE_W), lambda b, j: (b, j, 0)),
                  _const_spec((1, w)), _const_spec((1, w)), _const_spec((1, w)),
                  st((w, w)), st((1, w)), st((1, w))],
        out_specs=[pl.BlockSpec((nb, tt, w), lambda b, j: (b, j, 0)), st((w, w)), st((1, w)), st((1, w))],
        name="mlstm",
        compiler_params=_params(("parallel", "arbitrary"), est),
    )(qkvo, qkvo, qkvo, qkvo, gates, bias_i, bias_f, normg, c0, n0, m0)


def _ssd_chunk(xs, bm, cm, z, dt, a_x, dskip, normg, s_ref, bi, live=None):
    L, w = xs.shape
    dh = w // N_HEADS
    ng = s_ref.shape[1]
    gw = w // ng
    lmask = _head_lane_masks(L, w)
    tril = _tril(L)
    s_old = [s_ref[bi, g] for g in range(ng)]
    sb16 = [s.astype(BF16) for s in s_old]
    cmb = cm.astype(BF16)
    bmb = bm.astype(BF16)
    xdt = (xs * dt).astype(BF16)
    da_p = _bf16_pieces(dt * a_x, 3)
    yield
    cs = _dot_pieces_left(tril.astype(BF16), da_p)
    cb = [_dot_nt(cmb[:, g * gw:(g + 1) * gw], bmb[:, g * gw:(g + 1) * gw]) for g in range(ng)]
    y_in = jnp.concatenate([jnp.dot(cmb[:, g * gw:(g + 1) * gw], sb16[g], preferred_element_type=F32)
                            for g in range(ng)], axis=1)
    yield
    cs_t = cs.T
    cs_last = cs[L - 1:L, :]
    wl = jnp.exp(cs_last - cs) * dt
    wx = (xs * wl).astype(BF16)
    yield
    x_heads = jnp.concatenate([jnp.where(lmask[h], xdt, jnp.zeros_like(xdt)) for h in range(N_HEADS)], axis=0)
    mh = []
    for h in range(N_HEADS):
        dec = jnp.exp(jnp.where(tril, cs[:, h * dh:h * dh + 1] - cs_t[h * dh:h * dh + 1, :], NEG))
        mh.append((cb[h * ng // N_HEADS] * dec).astype(BF16))
        yield
    y = jnp.exp(cs) * y_in + jnp.dot(jnp.concatenate(mh, axis=1), x_heads, preferred_element_type=F32)
    yield
    dec_x = jnp.exp(cs_last)
    s_new = [dec_x[:, g * gw:(g + 1) * gw] * s_old[g] + _dot_tn(bmb[:, g * gw:(g + 1) * gw], wx[:, g * gw:(g + 1) * gw])
             for g in range(ng)]
    yield
    yc = y + dskip * xs
    out = _rmsnorm_rows(yc * (z * jax.nn.sigmoid(z)), normg)
    for g in range(ng):
        s_ref[bi, g] = s_new[g] if live is None else jnp.where(live, s_new[g], s_old[g])
    return out


def _ssd_kernel(xbc_ref, z_ref, gd_ref, cw_ref, cb_ref, dtb_ref, a_ref, dskip_ref, ng_ref, conv0_ref, s0_ref,
                y_ref, conv_ref, s_ref, ext_ref):
    nb, tt, w = z_ref.shape
    hist = CONV_C - 1
    off = SUBLANE

    @pl.when(pl.program_id(1) == 0)
    def _():
        s_ref[...] = s0_ref[...]
        ext_ref[:, off - hist:off, :] = conv0_ref[...]

    cw = cw_ref[...]
    for bi in range(nb):
        ext_ref[bi, off:off + tt, :] = xbc_ref[bi]
        acc = cb_ref[...] + cw[CONV_C - 1:CONV_C, :] * ext_ref[bi, off:off + tt, :]
        for j in range(CONV_C - 1):
            sh = CONV_C - 1 - j
            acc = acc + cw[j:j + 1, :] * ext_ref[bi, off - sh:off - sh + tt, :]
        tail = ext_ref[bi, off + tt - hist:off + tt, :]
        ext_ref[bi, off - hist:off, :] = tail
        conv_ref[bi] = tail
        ext_ref[bi, off:off + tt, :] = acc * jax.nn.sigmoid(acc)

    dtb = dtb_ref[...]
    a_x = -jnp.exp(a_ref[...])
    dskip = dskip_ref[...]
    ng = ng_ref[...]
    ex = _head_expander(w, 0)

    def chunk(bi, ext_rows, rows, n_valid):
        act = _pad_rows(ext_ref[bi, ext_rows, :], SPAT)
        dt = jax.nn.softplus(_split_dot(_pad_rows(gd_ref[bi, rows, :], SPAT), ex, 3) + dtb)
        if n_valid < SPAT:
            dt = jnp.where(_iota((SPAT, w), 0) < n_valid, dt, 0.0)
        return _ssd_chunk(act[:, 0:w], act[:, w:2 * w], act[:, 2 * w:3 * w], _pad_rows(z_ref[bi, rows, :], SPAT),
                          dt, a_x, dskip, ng, s_ref, bi)

    if tt < SPAT:
        gens = [chunk(bi, slice(off, off + tt), slice(None), tt) for bi in range(nb)]
        for bi, y in enumerate(_interleave(gens)):
            y_ref[bi] = y[0:tt].astype(y_ref.dtype)
    else:
        def body(i, carry):
            start = pl.multiple_of(i * SPAT, SPAT)
            ext_rows = pl.ds(pl.multiple_of(start + off, SUBLANE), SPAT)
            gens = [chunk(bi, ext_rows, pl.ds(start, SPAT), SPAT) for bi in range(nb)]
            for bi, y in enumerate(_interleave(gens)):
                y_ref[bi, pl.ds(start, SPAT), :] = y.astype(y_ref.dtype)
            return carry
        lax.fori_loop(0, tt // SPAT, body, 0)


def _ssd(cin, gates, conv_w, conv_b, dtb, a_log_x, dskip, normg, conv0, s0, nb, tt):
    bsz, t, wtot = cin.shape
    w = dskip.shape[1]
    xw = wtot - w
    ng, n_c, gp = s0.shape[1:]
    st = lambda shape: pl.BlockSpec((nb,) + shape, lambda b, j: (b,) + (0,) * len(shape))
    est = nb * (2 * 2 * tt * wtot * 4 + 2 * tt * w * 4 + (tt + SUBLANE) * xw * 4 + 3 * tt * xw * 4 + 40 * SPAT * w * 4)
    return pl.pallas_call(
        _ssd_kernel,
        out_shape=[jax.ShapeDtypeStruct((bsz, t, w), BF16),
                   jax.ShapeDtypeStruct((bsz, CONV_C - 1, xw), F32),
                   jax.ShapeDtypeStruct(s0.shape, F32)],
        grid=(bsz // nb, t // tt),
        in_specs=[pl.BlockSpec((nb, tt, xw), lambda b, j: (b, j, 0)),
                  pl.BlockSpec((nb, tt, w), lambda b, j: (b, j, xw // w)),
                  pl.BlockSpec((nb, tt, LANE), lambda b, j: (b, j, GATE_DT // LANE)),
                  _const_spec((CONV_C, xw)), _const_spec((1, xw)), _const_spec((1, w)), _const_spec((1, w)),
                  _const_spec((1, w)), _const_spec((1, w)),
                  st((CONV_C - 1, xw)), st((ng, n_c, gp))],
        out_specs=[pl.BlockSpec((nb, tt, w), lambda b, j: (b, j, 0)), st((CONV_C - 1, xw)), st((ng, n_c, gp))],
        scratch_shapes=[pltpu.VMEM((nb, tt + SUBLANE, xw), F32)],
        name="ssd",
        compiler_params=_params(("parallel", "arbitrary"), est),
    )(cin, cin, gates, conv_w, conv_b, dtb, a_log_x, dskip, normg, conv0, s0)


def _gmlp_kernel(u_ref, v_ref, ng_ref, ws_ref, bs_ref, y_ref, *vn_refs):
    tt, w = u_ref.shape[1], u_ref.shape[2]
    u = jax.nn.gelu(u_ref[0])
    vr = jax.nn.gelu(v_ref[0])
    mu = jnp.mean(vr, axis=-1, keepdims=True)
    xc = vr - mu
    vn = xc * lax.rsqrt(jnp.mean(xc * xc, axis=-1, keepdims=True) + EPS) * ng_ref[...]
    for vn_ref in vn_refs:
        vn_ref[0] = vn
    rows = _iota((N_HEADS * SPAT, SPAT), 0) % SPAT
    wst = jnp.where(rows >= _iota((N_HEADS * SPAT, SPAT), 1), ws_ref[...], 0.0).astype(BF16)
    lmask = _head_lane_masks(SPAT, w)
    bias = bs_ref[...]
    vpad = _pad_rows(vn, -(-tt // SPAT) * SPAT).astype(BF16)
    for i in range(vpad.shape[0] // SPAT):
        fs = jnp.dot(wst, vpad[i * SPAT:(i + 1) * SPAT], preferred_element_type=F32)
        f = bias
        for g in range(N_HEADS):
            f = f + jnp.where(lmask[g], fs[g * SPAT:(g + 1) * SPAT], 0.0)
        n = min(SPAT, tt - i * SPAT)
        y_ref[0, i * SPAT:i * SPAT + n, :] = (u[i * SPAT:i * SPAT + n] * f[0:n]).astype(y_ref.dtype)


def _gmlp(din, normg, w_s, b_x, tt, emit_v):
    bsz, t, w2 = din.shape
    w = w2 // 2
    est = 2 * 4 * tt * w * 4 + 8 * tt * w * 4 + 4 * N_HEADS * SPAT * (SPAT + w) * 4
    out_shape = [jax.ShapeDtypeStruct((bsz, t, w), BF16)]
    if emit_v:
        out_shape.append(jax.ShapeDtypeStruct((bsz, t, w), F32))
    return pl.pallas_call(
        _gmlp_kernel,
        out_shape=out_shape,
        grid=(bsz, t // tt),
        in_specs=[pl.BlockSpec((1, tt, w), lambda b, j: (b, j, 0)),
                  pl.BlockSpec((1, tt, w), lambda b, j: (b, j, 1)),
                  _const_spec((1, w)), _const_spec((N_HEADS * SPAT, SPAT)), _const_spec((SPAT, w))],
        out_specs=[pl.BlockSpec((1, tt, w), lambda b, j: (b, j, 0))] * len(out_shape),
        name="spatial_gate",
        compiler_params=_params(("parallel", "parallel"), est),
    )(din, din, normg, w_s, b_x)


def _post_kernel(x_ref, ya_ref, yb_ref, yc_ref, yd_ref, mod_ref, g_ref, wo_ref, wg_ref, wu_ref, wd_ref,
                 cw_ref, cb_ref, f0_ref, gfin_ref, o_ref, fc_ref, ext_ref, *, final, fchunk):
    bb, tt, d = x_ref.shape
    m = bb * tt
    gw = ya_ref.shape[2]
    f = wg_ref.shape[1]
    hist = CONV_F - 1
    off = SUBLANE
    mod = mod_ref[...]
    mix = None
    for j, y_ref in enumerate((ya_ref, yb_ref, yc_ref, yd_ref)):
        p = jnp.dot(y_ref[...].reshape(m, gw), wo_ref[j * gw:(j + 1) * gw, :], preferred_element_type=F32)
        mix = p if mix is None else mix + p
    x1 = x_ref[...] + mod[:, :, 2 * d:3 * d] * mix.reshape(bb, tt, d)
    h2 = _rmsnorm_rows(x1, g_ref[...]) * (1.0 + mod[:, :, 4 * d:5 * d]) + mod[:, :, 3 * d:4 * d]
    hb = h2.reshape(m, d).astype(BF16)

    @pl.when(pl.program_id(1) == 0)
    def _():
        ext_ref[:, off - hist:off, :] = f0_ref[...]

    down = None
    for c0 in range(0, f, fchunk):
        cols = slice(c0, c0 + fchunk)
        g = jnp.dot(hb, wg_ref[:, cols], preferred_element_type=F32).reshape(bb, tt, fchunk)
        u = jnp.dot(hb, wu_ref[:, cols], preferred_element_type=F32)
        ext_ref[:, off:off + tt, cols] = g
        acc = cb_ref[:, cols] + cw_ref[CONV_F - 1:CONV_F, cols] * g
        for j in range(CONV_F - 1):
            sh = CONV_F - 1 - j
            acc = acc + cw_ref[j:j + 1, cols] * ext_ref[:, off - sh:off - sh + tt, cols]
        act = (acc * jax.nn.sigmoid(acc)).reshape(m, fchunk) * u
        p = jnp.dot(act.astype(BF16), wd_ref[cols, :], preferred_element_type=F32)
        down = p if down is None else down + p
    tail = ext_ref[:, off + tt - hist:off + tt, :]
    ext_ref[:, off - hist:off, :] = tail
    fc_ref[...] = tail
    x2 = x1 + mod[:, :, 5 * d:6 * d] * down.reshape(bb, tt, d)
    if final:
        x2 = _rmsnorm_rows(x2, gfin_ref[...])
    o_ref[...] = x2


def _post(x, ys, mod, g_ffn, w_out, w_gate, w_up, w_down, conv_w, conv_b, f0, g_final, bb, tt, final):
    bsz, t, d = x.shape
    gw = ys[0].shape[2]
    f = w_gate.shape[1]
    m = bb * tt
    tok = lambda wd: pl.BlockSpec((bb, tt, wd), lambda i, j: (i, j, 0))
    fchunk = 2 * LANE
    assert f % fchunk == 0
    est = (2 * 2 * m * d * 4 + 2 * 4 * m * gw * 2 + (d * d + 3 * d * f) * 2 + bb * (tt + SUBLANE) * f * 4
           + 8 * m * fchunk * 4 + 6 * m * d * 4)
    return pl.pallas_call(
        functools.partial(_post_kernel, final=final, fchunk=fchunk),
        out_shape=[jax.ShapeDtypeStruct((bsz, t, d), F32), jax.ShapeDtypeStruct((bsz, CONV_F - 1, f), F32)],
        grid=(bsz // bb, t // tt),
        in_specs=[tok(d), tok(gw), tok(gw), tok(gw), tok(gw),
                  pl.BlockSpec((bb, 1, mod.shape[2]), lambda i, j: (i, 0, 0)),
                  _const_spec((1, 1, d)), _const_spec((d, d)), _const_spec((d, f)), _const_spec((d, f)),
                  _const_spec((f, d)), _const_spec((CONV_F, f)), _const_spec((1, f)),
                  pl.BlockSpec((bb, CONV_F - 1, f), lambda i, j: (i, 0, 0)),
                  _const_spec((1, 1, d))],
        out_specs=[tok(d), pl.BlockSpec((bb, CONV_F - 1, f), lambda i, j: (i, 0, 0))],
        scratch_shapes=[pltpu.VMEM((bb, tt + SUBLANE, f), F32)],
        name="post_ffn",
        compiler_params=_params(("parallel", "arbitrary"), est),
    )(x, *ys, mod, g_ffn.reshape(1, 1, d), w_out, w_gate, w_up, w_down, conv_w, conv_b, f0, g_final.reshape(1, 1, d))


def _mixpost_kernel(qkv_ref, qkvo_ref, cin_ref, din_ref, gate_ref, x_ref, mod_ref, *refs, nt, final, fchunk, n_hist):
    hist_refs = refs[:2 * n_hist]
    (bias_ref, bi_ref, bf_ref, bng_ref, ccw_ref, ccb_ref, dtb_ref, alog_ref, dskip_ref, cng_ref,
     dng_ref, dws_ref, dbx_ref, gffn_ref, wo_ref, wg_ref, wu_ref, wd_ref, fcw_ref, fcb_ref, gfin_ref,
     o_ref, c_ref, n_ref, m_ref, conv_ref, s_ref, fc_ref,
     kk_ref, vv_ref, ybuf_ref, cext_ref, fext_ref, fhist_ref) = refs[2 * n_hist:]
    tt, d = x_ref.shape[1], x_ref.shape[2]
    w = d // 4
    f = wg_ref.shape[1]
    band = (N_BAND + 1) * CHUNK
    off = SUBLANE
    s = pl.program_id(0)
    n_tiles = pl.num_programs(0) - 1
    live = s < n_tiles
    mj = jnp.minimum(s, n_tiles - 1) % nt
    pj = jnp.maximum(s - 1, 0) % nt
    wslot = s % 2
    rslot = 1 - wslot

    @pl.when(s == 0)
    def _():
        ybuf_ref[...] = jnp.zeros(ybuf_ref.shape, ybuf_ref.dtype)

    @pl.when(mj == 0)
    def _():
        c_ref[...] = jnp.zeros(c_ref.shape, F32)
        n_ref[...] = jnp.zeros(n_ref.shape, F32)
        m_ref[...] = jnp.zeros(m_ref.shape, F32)
        s_ref[...] = jnp.zeros(s_ref.shape, F32)
        cext_ref[0:off, :] = jnp.zeros((off, cext_ref.shape[1]), F32)

    @pl.when(pj == 0)
    def _():
        fhist_ref[...] = jnp.zeros(fhist_ref.shape, F32)

    ex = _head_expander(w, 0)

    def attention():
        scale = (w // N_HEADS) ** -0.5
        for i, (kh_ref, vh_ref) in enumerate(zip(hist_refs[0::2], hist_refs[1::2])):
            kk_ref[i * tt:(i + 1) * tt, :] = kh_ref[0].astype(BF16)
            vv_ref[i * tt:(i + 1) * tt, :] = vh_ref[0].astype(BF16)
        kk_ref[A_WIN:A_WIN + tt, :] = qkv_ref[0, :, w:2 * w].astype(BF16)
        vv_ref[A_WIN:A_WIN + tt, :] = qkv_ref[0, :, 2 * w:3 * w].astype(BF16)
        masks = _head_lane_masks(CHUNK, w)
        bias = bias_ref[...]
        slot = _iota((1, band), 1)
        q_all = qkv_ref[0, :, 0:w].astype(F32) * scale
        yield
        for i in range(tt // CHUNK):
            base = i * CHUNK
            pos = slot + base
            kvalid = pos >= A_WIN
            for back in range(1, A_WIN // tt + 1):
                kvalid = kvalid | ((pos >= A_WIN - back * tt) & (mj >= back))
            out = yield from _attend_chunk(q_all[base:base + CHUNK], kk_ref[base:base + band, :],
                                           vv_ref[base:base + band, :], bias, kvalid, masks, masks)
            ybuf_ref[wslot, 0, base:base + CHUNK, :] = out.astype(BF16)
            yield

    def mlstm():
        kscale = (w // N_HEADS) ** -0.5
        bias_i, bias_f, ng = bi_ref[...], bf_ref[...], bng_ref[...]
        for c in range(tt // SPAT):
            rows = slice(c * SPAT, (c + 1) * SPAT)
            col = lambda c: qkvo_ref[0, rows, c * w:(c + 1) * w].astype(F32)
            gi_p = _bf16_pieces(gate_ref[0, rows, GATE_I:GATE_I + LANE], GATE_PIECES)
            gf_p = _bf16_pieces(gate_ref[0, rows, GATE_F:GATE_F + LANE], GATE_PIECES)
            yield
            y = yield from _mlstm_chunk(
                col(0), col(1) * kscale, col(2), col(3),
                _dot_pieces(gi_p, ex) + bias_i, _dot_pieces(gf_p, ex) + bias_f,
                ng, c_ref, n_ref, m_ref, 0, SPAT, live)
            ybuf_ref[wslot, 1, rows, :] = y.astype(BF16)
            yield

    def ssd():
        xw = cext_ref.shape[1]
        hist = CONV_C - 1
        cext_ref[off:off + tt, :] = cin_ref[0, :, 0:xw].astype(F32)
        acc = ccb_ref[...] + ccw_ref[CONV_C - 1:CONV_C, :] * cext_ref[off:off + tt, :]
        for j in range(CONV_C - 1):
            sh = CONV_C - 1 - j
            acc = acc + ccw_ref[j:j + 1, :] * cext_ref[off - sh:off - sh + tt, :]
            yield
        tail = cext_ref[off + tt - hist:off + tt, :]
        cext_ref[off - hist:off, :] = tail
        conv_ref[0] = tail
        cext_ref[off:off + tt, :] = acc * jax.nn.sigmoid(acc)
        yield
        a_x = -jnp.exp(alog_ref[...])
        dtb, dskip, ng = dtb_ref[...], dskip_ref[...], cng_ref[...]
        for c in range(tt // SPAT):
            rows = slice(c * SPAT, (c + 1) * SPAT)
            erows = slice(off + c * SPAT, off + (c + 1) * SPAT)
            dt_p = _bf16_pieces(gate_ref[0, rows, GATE_DT:GATE_DT + LANE], GATE_PIECES)
            yield
            dt = jax.nn.softplus(_dot_pieces(dt_p, ex) + dtb)
            y = yield from _ssd_chunk(cext_ref[erows, 0:w], cext_ref[erows, w:2 * w], cext_ref[erows, 2 * w:3 * w],
                                      cin_ref[0, rows, xw:xw + w].astype(F32), dt, a_x, dskip, ng, s_ref, 0, live)
            ybuf_ref[wslot, 2, rows, :] = y.astype(BF16)
            yield

    def gmlp():
        u = jax.nn.gelu(din_ref[0, :, 0:w].astype(F32))
        vr = jax.nn.gelu(din_ref[0, :, w:2 * w].astype(F32))
        yield
        xc = vr - jnp.mean(vr, axis=-1, keepdims=True)
        vn = (xc * lax.rsqrt(jnp.mean(xc * xc, axis=-1, keepdims=True) + EPS) * dng_ref[...]).astype(BF16)
        rows = _iota((N_HEADS * SPAT, SPAT), 0) % SPAT
        wst = jnp.where(rows >= _iota((N_HEADS * SPAT, SPAT), 1), dws_ref[...], 0.0).astype(BF16)
        lmask = _head_lane_masks(SPAT, w)
        yield
        for i in range(tt // SPAT):
            fs = jnp.dot(wst, vn[i * SPAT:(i + 1) * SPAT], preferred_element_type=F32)
            yield
            fgate = dbx_ref[...]
            for g in range(N_HEADS):
                fgate = fgate + jnp.where(lmask[g], fs[g * SPAT:(g + 1) * SPAT], 0.0)
            ybuf_ref[wslot, 3, i * SPAT:(i + 1) * SPAT, :] = (u[i * SPAT:(i + 1) * SPAT] * fgate).astype(BF16)
            yield

    def post():
        hist = CONV_F - 1
        mod = mod_ref[0]
        mix = None
        for j in range(4):
            p = jnp.dot(ybuf_ref[rslot, j], wo_ref[j * w:(j + 1) * w, :], preferred_element_type=F32)
            mix = p if mix is None else mix + p
            yield
        x1 = x_ref[0] + mod[:, 2 * d:3 * d] * mix
        h2 = _rmsnorm_rows(x1, gffn_ref[...]) * (1.0 + mod[:, 4 * d:5 * d]) + mod[:, 3 * d:4 * d]
        hb = h2.astype(BF16)
        yield
        down = None
        act_b16, act_cols = None, None
        for ci, c0 in enumerate(range(0, f, fchunk)):
            cols = slice(c0, c0 + fchunk)
            g = jnp.dot(hb, wg_ref[:, cols], preferred_element_type=F32)
            u = jnp.dot(hb, wu_ref[:, cols], preferred_element_type=F32)
            if act_b16 is not None:
                p = jnp.dot(act_b16, wd_ref[act_cols, :], preferred_element_type=F32)
                down = p if down is None else down + p
            yield
            buf = ci % 2
            fext_ref[buf, off - hist:off, :] = fhist_ref[0, off - hist:off, cols]
            fext_ref[buf, off:off + tt, :] = g
            acc = fcb_ref[:, cols] + fcw_ref[CONV_F - 1:CONV_F, cols] * g
            for j in range(CONV_F - 1):
                sh = CONV_F - 1 - j
                acc = acc + fcw_ref[j:j + 1, cols] * fext_ref[buf, off - sh:off - sh + tt, :]
            fhist_ref[0, off - hist:off, cols] = g[tt - hist:tt]
            act_b16, act_cols = ((acc * jax.nn.sigmoid(acc)) * u).astype(BF16), cols
            yield
        down = down + jnp.dot(act_b16, wd_ref[act_cols, :], preferred_element_type=F32)
        fc_ref[...] = fhist_ref[:, off - hist:off, :]
        x2 = x1 + mod[:, 5 * d:6 * d] * down
        if final:
            x2 = _rmsnorm_rows(x2, gfin_ref[...])
        o_ref[0] = x2

    _interleave([post(), mlstm(), ssd(), attention(), gmlp()])


def _mix_post(qkv_a, qkvo_b, c_in, d_in, gates, x, mod, p, g_final, tt, final):
    bsz, t, d = x.shape
    w = d // 4
    nt = t // tt
    n_tiles = bsz * nt
    f = p["f_w_gate"].shape[1]
    xw = c_in.shape[2] - w
    ng, n_c, gp = 2, LANE, w // 2
    band = (N_BAND + 1) * CHUNK
    fchunk = 2 * LANE
    assert f % fchunk == 0 and tt % SPAT == 0 and t % tt == 0
    mix_tile = lambda width: pl.BlockSpec(
        (1, tt, width), lambda s: (jnp.minimum(s, n_tiles - 1) // nt, jnp.minimum(s, n_tiles - 1) % nt, 0))
    post_tile = lambda width: pl.BlockSpec(
        (1, tt, width), lambda s: (jnp.maximum(s - 1, 0) // nt, jnp.maximum(s - 1, 0) % nt, 0))
    mix_state = lambda shape: pl.BlockSpec(
        (1,) + shape, lambda s: (jnp.minimum(s, n_tiles - 1) // nt,) + (0,) * len(shape))
    post_state = lambda shape: pl.BlockSpec(
        (1,) + shape, lambda s: (jnp.maximum(s - 1, 0) // nt,) + (0,) * len(shape))
    consts = [p["bias_tab"].reshape(N_HEADS * CHUNK, band), p["bi"], p["bf"], p["b_norm_g"], p["c_conv_w"],
              p["c_conv_b"], p["dtb"], p["a_log_x"], p["dskip"], p["c_norm_g"], p["d_norm_g"], p["d_w_s"], p["d_b_x"],
              p["g_ffn"].reshape(1, d), p["w_out"], p["f_w_gate"], p["f_w_up"], p["f_w_down"], p["f_conv_w"],
              p["f_conv_b"], g_final.reshape(1, d)]
    in_widths = (qkv_a.shape[2], qkvo_b.shape[2], c_in.shape[2], d_in.shape[2], gates.shape[2])
    n_hist = A_WIN // tt
    assert n_hist * tt == A_WIN

    def hist_tile(back, col):
        def index(s):
            m = jnp.minimum(s, n_tiles - 1)
            return (m // nt, jnp.maximum(m % nt - back, 0), col)
        return pl.BlockSpec((1, tt, w), index)

    hist_specs = [hist_tile(back, col) for back in range(n_hist, 0, -1) for col in (1, 2)]
    in_bytes = sum(a.shape[2] * a.dtype.itemsize for a in (qkv_a, qkvo_b, c_in, d_in, gates))
    est = ((d * d + 3 * d * f) * 2 + 2 * tt * in_bytes + 4 * tt * d * 4 + (tt + SUBLANE) * (2 * fchunk + xw) * 4
           + 2 * (A_WIN + tt) * w * 2 + 8 * tt * w * 2 + 12 * tt * d * 4 + 6 * w * w * 4
           + 4 * n_hist * tt * w * qkv_a.dtype.itemsize)
    return pl.pallas_call(
        functools.partial(_mixpost_kernel, nt=nt, final=final, fchunk=fchunk, n_hist=n_hist),
        out_shape=[jax.ShapeDtypeStruct((bsz, t, d), F32),
                   jax.ShapeDtypeStruct((bsz, w, w), F32),
                   jax.ShapeDtypeStruct((bsz, 1, w), F32),
                   jax.ShapeDtypeStruct((bsz, 1, w), F32),
                   jax.ShapeDtypeStruct((bsz, CONV_C - 1, xw), F32),
                   jax.ShapeDtypeStruct((bsz, ng, n_c, gp), F32),
                   jax.ShapeDtypeStruct((bsz, CONV_F - 1, f), F32)],
        grid=(n_tiles + 1,),
        in_specs=[mix_tile(wd) for wd in in_widths] + [post_tile(d), post_state((1, mod.shape[2]))] + hist_specs
                 + [_const_spec(c.shape) for c in consts],
        out_specs=[post_tile(d), mix_state((w, w)), mix_state((1, w)), mix_state((1, w)),
                   mix_state((CONV_C - 1, xw)), mix_state((ng, n_c, gp)), post_state((CONV_F - 1, f))],
        scratch_shapes=[pltpu.VMEM((A_WIN + tt, w), BF16), pltpu.VMEM((A_WIN + tt, w), BF16),
                        pltpu.VMEM((2, 4, tt, w), BF16), pltpu.VMEM((tt + SUBLANE, xw), F32),
                        pltpu.VMEM((2, tt + SUBLANE, fchunk), F32), pltpu.VMEM((1, SUBLANE, f), F32)],
        name="mix_post",
        compiler_params=_params(("arbitrary",), est),
    )(qkv_a, qkvo_b, c_in, d_in, gates, x, mod, *([qkv_a] * len(hist_specs)), *consts)


def _layer(x, mod, st, p, tiles, final, g_final):
    bsz, t, d = x.shape
    gw = d // 4
    dh = gw // N_HEADS
    bb, tt, tmix, nb = tiles
    mod3 = mod.reshape(bsz, 1, mod.shape[1])
    dtypes = (BF16,) * 4 + (F32,) if st is None else (F32,) * 5
    qkv_a, qkvo_b, c_in, d_in, gates = _in_proj(x, mod3, p["g_mix"], p["w_in"], p["widths"], dtypes, bb, tt)

    if st is None:
        x_new, c_new, n_new, m_new, conv_new, ssm_new, fconv_new = _mix_post(
            qkv_a, qkvo_b, c_in, d_in, gates, x, mod3, p, g_final, tmix, final)
        vn = []
    else:
        y_a = _attention(qkv_a, st["a_k"], st["a_v"], p["bias_tab"], tmix)
        y_b, c_new, n_new, m_new = _mlstm(qkvo_b, gates, p["bi"], p["bf"], p["b_norm_g"], st["b_c"], st["b_n"],
                                          st["b_m"], nb, tmix)
        y_c, conv_new, ssm_new = _ssd(c_in, gates, p["c_conv_w"], p["c_conv_b"], p["dtb"], p["a_log_x"], p["dskip"],
                                      p["c_norm_g"], st["c_conv"], st["c_ssm"], nb, tmix)
        y_d, *vn = _gmlp(d_in, p["d_norm_g"], p["d_w_s"], p["d_b_x"], tmix, True)
        x_new, fconv_new = _post(x, (y_a, y_b, y_c, y_d), mod3, p["g_ffn"], p["w_out"], p["f_w_gate"], p["f_w_up"],
                                 p["f_w_down"], p["f_conv_w"], p["f_conv_b"], st["f_conv"], g_final, bb, tt, final)

    keep = min(A_WIN, t)
    new_k = qkv_a[:, t - keep:, gw:2 * gw].reshape(bsz, keep, N_HEADS, dh).astype(F32)
    new_v = qkv_a[:, t - keep:, 2 * gw:3 * gw].reshape(bsz, keep, N_HEADS, dh).astype(F32)
    idx = jnp.arange(N_HEADS)
    c_heads = c_new.reshape(bsz, N_HEADS, dh, N_HEADS, dh)[:, idx, :, idx, :]
    c_heads = jnp.moveaxis(c_heads, 0, 1)
    n_heads = n_new.reshape(bsz, N_HEADS, dh)
    m_heads = m_new[:, 0, ::dh]
    ssm = jnp.swapaxes(ssm_new, 2, 3).reshape(bsz, N_HEADS, gw // N_HEADS, ssm_new.shape[2])
    outs = (new_k, new_v, c_heads, n_heads, m_heads, ssm, conv_new, fconv_new, *vn)
    return x_new, outs


def _prep_layer(l, w_in, w_out, a_rel_bias, b_i_bias, b_f_bias, b_norm_g, c_conv_w, c_conv_b, c_dt_bias, c_a_log,
                c_d_skip, c_norm_g, d_norm_g, d_w_s, d_b_s, f_w_gate, f_w_up, f_conv_w, f_conv_b, f_w_down,
                g_norm_mix, g_norm_ffn):
    d = w_in.shape[1]
    gw = d // 4
    nh = b_i_bias.shape[1]
    xbc_w = c_conv_w.shape[2]
    sizes = (gw,) * 7 + (nh, nh, gw, xbc_w, c_dt_bias.shape[1], gw, gw)
    offs = [0]
    for s in sizes:
        offs.append(offs[-1] + s)
    moves, dst = [], 0
    for i in (0, 1, 2, 3, 4, 5, 6, 10, 9, 12, 13):
        moves.append((offs[i], dst, sizes[i]))
        dst += sizes[i]
    assert sizes[7] == sizes[8] == sizes[11] == N_HEADS
    packed = (dst, ((offs[7], GATE_I, N_HEADS), (offs[8], GATE_F, N_HEADS), (offs[11], GATE_DT, N_HEADS)))
    widths = (3 * gw, 4 * gw, xbc_w + gw, 2 * gw, GATE_W)
    dst += GATE_W
    assert dst == sum(widths)
    heads_x = lambda v: jnp.repeat(v.astype(F32), gw // nh)[None, :]
    return dict(
        w_in=_regroup_cast(w_in, l, moves, packed, dst), widths=widths, g_mix=g_norm_mix[l], g_ffn=g_norm_ffn[l],
        w_out=w_out[l].astype(BF16), bias_tab=_bias_table(a_rel_bias[l]),
        bi=heads_x(b_i_bias[l]), bf=heads_x(b_f_bias[l]), b_norm_g=b_norm_g[l][None, :],
        c_conv_w=c_conv_w[l], c_conv_b=c_conv_b[l][None, :], dtb=heads_x(c_dt_bias[l]),
        a_log_x=heads_x(c_a_log[l]), dskip=heads_x(c_d_skip[l]), c_norm_g=c_norm_g[l][None, :],
        d_norm_g=d_norm_g[l][None, :], d_w_s=d_w_s[l].reshape(-1, d_w_s.shape[-1]),
        d_b_x=jnp.repeat(d_b_s[l].T, gw // d_b_s.shape[1], axis=1),
        f_w_gate=f_w_gate[l].astype(BF16), f_w_up=f_w_up[l].astype(BF16), f_w_down=f_w_down[l].astype(BF16),
        f_conv_w=f_conv_w[l], f_conv_b=f_conv_b[l][None, :],
    )


def _blockdiag_state(c):
    bsz, nh, dh, _ = c.shape
    eye = jnp.eye(nh, dtype=c.dtype)
    return jnp.einsum("bhde,hg->bhdge", c, eye).reshape(bsz, nh * dh, nh * dh)


def kernel(x_prompt, x_sample, c_prompt, c_sample, cache_a_k, cache_a_v, state_b_c, state_b_n, state_b_m, state_c_ssm, state_c_conv, state_ffn_conv, w_ada, b_ada, g_norm_mix, g_norm_ffn, w_in, w_out, a_rel_bias, b_i_bias, b_f_bias, b_norm_g, c_conv_w, c_conv_b, c_dt_bias, c_a_log, c_d_skip, c_norm_g, d_norm_g, d_w_s, d_b_s, f_w_gate, f_w_up, f_conv_w, f_conv_b, f_w_down, g_final):
    depth = w_in.shape[0]
    bp, tp, d = x_prompt.shape
    bs, ts, _ = x_sample.shape
    gw = d // 4
    dh = gw // N_HEADS
    xbc_w = c_conv_w.shape[2]
    n_c = state_c_ssm.shape[-1]
    g_c = (xbc_w - gw) // 2 // n_c

    mod_all = _ada(jnp.concatenate([c_prompt, c_sample], axis=0), w_ada, b_ada)

    nb = max(n for n in (4, 2, 1) if bs % n == 0)
    tiles_p = (1, min(A_WIN, tp), min(A_WIN, tp), 1)
    tiles_s = (bs, ts, ts, nb)

    xp, xs = x_prompt, x_sample
    p_states, s_states = [], []
    for l in range(depth):
        p = _prep_layer(l, w_in, w_out, a_rel_bias, b_i_bias, b_f_bias, b_norm_g, c_conv_w, c_conv_b, c_dt_bias,
                        c_a_log, c_d_skip, c_norm_g, d_norm_g, d_w_s, d_b_s, f_w_gate, f_w_up, f_conv_w, f_conv_b,
                        f_w_down, g_norm_mix, g_norm_ffn)
        final = l == depth - 1
        xp, sp = _layer(xp, mod_all[l, :bp], None, p, tiles_p, final, g_final)
        st_s = dict(a_k=cache_a_k[l].reshape(bs, -1, gw), a_v=cache_a_v[l].reshape(bs, -1, gw),
                    b_c=_blockdiag_state(state_b_c[l]), b_n=state_b_n[l].reshape(bs, 1, gw),
                    b_m=jnp.repeat(state_b_m[l], dh, axis=1)[:, None, :],
                    c_conv=state_c_conv[l],
                    c_ssm=jnp.swapaxes(state_c_ssm[l].reshape(bs, g_c, gw // g_c, n_c), 2, 3),
                    f_conv=state_ffn_conv[l])
        xs, ss = _layer(xs, mod_all[l, bp:], st_s, p, tiles_s, final, g_final)
        p_states.append(sp)
        s_states.append(ss)

    stack = lambda states, i: jnp.stack([s[i] for s in states])
    return (xp, xs,
            *(stack(p_states, i) for i in range(8)),
            *(stack(s_states, i) for i in range(9)))
```

```python
import functools

import jax
import jax.numpy as jnp
from jax import lax
from jax.experimental import pallas as pl
from jax.experimental.pallas import tpu as pltpu

F32 = jnp.float32
BF16 = jnp.bfloat16

EPS = 1e-6
NEG = -1e30

CHUNK = 64
N_BAND = 8
A_WIN = N_BAND * CHUNK
REL_CLIP = 128
N_HEADS = 4
SPAT = 128
CONV_C = 4
CONV_F = 3
LANE = 128
GATE_I, GATE_F, GATE_DT = 0, 4, 8
GATE_W = LANE
GATE_PIECES = 2
SUBLANE = 8
VMEM_CAP = 64 * 1024 * 1024


def _vmem_limit(nbytes):
    return int(min(max(nbytes, 16 * 1024 * 1024), VMEM_CAP - 8 * 1024 * 1024))


def _params(sem, nbytes):
    return pltpu.CompilerParams(dimension_semantics=sem, vmem_limit_bytes=_vmem_limit(nbytes))


def _const_spec(shape):
    nd = len(shape)
    return pl.BlockSpec(shape, lambda *_: (0,) * nd, pipeline_mode=pl.Buffered(1))


def _iota(shape, dim):
    return lax.broadcasted_iota(jnp.int32, shape, dim)


def _bf16_pieces(x, parts):
    out = []
    r = x
    for i in range(parts):
        hi = r.astype(BF16)
        out.append(hi)
        if i + 1 < parts:
            r = r - hi.astype(F32)
    return out


def _dot_pieces(pieces, e):
    acc = None
    for piece in pieces:
        d = jnp.dot(piece, e, preferred_element_type=F32)
        acc = d if acc is None else acc + d
    return acc


def _dot_pieces_left(e, pieces):
    acc = None
    for piece in pieces:
        d = jnp.dot(e, piece, preferred_element_type=F32)
        acc = d if acc is None else acc + d
    return acc


def _split_dot(x, e, parts):
    return _dot_pieces(_bf16_pieces(x, parts), e)


def _dot_nt(a, b):
    return lax.dot_general(a, b, (((1,), (1,)), ((), ())), preferred_element_type=F32)


def _dot_tn(a, b):
    return lax.dot_general(a, b, (((0,), (0,)), ((), ())), preferred_element_type=F32)


def _head_expander(width, first):
    dh = width // N_HEADS
    return (_iota((LANE, width), 1) // dh == _iota((LANE, width), 0) - first).astype(BF16)


def _head_lane_masks(rows, width):
    dh = width // N_HEADS
    lane = _iota((rows, width), 1)
    return [(lane >= h * dh) & (lane < (h + 1) * dh) for h in range(N_HEADS)]


def _tril(n):
    return _iota((n, n), 0) >= _iota((n, n), 1)


def _pad_rows(x, rows):
    if x.shape[0] == rows:
        return x
    return jnp.concatenate([x, jnp.zeros((rows - x.shape[0], x.shape[1]), x.dtype)], axis=0)


def _interleave(gens):
    results = [None] * len(gens)
    live = list(range(len(gens)))
    while live:
        for i in list(live):
            try:
                next(gens[i])
            except StopIteration as stop:
                results[i] = stop.value
                live.remove(i)
    return results


def _rmsnorm_rows(x, g):
    return x * lax.rsqrt(jnp.mean(x * x, axis=-1, keepdims=True) + EPS) * g


def _regroup_kernel(w_ref, o_ref, *, moves, packed):
    w = w_ref[0]
    rows = w.shape[0]
    for src, dst, n in moves:
        o_ref[:, dst:dst + n] = w[:, src:src + n].astype(o_ref.dtype)
    dst, pieces = packed
    parts, lane = [], 0
    for src, first, n in pieces:
        if first > lane:
            parts.append(jnp.zeros((rows, first - lane), F32))
        parts.append(w[:, src:src + n])
        lane = first + n
    parts.append(jnp.zeros((rows, GATE_W - lane), F32))
    o_ref[:, dst:dst + GATE_W] = jnp.concatenate(parts, axis=1).astype(o_ref.dtype)


def _regroup_cast(w_all, layer, moves, packed, ncols, row_block=256):
    _, r, c = w_all.shape
    return pl.pallas_call(
        functools.partial(_regroup_kernel, moves=tuple(moves), packed=packed),
        out_shape=jax.ShapeDtypeStruct((r, ncols), BF16),
        grid=(r // row_block,),
        in_specs=[pl.BlockSpec((1, row_block, c), lambda i: (layer, i, 0))],
        out_specs=pl.BlockSpec((row_block, ncols), lambda i: (i, 0)),
        name="regroup_cast",
        compiler_params=_params(("parallel",), 2 * row_block * (c * 4 + ncols * 2) + row_block * c * 4),
    )(w_all)


def _ada_kernel(c_ref, w_ref, b_ref, o_ref):
    c = c_ref[...]
    h = (c * jax.nn.sigmoid(c)).astype(BF16)
    o_ref[0] = jnp.dot(h, w_ref[0].astype(BF16), preferred_element_type=F32) + b_ref[0]


def _ada(c_all, w_ada, b_ada):
    depth, d, n6 = w_ada.shape
    r = c_all.shape[0]
    tn = d
    return pl.pallas_call(
        _ada_kernel,
        out_shape=jax.ShapeDtypeStruct((depth, r, n6), F32),
        grid=(depth, n6 // tn),
        in_specs=[pl.BlockSpec((r, d), lambda l, j: (0, 0)),
                  pl.BlockSpec((1, d, tn), lambda l, j: (l, 0, j)),
                  pl.BlockSpec((1, 1, tn), lambda l, j: (l, 0, j))],
        out_specs=pl.BlockSpec((1, r, tn), lambda l, j: (l, 0, j)),
        name="ada_mod",
        compiler_params=_params(("parallel", "parallel"), 4 * (2 * d * tn * 4 + 2 * r * tn * 4 + r * d * 4)),
    )(c_all, w_ada, b_ada.reshape(depth, 1, n6))


def _bias_kernel(rb_ref, o_ref, *, lo, hi):
    nh, lq, lk = o_ref.shape
    idx = jnp.clip(A_WIN + _iota((lq, lk), 0) - _iota((lq, lk), 1), -REL_CLIP, REL_CLIP) + REL_CLIP
    for h in range(nh):
        def body(r, acc, h=h):
            return jnp.where(idx == r, rb_ref[h, r], acc)
        o_ref[h] = lax.fori_loop(lo, hi + 1, body, jnp.zeros((lq, lk), F32))


def _bias_table(rel_bias):
    nh = rel_bias.shape[0]
    band = (N_BAND + 1) * CHUNK
    lo = max(A_WIN - (band - 1), -REL_CLIP) + REL_CLIP
    hi = min(A_WIN + CHUNK - 1, REL_CLIP) + REL_CLIP
    return pl.pallas_call(
        functools.partial(_bias_kernel, lo=lo, hi=hi),
        out_shape=jax.ShapeDtypeStruct((nh, CHUNK, band), F32),
        in_specs=[pl.BlockSpec(memory_space=pltpu.SMEM)],
        out_specs=pl.BlockSpec(memory_space=pltpu.VMEM),
        name="rel_bias_table",
    )(rel_bias)


def _in_kernel(x_ref, mod_ref, g_ref, w_ref, *o_refs, col_starts):
    bb, tt, d = x_ref.shape
    x = x_ref[...]
    mod = mod_ref[...]
    h = _rmsnorm_rows(x, g_ref[...]) * (1.0 + mod[:, :, d:2 * d]) + mod[:, :, 0:d]
    hb = h.reshape(bb * tt, d).astype(BF16)
    for o_ref, (a, b) in zip(o_refs, col_starts):
        o_ref[...] = jnp.dot(hb, w_ref[:, a:b], preferred_element_type=F32).reshape(bb, tt, b - a).astype(o_ref.dtype)


def _in_proj(x, mod, g, w, widths, dtypes, bb, tt):
    bsz, t, d = x.shape
    ncols = w.shape[1]
    starts, a = [], 0
    for wd in widths:
        starts.append((a, a + wd))
        a += wd
    m = bb * tt
    est = 2 * m * d * 4 + 2 * d * ncols * 2 + 2 * m * ncols * 4 + 3 * m * d * 4 + m * max(widths) * 4
    return pl.pallas_call(
        functools.partial(_in_kernel, col_starts=tuple(starts)),
        out_shape=[jax.ShapeDtypeStruct((bsz, t, wd), dt) for wd, dt in zip(widths, dtypes)],
        grid=(bsz // bb, t // tt),
        in_specs=[pl.BlockSpec((bb, tt, d), lambda i, j: (i, j, 0)),
                  pl.BlockSpec((bb, 1, mod.shape[2]), lambda i, j: (i, 0, 0)),
                  _const_spec((1, 1, d)),
                  _const_spec((d, ncols))],
        out_specs=[pl.BlockSpec((bb, tt, wd), lambda i, j: (i, j, 0)) for wd in widths],
        name="in_proj",
        compiler_params=_params(("parallel", "parallel"), est),
    )(x, mod, g.reshape(1, 1, d), w)


def _attend_chunk(qc, kb, vb, bias, kvalid, masks_q, masks_o):
    lq = qc.shape[0]
    qs = jnp.concatenate([jnp.where(mk, qc, 0.0) for mk in masks_q], axis=0).astype(BF16)
    yield
    s = _dot_nt(qs, kb) + bias
    yield
    if kvalid is not None:
        s = jnp.where(kvalid, s, NEG)
    e = jnp.exp(s - jnp.max(s, axis=-1, keepdims=True))
    eb = e.astype(BF16)
    rinv = 1.0 / jnp.sum(e, axis=-1, keepdims=True)
    yield
    o = jnp.dot(eb, vb, preferred_element_type=F32)
    yield
    o = o * rinv
    out = jnp.where(masks_o[0], o[0:lq], 0.0)
    for h in range(1, N_HEADS):
        out = out + jnp.where(masks_o[h], o[h * lq:(h + 1) * lq], 0.0)
    return out


def _attn_kernel(q_ref, k_ref, v_ref, kh_ref, vh_ref, bias_ref, y_ref, kk_ref, vv_ref, *, hist_is_cache, t_valid):
    tq, w = q_ref.shape[1], q_ref.shape[2]
    tpad = kk_ref.shape[0] - A_WIN
    scale = (w // N_HEADS) ** -0.5
    kk_ref[0:A_WIN, :] = kh_ref[0].astype(BF16)
    vv_ref[0:A_WIN, :] = vh_ref[0].astype(BF16)
    kk_ref[A_WIN:A_WIN + tpad, :] = _pad_rows(k_ref[0], tpad).astype(BF16)
    vv_ref[A_WIN:A_WIN + tpad, :] = _pad_rows(v_ref[0], tpad).astype(BF16)
    masks = _head_lane_masks(CHUNK, w)
    bias = bias_ref[...]
    band = (N_BAND + 1) * CHUNK
    slot = _iota((1, band), 1)
    hist_ok = jnp.logical_or(pl.program_id(1) > 0, hist_is_cache)
    q_all = _pad_rows(q_ref[0], tpad) * scale
    gens = []
    for i in range(tpad // CHUNK):
        base = i * CHUNK
        pos = slot + base
        kvalid = (pos < A_WIN + t_valid) & ((pos >= A_WIN) | hist_ok)
        gens.append(_attend_chunk(q_all[base:base + CHUNK], kk_ref[base:base + band, :], vv_ref[base:base + band, :],
                                  bias, kvalid, masks, masks))
    for i, out in enumerate(_interleave(gens)):
        base = i * CHUNK
        rows = min(CHUNK, tq - base)
        y_ref[0, base:base + rows, :] = out[0:rows].astype(y_ref.dtype)


def _attention(qkv, k_hist, v_hist, bias_tab, tq):
    bsz, t, w3 = qkv.shape
    w = w3 // 3
    hist_is_cache = k_hist is not None
    nt = t // tq
    tpad = -(-tq // CHUNK) * CHUNK
    band = (N_BAND + 1) * CHUNK
    if hist_is_cache:
        assert nt == 1
        hist_specs = [pl.BlockSpec((1, A_WIN, w), lambda b, j: (b, 0, 0))] * 2
        hist_args = (k_hist, v_hist)
    else:
        assert tq == A_WIN
        hist_specs = [pl.BlockSpec((1, tq, w), lambda b, j: (b, jnp.maximum(j - 1, 0), 1)),
                      pl.BlockSpec((1, tq, w), lambda b, j: (b, jnp.maximum(j - 1, 0), 2))]
        hist_args = (qkv, qkv)
    est = 2 * 5 * tpad * w * 4 + 2 * tq * w * 4 + 2 * (A_WIN + tpad) * w * 2 + 12 * N_HEADS * CHUNK * band * 4
    return pl.pallas_call(
        functools.partial(_attn_kernel, hist_is_cache=hist_is_cache, t_valid=tq if nt == 1 else tpad),
        out_shape=jax.ShapeDtypeStruct((bsz, t, w), BF16),
        grid=(bsz, nt),
        in_specs=[pl.BlockSpec((1, tq, w), lambda b, j: (b, j, 0)),
                  pl.BlockSpec((1, tq, w), lambda b, j: (b, j, 1)),
                  pl.BlockSpec((1, tq, w), lambda b, j: (b, j, 2)),
                  *hist_specs,
                  _const_spec((N_HEADS * CHUNK, band))],
        out_specs=pl.BlockSpec((1, tq, w), lambda b, j: (b, j, 0)),
        scratch_shapes=[pltpu.VMEM((A_WIN + tpad, w), BF16), pltpu.VMEM((A_WIN + tpad, w), BF16)],
        name="band_attention",
        compiler_params=_params(("parallel", "parallel"), est),
    )(qkv, qkv, qkv, *hist_args, bias_tab.reshape(N_HEADS * CHUNK, band))


def _mlstm_chunk(q, k, v, og, gi, gf, normg, c_ref, n_ref, m_ref, bi, n_valid, live=None):
    L, w = q.shape
    dh = w // N_HEADS
    lmask = _head_lane_masks(L, w)
    tril = _tril(L)
    blockdiag = (_iota((w, w), 0) // dh) == (_iota((w, w), 1) // dh)
    bd = blockdiag.astype(BF16)

    m_prev = m_ref[bi]
    c_old = c_ref[bi]
    n_old = n_ref[bi]
    qb16 = q.astype(BF16)
    kb16 = k.astype(BF16)
    vb16 = v.astype(BF16)
    cb16 = c_old.astype(BF16)
    qmask = [jnp.where(lmask[h], q, 0.0).astype(BF16) for h in range(N_HEADS)]
    lf_p = _bf16_pieces(jax.nn.log_sigmoid(gf), 3)
    qn_p = _bf16_pieces(q * n_old, 2)
    yield
    b = _dot_pieces_left(tril.astype(BF16), lf_p)
    q_c = jnp.dot(qb16, cb16, preferred_element_type=F32)
    q_n = _dot_pieces(qn_p, bd)
    yield
    qk = [_dot_nt(qmask[h], kb16) for h in range(N_HEADS)]
    u = gi - b
    u_t = u.T
    yield
    cm = jnp.zeros((L, w), F32)
    for h in range(N_HEADS):
        cmh = jnp.max(jnp.where(tril, u_t[h * dh:h * dh + 1, :], NEG), axis=1, keepdims=True)
        cm = jnp.where(lmask[h], cmh, cm)
    mx = jnp.maximum(m_prev, cm)
    m_t = b + mx
    inter = jnp.exp(m_prev - mx)
    last = n_valid - 1
    m_last = m_t[last:last + 1, :]
    b_last = b[last:last + 1, :]
    decay = jnp.exp(b_last + m_prev - m_last)
    ws = jnp.exp(u + (b_last - m_last))
    if n_valid < L:
        ws = jnp.where(_iota((L, w), 0) < n_valid, ws, 0.0)
    kw = k * ws
    kwb16 = kw.astype(BF16)
    yield
    v_heads = jnp.concatenate([jnp.where(lmask[h], v, 0.0).astype(BF16) for h in range(N_HEADS)], axis=0)
    rs = jnp.zeros((L, w), F32)
    wq = []
    for h in range(N_HEADS):
        arg = jnp.where(tril, u_t[h * dh:h * dh + 1, :] - mx[:, h * dh:h * dh + 1], NEG)
        wqk = jnp.exp(arg) * qk[h]
        rs = jnp.where(lmask[h], jnp.sum(wqk, axis=1, keepdims=True), rs)
        wq.append(wqk.astype(BF16))
        yield
    num = jnp.dot(jnp.concatenate(wq, axis=1), v_heads, preferred_element_type=F32)
    upd = _dot_tn(kwb16, vb16)
    den = inter * q_n + rs
    hout = (inter * q_c + num) / jnp.maximum(jnp.abs(den), jnp.exp(-m_t))
    mu_p = _bf16_pieces(hout, 2)
    yield
    xc = hout - _dot_pieces(mu_p, bd) * (1.0 / dh)
    var_p = _bf16_pieces(xc * xc, 2)
    yield
    var = _dot_pieces(var_p, bd) * (1.0 / dh)
    yield
    y = jax.nn.sigmoid(og) * (xc * lax.rsqrt(var + EPS) * normg)
    keep = (lambda new, old: new) if live is None else (lambda new, old: jnp.where(live, new, old))
    c_ref[bi] = keep(decay * c_old + jnp.where(blockdiag, upd, 0.0), c_old)
    n_ref[bi] = keep(decay * n_old + jnp.sum(kw, axis=0, keepdims=True), n_old)
    m_ref[bi] = keep(m_last, m_prev)
    return y


def _heads_to_blockdiag(c):
    w, dh = c.shape
    tile = (_iota((dh, w), 1) % dh == _iota((dh, w), 0)).astype(BF16)
    blockdiag = (_iota((w, w), 0) // dh) == (_iota((w, w), 1) // dh)
    return jnp.where(blockdiag, _split_dot(c, tile, 3), 0.0)


def _blockdiag_to_heads(c_bd):
    w = c_bd.shape[0]
    dh = w // N_HEADS
    fold = (_iota((w, dh), 0) % dh == _iota((w, dh), 1)).astype(BF16)
    return _split_dot(c_bd, fold, 3)


def _mlstm_kernel(q_ref, k_ref, v_ref, o_ref, g_ref, bi_ref, bf_ref, ng_ref,
                  c0_ref, n0_ref, m0_ref, y_ref, cout_ref, n_ref, m_ref, c_ref):
    nb, tt, w = q_ref.shape
    kscale = (w // N_HEADS) ** -0.5

    @pl.when(pl.program_id(1) == 0)
    def _():
        for bi in range(nb):
            c_ref[bi] = _heads_to_blockdiag(c0_ref[bi])
        n_ref[...] = n0_ref[...]
        m_ref[...] = m0_ref[...]

    bias_i = bi_ref[...]
    bias_f = bf_ref[...]
    ng = ng_ref[...]
    ex_i, ex_f = _head_expander(w, GATE_I), _head_expander(w, GATE_F)

    def chunk(bi, rows, n_valid):
        pad = lambda r: _pad_rows(r[bi, rows, :], SPAT)
        g = pad(g_ref)
        return _mlstm_chunk(pad(q_ref), pad(k_ref) * kscale, pad(v_ref), pad(o_ref),
                            _split_dot(g, ex_i, 3) + bias_i, _split_dot(g, ex_f, 3) + bias_f,
                            ng, c_ref, n_ref, m_ref, bi, n_valid)

    if tt < SPAT:
        for bi, y in enumerate(_interleave([chunk(bi, slice(None), tt) for bi in range(nb)])):
            y_ref[bi] = y[0:tt].astype(y_ref.dtype)
    else:
        def body(i, carry):
            rows = pl.ds(pl.multiple_of(i * SPAT, SPAT), SPAT)
            for bi, y in enumerate(_interleave([chunk(bi, rows, SPAT) for bi in range(nb)])):
                y_ref[bi, rows, :] = y.astype(y_ref.dtype)
            return carry
        lax.fori_loop(0, tt // SPAT, body, 0)

    @pl.when(pl.program_id(1) == pl.num_programs(1) - 1)
    def _():
        for bi in range(nb):
            cout_ref[bi] = _blockdiag_to_heads(c_ref[bi])


def _mlstm(qkvo, gates, bias_i, bias_f, normg, c0, n0, m0, nb, tt):
    bsz, t, w4 = qkvo.shape
    w = w4 // 4
    dh = w // N_HEADS
    blk = lambda c: pl.BlockSpec((nb, tt, w), lambda b, j: (b, j, c))
    st = lambda shape: pl.BlockSpec((nb,) + shape, lambda b, j: (b, 0, 0))
    est = nb * (2 * 5 * tt * w * 4 + 4 * tt * LANE * 4 + 6 * w * w * 4 + 40 * SPAT * w * 4 + 24 * SPAT * SPAT * 4)
    return pl.pallas_call(
        _mlstm_kernel,
        out_shape=[jax.ShapeDtypeStruct((bsz, t, w), BF16),
                   jax.ShapeDtypeStruct((bsz, w, dh), F32),
                   jax.ShapeDtypeStruct((bsz, 1, w), F32),
                   jax.ShapeDtypeStruct((bsz, 1, w), F32)],
        grid=(bsz // nb, t // tt),
        in_specs=[blk(0), blk(1), blk(2), blk(3), pl.BlockSpec((nb, tt, GATE_W), lambda b, j: (b, j, 0)),
                  _const_spec((1, w)), _const_spec((1, w)), _const_spec((1, w)),
                  st((w, dh)), st((1, w)), st((1, w))],
        out_specs=[pl.BlockSpec((nb, tt, w), lambda b, j: (b, j, 0)), st((w, dh)), st((1, w)), st((1, w))],
        scratch_shapes=[pltpu.VMEM((nb, w, w), F32)],
        name="mlstm",
        compiler_params=_params(("parallel", "arbitrary"), est),
    )(qkvo, qkvo, qkvo, qkvo, gates, bias_i, bias_f, normg, c0, n0, m0)


def _ssd_chunk(xs, bm, cm, z, dt, a_x, dskip, normg, s_ref, bi, live=None):
    L, w = xs.shape
    dh = w // N_HEADS
    ng = s_ref.shape[1]
    gw = w // ng
    lmask = _head_lane_masks(L, w)
    tril = _tril(L)
    s_old = [s_ref[bi, g] for g in range(ng)]
    sb16 = [s.astype(BF16) for s in s_old]
    cmb = cm.astype(BF16)
    bmb = bm.astype(BF16)
    xdt = (xs * dt).astype(BF16)
    da_p = _bf16_pieces(dt * a_x, 3)
    yield
    cs = _dot_pieces_left(tril.astype(BF16), da_p)
    cb = [_dot_nt(cmb[:, g * gw:(g + 1) * gw], bmb[:, g * gw:(g + 1) * gw]) for g in range(ng)]
    y_in = jnp.concatenate([jnp.dot(cmb[:, g * gw:(g + 1) * gw], sb16[g], preferred_element_type=F32)
                            for g in range(ng)], axis=1)
    yield
    cs_t = cs.T
    cs_last = cs[L - 1:L, :]
    wl = jnp.exp(cs_last - cs) * dt
    wx = (xs * wl).astype(BF16)
    yield
    x_heads = jnp.concatenate([jnp.where(lmask[h], xdt, jnp.zeros_like(xdt)) for h in range(N_HEADS)], axis=0)
    mh = []
    for h in range(N_HEADS):
        dec = jnp.exp(jnp.where(tril, cs[:, h * dh:h * dh + 1] - cs_t[h * dh:h * dh + 1, :], NEG))
        mh.append((cb[h * ng // N_HEADS] * dec).astype(BF16))
        yield
    y = jnp.exp(cs) * y_in + jnp.dot(jnp.concatenate(mh, axis=1), x_heads, preferred_element_type=F32)
    yield
    dec_x = jnp.exp(cs_last)
    s_new = [dec_x[:, g * gw:(g + 1) * gw] * s_old[g] + _dot_tn(bmb[:, g * gw:(g + 1) * gw], wx[:, g * gw:(g + 1) * gw])
             for g in range(ng)]
    yield
    yc = y + dskip * xs
    out = _rmsnorm_rows(yc * (z * jax.nn.sigmoid(z)), normg)
    for g in range(ng):
        s_ref[bi, g] = s_new[g] if live is None else jnp.where(live, s_new[g], s_old[g])
    return out


def _ssd_kernel(xbc_ref, z_ref, gd_ref, cw_ref, cb_ref, dtb_ref, a_ref, dskip_ref, ng_ref, conv0_ref, s0_ref,
                y_ref, conv_ref, s_ref, ext_ref):
    nb, tt, w = z_ref.shape
    hist = CONV_C - 1
    off = SUBLANE

    @pl.when(pl.program_id(1) == 0)
    def _():
        s_ref[...] = s0_ref[...]
        ext_ref[:, off - hist:off, :] = conv0_ref[...]

    cw = cw_ref[...]
    for bi in range(nb):
        ext_ref[bi, off:off + tt, :] = xbc_ref[bi]
        acc = cb_ref[...] + cw[CONV_C - 1:CONV_C, :] * ext_ref[bi, off:off + tt, :]
        for j in range(CONV_C - 1):
            sh = CONV_C - 1 - j
            acc = acc + cw[j:j + 1, :] * ext_ref[bi, off - sh:off - sh + tt, :]
        tail = ext_ref[bi, off + tt - hist:off + tt, :]
        ext_ref[bi, off - hist:off, :] = tail
        conv_ref[bi] = tail
        ext_ref[bi, off:off + tt, :] = acc * jax.nn.sigmoid(acc)

    dtb = dtb_ref[...]
    a_x = -jnp.exp(a_ref[...])
    dskip = dskip_ref[...]
    ng = ng_ref[...]
    ex = _head_expander(w, GATE_DT)

    def chunk(bi, ext_rows, rows, n_valid):
        act = _pad_rows(ext_ref[bi, ext_rows, :], SPAT)
        dt = jax.nn.softplus(_split_dot(_pad_rows(gd_ref[bi, rows, :], SPAT), ex, 3) + dtb)
        if n_valid < SPAT:
            dt = jnp.where(_iota((SPAT, w), 0) < n_valid, dt, 0.0)
        return _ssd_chunk(act[:, 0:w], act[:, w:2 * w], act[:, 2 * w:3 * w], _pad_rows(z_ref[bi, rows, :], SPAT),
                          dt, a_x, dskip, ng, s_ref, bi)

    if tt < SPAT:
        gens = [chunk(bi, slice(off, off + tt), slice(None), tt) for bi in range(nb)]
        for bi, y in enumerate(_interleave(gens)):
            y_ref[bi] = y[0:tt].astype(y_ref.dtype)
    else:
        def body(i, carry):
            start = pl.multiple_of(i * SPAT, SPAT)
            ext_rows = pl.ds(pl.multiple_of(start + off, SUBLANE), SPAT)
            gens = [chunk(bi, ext_rows, pl.ds(start, SPAT), SPAT) for bi in range(nb)]
            for bi, y in enumerate(_interleave(gens)):
                y_ref[bi, pl.ds(start, SPAT), :] = y.astype(y_ref.dtype)
            return carry
        lax.fori_loop(0, tt // SPAT, body, 0)


def _ssd(cin, gates, conv_w, conv_b, dtb, a_log_x, dskip, normg, conv0, s0, nb, tt):
    bsz, t, wtot = cin.shape
    w = dskip.shape[1]
    xw = wtot - w
    ng, n_c, gp = s0.shape[1:]
    st = lambda shape: pl.BlockSpec((nb,) + shape, lambda b, j: (b,) + (0,) * len(shape))
    est = nb * (2 * 2 * tt * wtot * 4 + 2 * tt * w * 4 + (tt + SUBLANE) * xw * 4 + 3 * tt * xw * 4 + 40 * SPAT * w * 4)
    return pl.pallas_call(
        _ssd_kernel,
        out_shape=[jax.ShapeDtypeStruct((bsz, t, w), BF16),
                   jax.ShapeDtypeStruct((bsz, CONV_C - 1, xw), F32),
                   jax.ShapeDtypeStruct(s0.shape, F32)],
        grid=(bsz // nb, t // tt),
        in_specs=[pl.BlockSpec((nb, tt, xw), lambda b, j: (b, j, 0)),
                  pl.BlockSpec((nb, tt, w), lambda b, j: (b, j, xw // w)),
                  pl.BlockSpec((nb, tt, GATE_W), lambda b, j: (b, j, 0)),
                  _const_spec((CONV_C, xw)), _const_spec((1, xw)), _const_spec((1, w)), _const_spec((1, w)),
                  _const_spec((1, w)), _const_spec((1, w)),
                  st((CONV_C - 1, xw)), st((ng, n_c, gp))],
        out_specs=[pl.BlockSpec((nb, tt, w), lambda b, j: (b, j, 0)), st((CONV_C - 1, xw)), st((ng, n_c, gp))],
        scratch_shapes=[pltpu.VMEM((nb, tt + SUBLANE, xw), F32)],
        name="ssd",
        compiler_params=_params(("parallel", "arbitrary"), est),
    )(cin, cin, gates, conv_w, conv_b, dtb, a_log_x, dskip, normg, conv0, s0)


def _gmlp_kernel(u_ref, v_ref, ng_ref, ws_ref, bs_ref, y_ref, *vn_refs):
    tt, w = u_ref.shape[1], u_ref.shape[2]
    u = jax.nn.gelu(u_ref[0])
    vr = jax.nn.gelu(v_ref[0])
    mu = jnp.mean(vr, axis=-1, keepdims=True)
    xc = vr - mu
    vn = xc * lax.rsqrt(jnp.mean(xc * xc, axis=-1, keepdims=True) + EPS) * ng_ref[...]
    for vn_ref in vn_refs:
        vn_ref[0] = vn
    rows = _iota((N_HEADS * SPAT, SPAT), 0) % SPAT
    wst = jnp.where(rows >= _iota((N_HEADS * SPAT, SPAT), 1), ws_ref[...], 0.0).astype(BF16)
    lmask = _head_lane_masks(SPAT, w)
    bias = bs_ref[...]
    vpad = _pad_rows(vn, -(-tt // SPAT) * SPAT).astype(BF16)
    for i in range(vpad.shape[0] // SPAT):
        fs = jnp.dot(wst, vpad[i * SPAT:(i + 1) * SPAT], preferred_element_type=F32)
        f = bias
        for g in range(N_HEADS):
            f = f + jnp.where(lmask[g], fs[g * SPAT:(g + 1) * SPAT], 0.0)
        n = min(SPAT, tt - i * SPAT)
        y_ref[0, i * SPAT:i * SPAT + n, :] = (u[i * SPAT:i * SPAT + n] * f[0:n]).astype(y_ref.dtype)


def _gmlp(din, normg, w_s, b_x, tt, emit_v):
    bsz, t, w2 = din.shape
    w = w2 // 2
    est = 2 * 4 * tt * w * 4 + 8 * tt * w * 4 + 4 * N_HEADS * SPAT * (SPAT + w) * 4
    out_shape = [jax.ShapeDtypeStruct((bsz, t, w), BF16)]
    if emit_v:
        out_shape.append(jax.ShapeDtypeStruct((bsz, t, w), F32))
    return pl.pallas_call(
        _gmlp_kernel,
        out_shape=out_shape,
        grid=(bsz, t // tt),
        in_specs=[pl.BlockSpec((1, tt, w), lambda b, j: (b, j, 0)),
                  pl.BlockSpec((1, tt, w), lambda b, j: (b, j, 1)),
                  _const_spec((1, w)), _const_spec((N_HEADS * SPAT, SPAT)), _const_spec((SPAT, w))],
        out_specs=[pl.BlockSpec((1, tt, w), lambda b, j: (b, j, 0))] * len(out_shape),
        name="spatial_gate",
        compiler_params=_params(("parallel", "parallel"), est),
    )(din, din, normg, w_s, b_x)


def _post_kernel(x_ref, ya_ref, yb_ref, yc_ref, yd_ref, mod_ref, g_ref, wo_ref, wg_ref, wu_ref, wd_ref,
                 cw_ref, cb_ref, f0_ref, gfin_ref, o_ref, fc_ref, ext_ref, *, final, fchunk):
    bb, tt, d = x_ref.shape
    m = bb * tt
    gw = ya_ref.shape[2]
    f = wg_ref.shape[1]
    hist = CONV_F - 1
    off = SUBLANE
    mod = mod_ref[...]
    mix = None
    for j, y_ref in enumerate((ya_ref, yb_ref, yc_ref, yd_ref)):
        p = jnp.dot(y_ref[...].reshape(m, gw), wo_ref[j * gw:(j + 1) * gw, :], preferred_element_type=F32)
        mix = p if mix is None else mix + p
    x1 = x_ref[...] + mod[:, :, 2 * d:3 * d] * mix.reshape(bb, tt, d)
    h2 = _rmsnorm_rows(x1, g_ref[...]) * (1.0 + mod[:, :, 4 * d:5 * d]) + mod[:, :, 3 * d:4 * d]
    hb = h2.reshape(m, d).astype(BF16)

    @pl.when(pl.program_id(1) == 0)
    def _():
        ext_ref[:, off - hist:off, :] = f0_ref[...]

    down = None
    for c0 in range(0, f, fchunk):
        cols = slice(c0, c0 + fchunk)
        g = jnp.dot(hb, wg_ref[:, cols], preferred_element_type=F32).reshape(bb, tt, fchunk)
        u = jnp.dot(hb, wu_ref[:, cols], preferred_element_type=F32)
        ext_ref[:, off:off + tt, cols] = g
        acc = cb_ref[:, cols] + cw_ref[CONV_F - 1:CONV_F, cols] * g
        for j in range(CONV_F - 1):
            sh = CONV_F - 1 - j
            acc = acc + cw_ref[j:j + 1, cols] * ext_ref[:, off - sh:off - sh + tt, cols]
        act = (acc * jax.nn.sigmoid(acc)).reshape(m, fchunk) * u
        p = jnp.dot(act.astype(BF16), wd_ref[cols, :], preferred_element_type=F32)
        down = p if down is None else down + p
    tail = ext_ref[:, off + tt - hist:off + tt, :]
    ext_ref[:, off - hist:off, :] = tail
    fc_ref[...] = tail
    x2 = x1 + mod[:, :, 5 * d:6 * d] * down.reshape(bb, tt, d)
    if final:
        x2 = _rmsnorm_rows(x2, gfin_ref[...])
    o_ref[...] = x2


def _post(x, ys, mod, g_ffn, w_out, w_gate, w_up, w_down, conv_w, conv_b, f0, g_final, bb, tt, final):
    bsz, t, d = x.shape
    gw = ys[0].shape[2]
    f = w_gate.shape[1]
    m = bb * tt
    tok = lambda wd: pl.BlockSpec((bb, tt, wd), lambda i, j: (i, j, 0))
    fchunk = 2 * LANE
    assert f % fchunk == 0
    est = (2 * 2 * m * d * 4 + 2 * 4 * m * gw * 2 + (d * d + 3 * d * f) * 2 + bb * (tt + SUBLANE) * f * 4
           + 8 * m * fchunk * 4 + 6 * m * d * 4)
    return pl.pallas_call(
        functools.partial(_post_kernel, final=final, fchunk=fchunk),
        out_shape=[jax.ShapeDtypeStruct((bsz, t, d), F32), jax.ShapeDtypeStruct((bsz, CONV_F - 1, f), F32)],
        grid=(bsz // bb, t // tt),
        in_specs=[tok(d), tok(gw), tok(gw), tok(gw), tok(gw),
                  pl.BlockSpec((bb, 1, mod.shape[2]), lambda i, j: (i, 0, 0)),
                  _const_spec((1, 1, d)), _const_spec((d, d)), _const_spec((d, f)), _const_spec((d, f)),
                  _const_spec((f, d)), _const_spec((CONV_F, f)), _const_spec((1, f)),
                  pl.BlockSpec((bb, CONV_F - 1, f), lambda i, j: (i, 0, 0)),
                  _const_spec((1, 1, d))],
        out_specs=[tok(d), pl.BlockSpec((bb, CONV_F - 1, f), lambda i, j: (i, 0, 0))],
        scratch_shapes=[pltpu.VMEM((bb, tt + SUBLANE, f), F32)],
        name="post_ffn",
        compiler_params=_params(("parallel", "arbitrary"), est),
    )(x, *ys, mod, g_ffn.reshape(1, 1, d), w_out, w_gate, w_up, w_down, conv_w, conv_b, f0, g_final.reshape(1, 1, d))


def _mixpost_kernel(qkv_ref, qkvo_ref, cin_ref, din_ref, gate_ref, x_ref, mod_ref, *refs, nt, final, fchunk, n_hist):
    hist_refs = refs[:2 * n_hist]
    (bias_ref, bi_ref, bf_ref, bng_ref, ccw_ref, ccb_ref, dtb_ref, alog_ref, dskip_ref, cng_ref,
     dng_ref, dws_ref, dbx_ref, gffn_ref, wo_ref, wg_ref, wu_ref, wd_ref, fcw_ref, fcb_ref, gfin_ref,
     o_ref, cout_ref, n_ref, m_ref, conv_ref, s_ref, fc_ref,
     kk_ref, vv_ref, ybuf_ref, cext_ref, fext_ref, fhist_ref, c_ref) = refs[2 * n_hist:]
    tt, d = x_ref.shape[1], x_ref.shape[2]
    w = d // 4
    f = wg_ref.shape[1]
    band = (N_BAND + 1) * CHUNK
    off = SUBLANE
    s = pl.program_id(0)
    n_tiles = pl.num_programs(0) - 1
    live = s < n_tiles
    mj = jnp.minimum(s, n_tiles - 1) % nt
    pj = jnp.maximum(s - 1, 0) % nt
    wslot = s % 2
    rslot = 1 - wslot

    @pl.when(s == 0)
    def _():
        ybuf_ref[...] = jnp.zeros(ybuf_ref.shape, ybuf_ref.dtype)

    @pl.when(mj == 0)
    def _():
        c_ref[...] = jnp.zeros(c_ref.shape, F32)
        n_ref[...] = jnp.zeros(n_ref.shape, F32)
        m_ref[...] = jnp.zeros(m_ref.shape, F32)
        s_ref[...] = jnp.zeros(s_ref.shape, F32)
        cext_ref[0:off, :] = jnp.zeros((off, cext_ref.shape[1]), F32)

    @pl.when(pj == 0)
    def _():
        fhist_ref[...] = jnp.zeros(fhist_ref.shape, F32)

    ex_i, ex_f, ex_dt = (_head_expander(w, first) for first in (GATE_I, GATE_F, GATE_DT))

    def attention():
        scale = (w // N_HEADS) ** -0.5
        for i, (kh_ref, vh_ref) in enumerate(zip(hist_refs[0::2], hist_refs[1::2])):
            kk_ref[i * tt:(i + 1) * tt, :] = kh_ref[0].astype(BF16)
            vv_ref[i * tt:(i + 1) * tt, :] = vh_ref[0].astype(BF16)
        kk_ref[A_WIN:A_WIN + tt, :] = qkv_ref[0, :, w:2 * w].astype(BF16)
        vv_ref[A_WIN:A_WIN + tt, :] = qkv_ref[0, :, 2 * w:3 * w].astype(BF16)
        masks = _head_lane_masks(CHUNK, w)
        bias = bias_ref[...]
        slot = _iota((1, band), 1)
        q_all = qkv_ref[0, :, 0:w].astype(F32) * scale
        yield
        for i in range(tt // CHUNK):
            base = i * CHUNK
            pos = slot + base
            kvalid = pos >= A_WIN
            for back in range(1, A_WIN // tt + 1):
                kvalid = kvalid | ((pos >= A_WIN - back * tt) & (mj >= back))
            out = yield from _attend_chunk(q_all[base:base + CHUNK], kk_ref[base:base + band, :],
                                           vv_ref[base:base + band, :], bias, kvalid, masks, masks)
            ybuf_ref[wslot, 0, base:base + CHUNK, :] = out.astype(BF16)
            yield

    def mlstm():
        kscale = (w // N_HEADS) ** -0.5
        bias_i, bias_f, ng = bi_ref[...], bf_ref[...], bng_ref[...]
        for c in range(tt // SPAT):
            rows = slice(c * SPAT, (c + 1) * SPAT)
            col = lambda c: qkvo_ref[0, rows, c * w:(c + 1) * w].astype(F32)
            g_p = _bf16_pieces(gate_ref[0, rows, :], GATE_PIECES)
            yield
            y = yield from _mlstm_chunk(
                col(0), col(1) * kscale, col(2), col(3),
                _dot_pieces(g_p, ex_i) + bias_i, _dot_pieces(g_p, ex_f) + bias_f,
                ng, c_ref, n_ref, m_ref, 0, SPAT, live)
            ybuf_ref[wslot, 1, rows, :] = y.astype(BF16)
            yield

    def ssd():
        xw = cext_ref.shape[1]
        hist = CONV_C - 1
        cext_ref[off:off + tt, :] = cin_ref[0, :, 0:xw].astype(F32)
        acc = ccb_ref[...] + ccw_ref[CONV_C - 1:CONV_C, :] * cext_ref[off:off + tt, :]
        for j in range(CONV_C - 1):
            sh = CONV_C - 1 - j
            acc = acc + ccw_ref[j:j + 1, :] * cext_ref[off - sh:off - sh + tt, :]
            yield
        tail = cext_ref[off + tt - hist:off + tt, :]
        cext_ref[off - hist:off, :] = tail
        conv_ref[0] = tail
        cext_ref[off:off + tt, :] = acc * jax.nn.sigmoid(acc)
        yield
        a_x = -jnp.exp(alog_ref[...])
        dtb, dskip, ng = dtb_ref[...], dskip_ref[...], cng_ref[...]
        for c in range(tt // SPAT):
            rows = slice(c * SPAT, (c + 1) * SPAT)
            erows = slice(off + c * SPAT, off + (c + 1) * SPAT)
            dt_p = _bf16_pieces(gate_ref[0, rows, :], GATE_PIECES)
            yield
            dt = jax.nn.softplus(_dot_pieces(dt_p, ex_dt) + dtb)
            y = yield from _ssd_chunk(cext_ref[erows, 0:w], cext_ref[erows, w:2 * w], cext_ref[erows, 2 * w:3 * w],
                                      cin_ref[0, rows, xw:xw + w].astype(F32), dt, a_x, dskip, ng, s_ref, 0, live)
            ybuf_ref[wslot, 2, rows, :] = y.astype(BF16)
            yield

    def gmlp():
        u = jax.nn.gelu(din_ref[0, :, 0:w].astype(F32))
        vr = jax.nn.gelu(din_ref[0, :, w:2 * w].astype(F32))
        yield
        xc = vr - jnp.mean(vr, axis=-1, keepdims=True)
        vn = (xc * lax.rsqrt(jnp.mean(xc * xc, axis=-1, keepdims=True) + EPS) * dng_ref[...]).astype(BF16)
        rows = _iota((N_HEADS * SPAT, SPAT), 0) % SPAT
        wst = jnp.where(rows >= _iota((N_HEADS * SPAT, SPAT), 1), dws_ref[...], 0.0).astype(BF16)
        lmask = _head_lane_masks(SPAT, w)
        yield
        for i in range(tt // SPAT):
            fs = jnp.dot(wst, vn[i * SPAT:(i + 1) * SPAT], preferred_element_type=F32)
            yield
            fgate = dbx_ref[...]
            for g in range(N_HEADS):
                fgate = fgate + jnp.where(lmask[g], fs[g * SPAT:(g + 1) * SPAT], 0.0)
            ybuf_ref[wslot, 3, i * SPAT:(i + 1) * SPAT, :] = (u[i * SPAT:(i + 1) * SPAT] * fgate).astype(BF16)
            yield

    def post():
        hist = CONV_F - 1
        mod = mod_ref[0]
        mix = None
        for j in range(4):
            p = jnp.dot(ybuf_ref[rslot, j], wo_ref[j * w:(j + 1) * w, :], preferred_element_type=F32)
            mix = p if mix is None else mix + p
            yield
        x1 = x_ref[0] + mod[:, 2 * d:3 * d] * mix
        h2 = _rmsnorm_rows(x1, gffn_ref[...]) * (1.0 + mod[:, 4 * d:5 * d]) + mod[:, 3 * d:4 * d]
        hb = h2.astype(BF16)
        yield
        down = None
        act_b16, act_cols = None, None
        for ci, c0 in enumerate(range(0, f, fchunk)):
            cols = slice(c0, c0 + fchunk)
            g = jnp.dot(hb, wg_ref[:, cols], preferred_element_type=F32)
            u = jnp.dot(hb, wu_ref[:, cols], preferred_element_type=F32)
            if act_b16 is not None:
                p = jnp.dot(act_b16, wd_ref[act_cols, :], preferred_element_type=F32)
                down = p if down is None else down + p
            yield
            buf = ci % 2
            fext_ref[buf, off - hist:off, :] = fhist_ref[0, off - hist:off, cols]
            fext_ref[buf, off:off + tt, :] = g
            acc = fcb_ref[:, cols] + fcw_ref[CONV_F - 1:CONV_F, cols] * g
            for j in range(CONV_F - 1):
                sh = CONV_F - 1 - j
                acc = acc + fcw_ref[j:j + 1, cols] * fext_ref[buf, off - sh:off - sh + tt, :]
            fhist_ref[0, off - hist:off, cols] = g[tt - hist:tt]
            act_b16, act_cols = ((acc * jax.nn.sigmoid(acc)) * u).astype(BF16), cols
            yield
        down = down + jnp.dot(act_b16, wd_ref[act_cols, :], preferred_element_type=F32)
        fc_ref[...] = fhist_ref[:, off - hist:off, :]
        x2 = x1 + mod[:, 5 * d:6 * d] * down
        if final:
            x2 = _rmsnorm_rows(x2, gfin_ref[...])
        o_ref[0] = x2

    _interleave([post(), mlstm(), ssd(), attention(), gmlp()])

    @pl.when(mj == nt - 1)
    def _():
        cout_ref[0] = _blockdiag_to_heads(c_ref[0])


def _mix_post(qkv_a, qkvo_b, c_in, d_in, gates, x, mod, p, g_final, tt, final):
    bsz, t, d = x.shape
    w = d // 4
    nt = t // tt
    n_tiles = bsz * nt
    f = p["f_w_gate"].shape[1]
    xw = c_in.shape[2] - w
    ng, n_c, gp = 2, LANE, w // 2
    band = (N_BAND + 1) * CHUNK
    fchunk = 2 * LANE
    assert f % fchunk == 0 and tt % SPAT == 0 and t % tt == 0
    mix_tile = lambda width: pl.BlockSpec(
        (1, tt, width), lambda s: (jnp.minimum(s, n_tiles - 1) // nt, jnp.minimum(s, n_tiles - 1) % nt, 0))
    post_tile = lambda width: pl.BlockSpec(
        (1, tt, width), lambda s: (jnp.maximum(s - 1, 0) // nt, jnp.maximum(s - 1, 0) % nt, 0))
    mix_state = lambda shape: pl.BlockSpec(
        (1,) + shape, lambda s: (jnp.minimum(s, n_tiles - 1) // nt,) + (0,) * len(shape))
    post_state = lambda shape: pl.BlockSpec(
        (1,) + shape, lambda s: (jnp.maximum(s - 1, 0) // nt,) + (0,) * len(shape))
    consts = [p["bias_tab"].reshape(N_HEADS * CHUNK, band), p["bi"], p["bf"], p["b_norm_g"], p["c_conv_w"],
              p["c_conv_b"], p["dtb"], p["a_log_x"], p["dskip"], p["c_norm_g"], p["d_norm_g"], p["d_w_s"], p["d_b_x"],
              p["g_ffn"].reshape(1, d), p["w_out"], p["f_w_gate"], p["f_w_up"], p["f_w_down"], p["f_conv_w"],
              p["f_conv_b"], g_final.reshape(1, d)]
    in_widths = (qkv_a.shape[2], qkvo_b.shape[2], c_in.shape[2], d_in.shape[2], gates.shape[2])
    n_hist = A_WIN // tt
    assert n_hist * tt == A_WIN

    def hist_tile(back, col):
        def index(s):
            m = jnp.minimum(s, n_tiles - 1)
            return (m // nt, jnp.maximum(m % nt - back, 0), col)
        return pl.BlockSpec((1, tt, w), index)

    hist_specs = [hist_tile(back, col) for back in range(n_hist, 0, -1) for col in (1, 2)]
    in_bytes = sum(a.shape[2] * a.dtype.itemsize for a in (qkv_a, qkvo_b, c_in, d_in, gates))
    est = ((d * d + 3 * d * f) * 2 + 2 * tt * in_bytes + 4 * tt * d * 4 + (tt + SUBLANE) * (2 * fchunk + xw) * 4
           + 2 * (A_WIN + tt) * w * 2 + 8 * tt * w * 2 + 12 * tt * d * 4 + 6 * w * w * 4
           + 4 * n_hist * tt * w * qkv_a.dtype.itemsize)
    return pl.pallas_call(
        functools.partial(_mixpost_kernel, nt=nt, final=final, fchunk=fchunk, n_hist=n_hist),
        out_shape=[jax.ShapeDtypeStruct((bsz, t, d), F32),
                   jax.ShapeDtypeStruct((bsz, w, w // N_HEADS), F32),
                   jax.ShapeDtypeStruct((bsz, 1, w), F32),
                   jax.ShapeDtypeStruct((bsz, 1, w), F32),
                   jax.ShapeDtypeStruct((bsz, CONV_C - 1, xw), F32),
                   jax.ShapeDtypeStruct((bsz, ng, n_c, gp), F32),
                   jax.ShapeDtypeStruct((bsz, CONV_F - 1, f), F32)],
        grid=(n_tiles + 1,),
        in_specs=[mix_tile(wd) for wd in in_widths] + [post_tile(d), post_state((1, mod.shape[2]))] + hist_specs
                 + [_const_spec(c.shape) for c in consts],
        out_specs=[post_tile(d), mix_state((w, w // N_HEADS)), mix_state((1, w)), mix_state((1, w)),
                   mix_state((CONV_C - 1, xw)), mix_state((ng, n_c, gp)), post_state((CONV_F - 1, f))],
        scratch_shapes=[pltpu.VMEM((A_WIN + tt, w), BF16), pltpu.VMEM((A_WIN + tt, w), BF16),
                        pltpu.VMEM((2, 4, tt, w), BF16), pltpu.VMEM((tt + SUBLANE, xw), F32),
                        pltpu.VMEM((2, tt + SUBLANE, fchunk), F32), pltpu.VMEM((1, SUBLANE, f), F32),
                        pltpu.VMEM((1, w, w), F32)],
        name="mix_post",
        compiler_params=_params(("arbitrary",), est),
    )(qkv_a, qkvo_b, c_in, d_in, gates, x, mod, *([qkv_a] * len(hist_specs)), *consts)


def _layer(x, mod, st, p, tiles, final, g_final):
    bsz, t, d = x.shape
    gw = d // 4
    dh = gw // N_HEADS
    bb, tt, tmix, nb = tiles
    mod3 = mod.reshape(bsz, 1, mod.shape[1])
    dtypes = (BF16,) * 4 + (F32,) if st is None else (F32,) * 5
    qkv_a, qkvo_b, c_in, d_in, gates = _in_proj(x, mod3, p["g_mix"], p["w_in"], p["widths"], dtypes, bb, tt)

    if st is None:
        x_new, c_new, n_new, m_new, conv_new, ssm_new, fconv_new = _mix_post(
            qkv_a, qkvo_b, c_in, d_in, gates, x, mod3, p, g_final, tmix, final)
        vn = []
    else:
        y_a = _attention(qkv_a, st["a_k"], st["a_v"], p["bias_tab"], tmix)
        y_b, c_new, n_new, m_new = _mlstm(qkvo_b, gates, p["bi"], p["bf"], p["b_norm_g"], st["b_c"], st["b_n"],
                                          st["b_m"], nb, tmix)
        y_c, conv_new, ssm_new = _ssd(c_in, gates, p["c_conv_w"], p["c_conv_b"], p["dtb"], p["a_log_x"], p["dskip"],
                                      p["c_norm_g"], st["c_conv"], st["c_ssm"], nb, tmix)
        y_d, *vn = _gmlp(d_in, p["d_norm_g"], p["d_w_s"], p["d_b_x"], tmix, True)
        x_new, fconv_new = _post(x, (y_a, y_b, y_c, y_d), mod3, p["g_ffn"], p["w_out"], p["f_w_gate"], p["f_w_up"],
                                 p["f_w_down"], p["f_conv_w"], p["f_conv_b"], st["f_conv"], g_final, bb, tt, final)

    keep = min(A_WIN, t)
    new_k = qkv_a[:, t - keep:, gw:2 * gw].reshape(bsz, keep, N_HEADS, dh).astype(F32)
    new_v = qkv_a[:, t - keep:, 2 * gw:3 * gw].reshape(bsz, keep, N_HEADS, dh).astype(F32)
    c_heads = c_new.reshape(bsz, N_HEADS, dh, dh)
    n_heads = n_new.reshape(bsz, N_HEADS, dh)
    m_heads = m_new[:, 0, ::dh]
    ssm = jnp.swapaxes(ssm_new, 2, 3).reshape(bsz, N_HEADS, gw // N_HEADS, ssm_new.shape[2])
    outs = (new_k, new_v, c_heads, n_heads, m_heads, ssm, conv_new, fconv_new, *vn)
    return x_new, outs


def _prep_layer(l, w_in, w_out, a_rel_bias, b_i_bias, b_f_bias, b_norm_g, c_conv_w, c_conv_b, c_dt_bias, c_a_log,
                c_d_skip, c_norm_g, d_norm_g, d_w_s, d_b_s, f_w_gate, f_w_up, f_conv_w, f_conv_b, f_w_down,
                g_norm_mix, g_norm_ffn):
    d = w_in.shape[1]
    gw = d // 4
    nh = b_i_bias.shape[1]
    xbc_w = c_conv_w.shape[2]
    sizes = (gw,) * 7 + (nh, nh, gw, xbc_w, c_dt_bias.shape[1], gw, gw)
    offs = [0]
    for s in sizes:
        offs.append(offs[-1] + s)
    moves, dst = [], 0
    for i in (0, 1, 2, 3, 4, 5, 6, 10, 9, 12, 13):
        moves.append((offs[i], dst, sizes[i]))
        dst += sizes[i]
    assert sizes[7] == sizes[8] == sizes[11] == N_HEADS
    packed = (dst, ((offs[7], GATE_I, N_HEADS), (offs[8], GATE_F, N_HEADS), (offs[11], GATE_DT, N_HEADS)))
    widths = (3 * gw, 4 * gw, xbc_w + gw, 2 * gw, GATE_W)
    dst += GATE_W
    assert dst == sum(widths)
    heads_x = lambda v: jnp.repeat(v.astype(F32), gw // nh)[None, :]
    return dict(
        w_in=_regroup_cast(w_in, l, moves, packed, dst), widths=widths, g_mix=g_norm_mix[l], g_ffn=g_norm_ffn[l],
        w_out=w_out[l].astype(BF16), bias_tab=_bias_table(a_rel_bias[l]),
        bi=heads_x(b_i_bias[l]), bf=heads_x(b_f_bias[l]), b_norm_g=b_norm_g[l][None, :],
        c_conv_w=c_conv_w[l], c_conv_b=c_conv_b[l][None, :], dtb=heads_x(c_dt_bias[l]),
        a_log_x=heads_x(c_a_log[l]), dskip=heads_x(c_d_skip[l]), c_norm_g=c_norm_g[l][None, :],
        d_norm_g=d_norm_g[l][None, :], d_w_s=d_w_s[l].reshape(-1, d_w_s.shape[-1]),
        d_b_x=jnp.repeat(d_b_s[l].T, gw // d_b_s.shape[1], axis=1),
        f_w_gate=f_w_gate[l].astype(BF16), f_w_up=f_w_up[l].astype(BF16), f_w_down=f_w_down[l].astype(BF16),
        f_conv_w=f_conv_w[l], f_conv_b=f_conv_b[l][None, :],
    )


def kernel(x_prompt, x_sample, c_prompt, c_sample, cache_a_k, cache_a_v, state_b_c, state_b_n, state_b_m, state_c_ssm, state_c_conv, state_ffn_conv, w_ada, b_ada, g_norm_mix, g_norm_ffn, w_in, w_out, a_rel_bias, b_i_bias, b_f_bias, b_norm_g, c_conv_w, c_conv_b, c_dt_bias, c_a_log, c_d_skip, c_norm_g, d_norm_g, d_w_s, d_b_s, f_w_gate, f_w_up, f_conv_w, f_conv_b, f_w_down, g_final):
    depth = w_in.shape[0]
    bp, tp, d = x_prompt.shape
    bs, ts, _ = x_sample.shape
    gw = d // 4
    dh = gw // N_HEADS
    xbc_w = c_conv_w.shape[2]
    n_c = state_c_ssm.shape[-1]
    g_c = (xbc_w - gw) // 2 // n_c

    mod_all = _ada(jnp.concatenate([c_prompt, c_sample], axis=0), w_ada, b_ada)

    nb = max(n for n in (4, 2, 1) if bs % n == 0)
    tiles_p = (1, min(A_WIN, tp), min(A_WIN, tp), 1)
    tiles_s = (bs, ts, ts, nb)

    xp, xs = x_prompt, x_sample
    p_states, s_states = [], []
    for l in range(depth):
        p = _prep_layer(l, w_in, w_out, a_rel_bias, b_i_bias, b_f_bias, b_norm_g, c_conv_w, c_conv_b, c_dt_bias,
                        c_a_log, c_d_skip, c_norm_g, d_norm_g, d_w_s, d_b_s, f_w_gate, f_w_up, f_conv_w, f_conv_b,
                        f_w_down, g_norm_mix, g_norm_ffn)
        final = l == depth - 1
        xp, sp = _layer(xp, mod_all[l, :bp], None, p, tiles_p, final, g_final)
        st_s = dict(a_k=cache_a_k[l].reshape(bs, -1, gw), a_v=cache_a_v[l].reshape(bs, -1, gw),
                    b_c=state_b_c[l].reshape(bs, gw, dh), b_n=state_b_n[l].reshape(bs, 1, gw),
                    b_m=jnp.repeat(state_b_m[l], dh, axis=1)[:, None, :],
                    c_conv=state_c_conv[l],
                    c_ssm=jnp.swapaxes(state_c_ssm[l].reshape(bs, g_c, gw // g_c, n_c), 2, 3),
                    f_conv=state_ffn_conv[l])
        xs, ss = _layer(xs, mod_all[l, bp:], st_s, p, tiles_s, final, g_final)
        p_states.append(sp)
        s_states.append(ss)

    stack = lambda states, i: jnp.stack([s[i] for s in states])
    return (xp, xs,
            *(stack(p_states, i) for i in range(8)),
            *(stack(s_states, i) for i in range(9)))
```

```python
import functools

import jax
import jax.numpy as jnp
from jax import lax
from jax.experimental import pallas as pl
from jax.experimental.pallas import tpu as pltpu

F32 = jnp.float32
BF16 = jnp.bfloat16

EPS = 1e-6
NEG = -1e30

CHUNK = 64
N_BAND = 8
A_WIN = N_BAND * CHUNK
REL_CLIP = 128
N_HEADS = 4
SPAT = 128
CONV_C = 4
CONV_F = 3
LANE = 128
GATE_I, GATE_F, GATE_DT = 0, 4, 8
GATE_W = LANE
GATE_PIECES = 2
SUBLANE = 8
VMEM_CAP = 64 * 1024 * 1024


def _vmem_limit(nbytes):
    return int(min(max(nbytes, 16 * 1024 * 1024), VMEM_CAP - 8 * 1024 * 1024))


def _params(sem, nbytes):
    return pltpu.CompilerParams(dimension_semantics=sem, vmem_limit_bytes=_vmem_limit(nbytes))


def _const_spec(shape):
    nd = len(shape)
    return pl.BlockSpec(shape, lambda *_: (0,) * nd, pipeline_mode=pl.Buffered(1))


def _iota(shape, dim):
    return lax.broadcasted_iota(jnp.int32, shape, dim)


def _bf16_pieces(x, parts):
    out = []
    r = x
    for i in range(parts):
        hi = r.astype(BF16)
        out.append(hi)
        if i + 1 < parts:
            r = r - hi.astype(F32)
    return out


def _dot_pieces(pieces, e):
    acc = None
    for piece in pieces:
        d = jnp.dot(piece, e, preferred_element_type=F32)
        acc = d if acc is None else acc + d
    return acc


def _dot_pieces_left(e, pieces):
    acc = None
    for piece in pieces:
        d = jnp.dot(e, piece, preferred_element_type=F32)
        acc = d if acc is None else acc + d
    return acc


def _split_dot(x, e, parts):
    return _dot_pieces(_bf16_pieces(x, parts), e)


def _dot_nt(a, b):
    return lax.dot_general(a, b, (((1,), (1,)), ((), ())), preferred_element_type=F32)


def _dot_tn(a, b):
    return lax.dot_general(a, b, (((0,), (0,)), ((), ())), preferred_element_type=F32)


def _head_expander(width, first):
    dh = width // N_HEADS
    return (_iota((LANE, width), 1) // dh == _iota((LANE, width), 0) - first).astype(BF16)


def _head_lane_masks(rows, width):
    dh = width // N_HEADS
    lane = _iota((rows, width), 1)
    return [(lane >= h * dh) & (lane < (h + 1) * dh) for h in range(N_HEADS)]


def _tril(n):
    return _iota((n, n), 0) >= _iota((n, n), 1)


def _pad_rows(x, rows):
    if x.shape[0] == rows:
        return x
    return jnp.concatenate([x, jnp.zeros((rows - x.shape[0], x.shape[1]), x.dtype)], axis=0)


def _interleave(gens):
    results = [None] * len(gens)
    live = list(range(len(gens)))
    while live:
        for i in list(live):
            try:
                next(gens[i])
            except StopIteration as stop:
                results[i] = stop.value
                live.remove(i)
    return results


def _rmsnorm_rows(x, g):
    return x * lax.rsqrt(jnp.mean(x * x, axis=-1, keepdims=True) + EPS) * g


def _regroup_kernel(w_ref, o_ref, *, moves, packed):
    w = w_ref[0]
    rows = w.shape[0]
    for src, dst, n in moves:
        o_ref[:, dst:dst + n] = w[:, src:src + n].astype(o_ref.dtype)
    dst, pieces = packed
    parts, lane = [], 0
    for src, first, n in pieces:
        if first > lane:
            parts.append(jnp.zeros((rows, first - lane), F32))
        parts.append(w[:, src:src + n])
        lane = first + n
    parts.append(jnp.zeros((rows, GATE_W - lane), F32))
    o_ref[:, dst:dst + GATE_W] = jnp.concatenate(parts, axis=1).astype(o_ref.dtype)


def _regroup_cast(w_all, layer, moves, packed, ncols, row_block=256):
    _, r, c = w_all.shape
    return pl.pallas_call(
        functools.partial(_regroup_kernel, moves=tuple(moves), packed=packed),
        out_shape=jax.ShapeDtypeStruct((r, ncols), BF16),
        grid=(r // row_block,),
        in_specs=[pl.BlockSpec((1, row_block, c), lambda i: (layer, i, 0))],
        out_specs=pl.BlockSpec((row_block, ncols), lambda i: (i, 0)),
        name="regroup_cast",
        compiler_params=_params(("parallel",), 2 * row_block * (c * 4 + ncols * 2) + row_block * c * 4),
    )(w_all)


def _ada_kernel(c_ref, w_ref, b_ref, o_ref):
    c = c_ref[...]
    h = (c * jax.nn.sigmoid(c)).astype(BF16)
    o_ref[0] = jnp.dot(h, w_ref[0].astype(BF16), preferred_element_type=F32) + b_ref[0]


def _ada(c_all, w_ada, b_ada):
    depth, d, n6 = w_ada.shape
    r = c_all.shape[0]
    tn = d
    return pl.pallas_call(
        _ada_kernel,
        out_shape=jax.ShapeDtypeStruct((depth, r, n6), F32),
        grid=(depth, n6 // tn),
        in_specs=[pl.BlockSpec((r, d), lambda l, j: (0, 0)),
                  pl.BlockSpec((1, d, tn), lambda l, j: (l, 0, j)),
                  pl.BlockSpec((1, 1, tn), lambda l, j: (l, 0, j))],
        out_specs=pl.BlockSpec((1, r, tn), lambda l, j: (l, 0, j)),
        name="ada_mod",
        compiler_params=_params(("parallel", "parallel"), 4 * (2 * d * tn * 4 + 2 * r * tn * 4 + r * d * 4)),
    )(c_all, w_ada, b_ada.reshape(depth, 1, n6))


def _bias_kernel(rb_ref, o_ref, *, lo, hi):
    nh, lq, lk = o_ref.shape
    idx = jnp.clip(A_WIN + _iota((lq, lk), 0) - _iota((lq, lk), 1), -REL_CLIP, REL_CLIP) + REL_CLIP
    for h in range(nh):
        def body(r, acc, h=h):
            return jnp.where(idx == r, rb_ref[h, r], acc)
        o_ref[h] = lax.fori_loop(lo, hi + 1, body, jnp.zeros((lq, lk), F32))


def _bias_table(rel_bias):
    nh = rel_bias.shape[0]
    band = (N_BAND + 1) * CHUNK
    lo = max(A_WIN - (band - 1), -REL_CLIP) + REL_CLIP
    hi = min(A_WIN + CHUNK - 1, REL_CLIP) + REL_CLIP
    return pl.pallas_call(
        functools.partial(_bias_kernel, lo=lo, hi=hi),
        out_shape=jax.ShapeDtypeStruct((nh, CHUNK, band), F32),
        in_specs=[pl.BlockSpec(memory_space=pltpu.SMEM)],
        out_specs=pl.BlockSpec(memory_space=pltpu.VMEM),
        name="rel_bias_table",
    )(rel_bias)


def _in_kernel(x_ref, mod_ref, g_ref, w_ref, *o_refs, col_starts):
    bb, tt, d = x_ref.shape
    x = x_ref[...]
    mod = mod_ref[...]
    h = _rmsnorm_rows(x, g_ref[...]) * (1.0 + mod[:, :, d:2 * d]) + mod[:, :, 0:d]
    hb = h.reshape(bb * tt, d).astype(BF16)
    for o_ref, (a, b) in zip(o_refs, col_starts):
        o_ref[...] = jnp.dot(hb, w_ref[:, a:b], preferred_element_type=F32).reshape(bb, tt, b - a).astype(o_ref.dtype)


def _in_proj(x, mod, g, w, widths, dtypes, bb, tt):
    bsz, t, d = x.shape
    ncols = w.shape[1]
    starts, a = [], 0
    for wd in widths:
        starts.append((a, a + wd))
        a += wd
    m = bb * tt
    est = 2 * m * d * 4 + 2 * d * ncols * 2 + 2 * m * ncols * 4 + 3 * m * d * 4 + m * max(widths) * 4
    return pl.pallas_call(
        functools.partial(_in_kernel, col_starts=tuple(starts)),
        out_shape=[jax.ShapeDtypeStruct((bsz, t, wd), dt) for wd, dt in zip(widths, dtypes)],
        grid=(bsz // bb, t // tt),
        in_specs=[pl.BlockSpec((bb, tt, d), lambda i, j: (i, j, 0)),
                  pl.BlockSpec((bb, 1, mod.shape[2]), lambda i, j: (i, 0, 0)),
                  _const_spec((1, 1, d)),
                  _const_spec((d, ncols))],
        out_specs=[pl.BlockSpec((bb, tt, wd), lambda i, j: (i, j, 0)) for wd in widths],
        name="in_proj",
        compiler_params=_params(("parallel", "parallel"), est),
    )(x, mod, g.reshape(1, 1, d), w)


def _attend_chunk(qc, kb, vb, bias, kvalid, masks_q, masks_o):
    lq = qc.shape[0]
    qs = jnp.concatenate([jnp.where(mk, qc, 0.0) for mk in masks_q], axis=0).astype(BF16)
    yield
    s = _dot_nt(qs, kb) + bias
    yield
    if kvalid is not None:
        s = jnp.where(kvalid, s, NEG)
    e = jnp.exp(s - jnp.max(s, axis=-1, keepdims=True))
    eb = e.astype(BF16)
    rinv = 1.0 / jnp.sum(e, axis=-1, keepdims=True)
    yield
    o = jnp.dot(eb, vb, preferred_element_type=F32)
    yield
    o = o * rinv
    out = jnp.where(masks_o[0], o[0:lq], 0.0)
    for h in range(1, N_HEADS):
        out = out + jnp.where(masks_o[h], o[h * lq:(h + 1) * lq], 0.0)
    return out


def _attn_kernel(q_ref, k_ref, v_ref, kh_ref, vh_ref, bias_ref, y_ref, kk_ref, vv_ref, *, hist_is_cache, t_valid):
    tq, w = q_ref.shape[1], q_ref.shape[2]
    tpad = kk_ref.shape[0] - A_WIN
    scale = (w // N_HEADS) ** -0.5
    kk_ref[0:A_WIN, :] = kh_ref[0].astype(BF16)
    vv_ref[0:A_WIN, :] = vh_ref[0].astype(BF16)
    kk_ref[A_WIN:A_WIN + tpad, :] = _pad_rows(k_ref[0], tpad).astype(BF16)
    vv_ref[A_WIN:A_WIN + tpad, :] = _pad_rows(v_ref[0], tpad).astype(BF16)
    masks = _head_lane_masks(CHUNK, w)
    bias = bias_ref[...]
    band = (N_BAND + 1) * CHUNK
    slot = _iota((1, band), 1)
    hist_ok = jnp.logical_or(pl.program_id(1) > 0, hist_is_cache)
    q_all = _pad_rows(q_ref[0], tpad) * scale
    gens = []
    for i in range(tpad // CHUNK):
        base = i * CHUNK
        pos = slot + base
        kvalid = (pos < A_WIN + t_valid) & ((pos >= A_WIN) | hist_ok)
        gens.append(_attend_chunk(q_all[base:base + CHUNK], kk_ref[base:base + band, :], vv_ref[base:base + band, :],
                                  bias, kvalid, masks, masks))
    for i, out in enumerate(_interleave(gens)):
        base = i * CHUNK
        rows = min(CHUNK, tq - base)
        y_ref[0, base:base + rows, :] = out[0:rows].astype(y_ref.dtype)


def _attention(qkv, k_hist, v_hist, bias_tab, tq):
    bsz, t, w3 = qkv.shape
    w = w3 // 3
    hist_is_cache = k_hist is not None
    nt = t // tq
    tpad = -(-tq // CHUNK) * CHUNK
    band = (N_BAND + 1) * CHUNK
    if hist_is_cache:
        assert nt == 1
        hist_specs = [pl.BlockSpec((1, A_WIN, w), lambda b, j: (b, 0, 0))] * 2
        hist_args = (k_hist, v_hist)
    else:
        assert tq == A_WIN
        hist_specs = [pl.BlockSpec((1, tq, w), lambda b, j: (b, jnp.maximum(j - 1, 0), 1)),
                      pl.BlockSpec((1, tq, w), lambda b, j: (b, jnp.maximum(j - 1, 0), 2))]
        hist_args = (qkv, qkv)
    est = 2 * 5 * tpad * w * 4 + 2 * tq * w * 4 + 2 * (A_WIN + tpad) * w * 2 + 12 * N_HEADS * CHUNK * band * 4
    return pl.pallas_call(
        functools.partial(_attn_kernel, hist_is_cache=hist_is_cache, t_valid=tq if nt == 1 else tpad),
        out_shape=jax.ShapeDtypeStruct((bsz, t, w), BF16),
        grid=(bsz, nt),
        in_specs=[pl.BlockSpec((1, tq, w), lambda b, j: (b, j, 0)),
                  pl.BlockSpec((1, tq, w), lambda b, j: (b, j, 1)),
                  pl.BlockSpec((1, tq, w), lambda b, j: (b, j, 2)),
                  *hist_specs,
                  _const_spec((N_HEADS * CHUNK, band))],
        out_specs=pl.BlockSpec((1, tq, w), lambda b, j: (b, j, 0)),
        scratch_shapes=[pltpu.VMEM((A_WIN + tpad, w), BF16), pltpu.VMEM((A_WIN + tpad, w), BF16)],
        name="band_attention",
        compiler_params=_params(("parallel", "parallel"), est),
    )(qkv, qkv, qkv, *hist_args, bias_tab.reshape(N_HEADS * CHUNK, band))


def _mlstm_chunk(q, k, v, og, gi, gf, normg, c_ref, n_ref, m_ref, bi, n_valid, live=None):
    L, w = q.shape
    dh = w // N_HEADS
    lmask = _head_lane_masks(L, w)
    tril = _tril(L)
    blockdiag = (_iota((w, w), 0) // dh) == (_iota((w, w), 1) // dh)
    bd = blockdiag.astype(BF16)

    m_prev = m_ref[bi]
    c_old = c_ref[bi]
    n_old = n_ref[bi]
    qb16 = q.astype(BF16)
    kb16 = k.astype(BF16)
    vb16 = v.astype(BF16)
    cb16 = c_old.astype(BF16)
    qmask = [jnp.where(lmask[h], q, 0.0).astype(BF16) for h in range(N_HEADS)]
    lf_p = _bf16_pieces(jax.nn.log_sigmoid(gf), 3)
    qn_p = _bf16_pieces(q * n_old, 2)
    yield
    b = _dot_pieces_left(tril.astype(BF16), lf_p)
    q_c = jnp.dot(qb16, cb16, preferred_element_type=F32)
    q_n = _dot_pieces(qn_p, bd)
    yield
    qk = [_dot_nt(qmask[h], kb16) for h in range(N_HEADS)]
    u = gi - b
    u_t = u.T
    yield
    cm = jnp.zeros((L, w), F32)
    for h in range(N_HEADS):
        cmh = jnp.max(jnp.where(tril, u_t[h * dh:h * dh + 1, :], NEG), axis=1, keepdims=True)
        cm = jnp.where(lmask[h], cmh, cm)
    mx = jnp.maximum(m_prev, cm)
    m_t = b + mx
    inter = jnp.exp(m_prev - mx)
    last = n_valid - 1
    m_last = m_t[last:last + 1, :]
    b_last = b[last:last + 1, :]
    decay = jnp.exp(b_last + m_prev - m_last)
    ws = jnp.exp(u + (b_last - m_last))
    if n_valid < L:
        ws = jnp.where(_iota((L, w), 0) < n_valid, ws, 0.0)
    kw = k * ws
    kwb16 = kw.astype(BF16)
    yield
    v_heads = jnp.concatenate([jnp.where(lmask[h], v, 0.0).astype(BF16) for h in range(N_HEADS)], axis=0)
    rs = jnp.zeros((L, w), F32)
    wq = []
    for h in range(N_HEADS):
        arg = jnp.where(tril, u_t[h * dh:h * dh + 1, :] - mx[:, h * dh:h * dh + 1], NEG)
        wqk = jnp.exp(arg) * qk[h]
        rs = jnp.where(lmask[h], jnp.sum(wqk, axis=1, keepdims=True), rs)
        wq.append(wqk.astype(BF16))
        yield
    num = jnp.dot(jnp.concatenate(wq, axis=1), v_heads, preferred_element_type=F32)
    upd = _dot_tn(kwb16, vb16)
    den = inter * q_n + rs
    hout = (inter * q_c + num) / jnp.maximum(jnp.abs(den), jnp.exp(-m_t))
    mu_p = _bf16_pieces(hout, 2)
    yield
    xc = hout - _dot_pieces(mu_p, bd) * (1.0 / dh)
    var_p = _bf16_pieces(xc * xc, 2)
    yield
    var = _dot_pieces(var_p, bd) * (1.0 / dh)
    yield
    y = jax.nn.sigmoid(og) * (xc * lax.rsqrt(var + EPS) * normg)
    keep = (lambda new, old: new) if live is None else (lambda new, old: jnp.where(live, new, old))
    c_ref[bi] = keep(decay * c_old + jnp.where(blockdiag, upd, 0.0), c_old)
    n_ref[bi] = keep(decay * n_old + jnp.sum(kw, axis=0, keepdims=True), n_old)
    m_ref[bi] = keep(m_last, m_prev)
    return y


def _heads_to_blockdiag(c):
    w, dh = c.shape
    tile = (_iota((dh, w), 1) % dh == _iota((dh, w), 0)).astype(BF16)
    blockdiag = (_iota((w, w), 0) // dh) == (_iota((w, w), 1) // dh)
    return jnp.where(blockdiag, _split_dot(c, tile, 3), 0.0)


def _blockdiag_to_heads(c_bd):
    w = c_bd.shape[0]
    dh = w // N_HEADS
    fold = (_iota((w, dh), 0) % dh == _iota((w, dh), 1)).astype(BF16)
    return _split_dot(c_bd, fold, 3)


def _mlstm_kernel(q_ref, k_ref, v_ref, o_ref, g_ref, bi_ref, bf_ref, ng_ref,
                  c0_ref, n0_ref, m0_ref, y_ref, cout_ref, n_ref, m_ref, c_ref):
    nb, tt, w = q_ref.shape
    kscale = (w // N_HEADS) ** -0.5

    @pl.when(pl.program_id(1) == 0)
    def _():
        for bi in range(nb):
            c_ref[bi] = _heads_to_blockdiag(c0_ref[bi])
        n_ref[...] = n0_ref[...]
        m_ref[...] = m0_ref[...]

    bias_i = bi_ref[...]
    bias_f = bf_ref[...]
    ng = ng_ref[...]
    ex_i, ex_f = _head_expander(w, GATE_I), _head_expander(w, GATE_F)

    def chunk(bi, rows, n_valid):
        pad = lambda r: _pad_rows(r[bi, rows, :], SPAT)
        g = pad(g_ref)
        return _mlstm_chunk(pad(q_ref), pad(k_ref) * kscale, pad(v_ref), pad(o_ref),
                            _split_dot(g, ex_i, 3) + bias_i, _split_dot(g, ex_f, 3) + bias_f,
                            ng, c_ref, n_ref, m_ref, bi, n_valid)

    if tt < SPAT:
        for bi, y in enumerate(_interleave([chunk(bi, slice(None), tt) for bi in range(nb)])):
            y_ref[bi] = y[0:tt].astype(y_ref.dtype)
    else:
        def body(i, carry):
            rows = pl.ds(pl.multiple_of(i * SPAT, SPAT), SPAT)
            for bi, y in enumerate(_interleave([chunk(bi, rows, SPAT) for bi in range(nb)])):
                y_ref[bi, rows, :] = y.astype(y_ref.dtype)
            return carry
        lax.fori_loop(0, tt // SPAT, body, 0)

    @pl.when(pl.program_id(1) == pl.num_programs(1) - 1)
    def _():
        for bi in range(nb):
            cout_ref[bi] = _blockdiag_to_heads(c_ref[bi])


def _mlstm(qkvo, gates, bias_i, bias_f, normg, c0, n0, m0, nb, tt):
    bsz, t, w4 = qkvo.shape
    w = w4 // 4
    dh = w // N_HEADS
    blk = lambda c: pl.BlockSpec((nb, tt, w), lambda b, j: (b, j, c))
    st = lambda shape: pl.BlockSpec((nb,) + shape, lambda b, j: (b, 0, 0))
    est = nb * (2 * 5 * tt * w * 4 + 4 * tt * LANE * 4 + 6 * w * w * 4 + 40 * SPAT * w * 4 + 24 * SPAT * SPAT * 4)
    return pl.pallas_call(
        _mlstm_kernel,
        out_shape=[jax.ShapeDtypeStruct((bsz, t, w), BF16),
                   jax.ShapeDtypeStruct((bsz, w, dh), F32),
                   jax.ShapeDtypeStruct((bsz, 1, w), F32),
                   jax.ShapeDtypeStruct((bsz, 1, w), F32)],
        grid=(bsz // nb, t // tt),
        in_specs=[blk(0), blk(1), blk(2), blk(3), pl.BlockSpec((nb, tt, GATE_W), lambda b, j: (b, j, 0)),
                  _const_spec((1, w)), _const_spec((1, w)), _const_spec((1, w)),
                  st((w, dh)), st((1, w)), st((1, w))],
        out_specs=[pl.BlockSpec((nb, tt, w), lambda b, j: (b, j, 0)), st((w, dh)), st((1, w)), st((1, w))],
        scratch_shapes=[pltpu.VMEM((nb, w, w), F32)],
        name="mlstm",
        compiler_params=_params(("parallel", "arbitrary"), est),
    )(qkvo, qkvo, qkvo, qkvo, gates, bias_i, bias_f, normg, c0, n0, m0)


def _ssd_chunk(xs, bm, cm, z, dt, a_x, dskip, normg, s_ref, bi, live=None):
    L, w = xs.shape
    dh = w // N_HEADS
    ng = s_ref.shape[1]
    gw = w // ng
    lmask = _head_lane_masks(L, w)
    tril = _tril(L)
    s_old = [s_ref[bi, g] for g in range(ng)]
    sb16 = [s.astype(BF16) for s in s_old]
    cmb = cm.astype(BF16)
    bmb = bm.astype(BF16)
    xdt = (xs * dt).astype(BF16)
    da_p = _bf16_pieces(dt * a_x, 3)
    yield
    cs = _dot_pieces_left(tril.astype(BF16), da_p)
    cb = [_dot_nt(cmb[:, g * gw:(g + 1) * gw], bmb[:, g * gw:(g + 1) * gw]) for g in range(ng)]
    y_in = jnp.concatenate([jnp.dot(cmb[:, g * gw:(g + 1) * gw], sb16[g], preferred_element_type=F32)
                            for g in range(ng)], axis=1)
    yield
    cs_t = cs.T
    cs_last = cs[L - 1:L, :]
    wl = jnp.exp(cs_last - cs) * dt
    wx = (xs * wl).astype(BF16)
    yield
    x_heads = jnp.concatenate([jnp.where(lmask[h], xdt, jnp.zeros_like(xdt)) for h in range(N_HEADS)], axis=0)
    mh = []
    for h in range(N_HEADS):
        dec = jnp.exp(jnp.where(tril, cs[:, h * dh:h * dh + 1] - cs_t[h * dh:h * dh + 1, :], NEG))
        mh.append((cb[h * ng // N_HEADS] * dec).astype(BF16))
        yield
    y = jnp.exp(cs) * y_in + jnp.dot(jnp.concatenate(mh, axis=1), x_heads, preferred_element_type=F32)
    yield
    dec_x = jnp.exp(cs_last)
    s_new = [dec_x[:, g * gw:(g + 1) * gw] * s_old[g] + _dot_tn(bmb[:, g * gw:(g + 1) * gw], wx[:, g * gw:(g + 1) * gw])
             for g in range(ng)]
    yield
    yc = y + dskip * xs
    out = _rmsnorm_rows(yc * (z * jax.nn.sigmoid(z)), normg)
    for g in range(ng):
        s_ref[bi, g] = s_new[g] if live is None else jnp.where(live, s_new[g], s_old[g])
    return out


def _ssd_kernel(xbc_ref, z_ref, gd_ref, cw_ref, cb_ref, dtb_ref, a_ref, dskip_ref, ng_ref, conv0_ref, s0_ref,
                y_ref, conv_ref, s_ref, ext_ref):
    nb, tt, w = z_ref.shape
    hist = CONV_C - 1
    off = SUBLANE

    @pl.when(pl.program_id(1) == 0)
    def _():
        s_ref[...] = s0_ref[...]
        ext_ref[:, off - hist:off, :] = conv0_ref[...]

    cw = cw_ref[...]
    for bi in range(nb):
        ext_ref[bi, off:off + tt, :] = xbc_ref[bi]
        acc = cb_ref[...] + cw[CONV_C - 1:CONV_C, :] * ext_ref[bi, off:off + tt, :]
        for j in range(CONV_C - 1):
            sh = CONV_C - 1 - j
            acc = acc + cw[j:j + 1, :] * ext_ref[bi, off - sh:off - sh + tt, :]
        tail = ext_ref[bi, off + tt - hist:off + tt, :]
        ext_ref[bi, off - hist:off, :] = tail
        conv_ref[bi] = tail
        ext_ref[bi, off:off + tt, :] = acc * jax.nn.sigmoid(acc)

    dtb = dtb_ref[...]
    a_x = -jnp.exp(a_ref[...])
    dskip = dskip_ref[...]
    ng = ng_ref[...]
    ex = _head_expander(w, GATE_DT)

    def chunk(bi, ext_rows, rows, n_valid):
        act = _pad_rows(ext_ref[bi, ext_rows, :], SPAT)
        dt = jax.nn.softplus(_split_dot(_pad_rows(gd_ref[bi, rows, :], SPAT), ex, 3) + dtb)
        if n_valid < SPAT:
            dt = jnp.where(_iota((SPAT, w), 0) < n_valid, dt, 0.0)
        return _ssd_chunk(act[:, 0:w], act[:, w:2 * w], act[:, 2 * w:3 * w], _pad_rows(z_ref[bi, rows, :], SPAT),
                          dt, a_x, dskip, ng, s_ref, bi)

    if tt < SPAT:
        gens = [chunk(bi, slice(off, off + tt), slice(None), tt) for bi in range(nb)]
        for bi, y in enumerate(_interleave(gens)):
            y_ref[bi] = y[0:tt].astype(y_ref.dtype)
    else:
        def body(i, carry):
            start = pl.multiple_of(i * SPAT, SPAT)
            ext_rows = pl.ds(pl.multiple_of(start + off, SUBLANE), SPAT)
            gens = [chunk(bi, ext_rows, pl.ds(start, SPAT), SPAT) for bi in range(nb)]
            for bi, y in enumerate(_interleave(gens)):
                y_ref[bi, pl.ds(start, SPAT), :] = y.astype(y_ref.dtype)
            return carry
        lax.fori_loop(0, tt // SPAT, body, 0)


def _ssd(cin, gates, conv_w, conv_b, dtb, a_log_x, dskip, normg, conv0, s0, nb, tt):
    bsz, t, wtot = cin.shape
    w = dskip.shape[1]
    xw = wtot - w
    ng, n_c, gp = s0.shape[1:]
    st = lambda shape: pl.BlockSpec((nb,) + shape, lambda b, j: (b,) + (0,) * len(shape))
    est = nb * (2 * 2 * tt * wtot * 4 + 2 * tt * w * 4 + (tt + SUBLANE) * xw * 4 + 3 * tt * xw * 4 + 40 * SPAT * w * 4)
    return pl.pallas_call(
        _ssd_kernel,
        out_shape=[jax.ShapeDtypeStruct((bsz, t, w), BF16),
                   jax.ShapeDtypeStruct((bsz, CONV_C - 1, xw), F32),
                   jax.ShapeDtypeStruct(s0.shape, F32)],
        grid=(bsz // nb, t // tt),
        in_specs=[pl.BlockSpec((nb, tt, xw), lambda b, j: (b, j, 0)),
                  pl.BlockSpec((nb, tt, w), lambda b, j: (b, j, xw // w)),
                  pl.BlockSpec((nb, tt, GATE_W), lambda b, j: (b, j, 0)),
                  _const_spec((CONV_C, xw)), _const_spec((1, xw)), _const_spec((1, w)), _const_spec((1, w)),
                  _const_spec((1, w)), _const_spec((1, w)),
                  st((CONV_C - 1, xw)), st((ng, n_c, gp))],
        out_specs=[pl.BlockSpec((nb, tt, w), lambda b, j: (b, j, 0)), st((CONV_C - 1, xw)), st((ng, n_c, gp))],
        scratch_shapes=[pltpu.VMEM((nb, tt + SUBLANE, xw), F32)],
        name="ssd",
        compiler_params=_params(("parallel", "arbitrary"), est),
    )(cin, cin, gates, conv_w, conv_b, dtb, a_log_x, dskip, normg, conv0, s0)


def _gmlp_kernel(u_ref, v_ref, ng_ref, ws_ref, bs_ref, y_ref, *vn_refs):
    tt, w = u_ref.shape[1], u_ref.shape[2]
    u = jax.nn.gelu(u_ref[0])
    vr = jax.nn.gelu(v_ref[0])
    mu = jnp.mean(vr, axis=-1, keepdims=True)
    xc = vr - mu
    vn = xc * lax.rsqrt(jnp.mean(xc * xc, axis=-1, keepdims=True) + EPS) * ng_ref[...]
    for vn_ref in vn_refs:
        vn_ref[0] = vn
    rows = _iota((N_HEADS * SPAT, SPAT), 0) % SPAT
    wst = jnp.where(rows >= _iota((N_HEADS * SPAT, SPAT), 1), ws_ref[...], 0.0).astype(BF16)
    lmask = _head_lane_masks(SPAT, w)
    bias = bs_ref[...]
    vpad = _pad_rows(vn, -(-tt // SPAT) * SPAT).astype(BF16)
    for i in range(vpad.shape[0] // SPAT):
        fs = jnp.dot(wst, vpad[i * SPAT:(i + 1) * SPAT], preferred_element_type=F32)
        f = bias
        for g in range(N_HEADS):
            f = f + jnp.where(lmask[g], fs[g * SPAT:(g + 1) * SPAT], 0.0)
        n = min(SPAT, tt - i * SPAT)
        y_ref[0, i * SPAT:i * SPAT + n, :] = (u[i * SPAT:i * SPAT + n] * f[0:n]).astype(y_ref.dtype)


def _gmlp(din, normg, w_s, b_x, tt, emit_v):
    bsz, t, w2 = din.shape
    w = w2 // 2
    est = 2 * 4 * tt * w * 4 + 8 * tt * w * 4 + 4 * N_HEADS * SPAT * (SPAT + w) * 4
    out_shape = [jax.ShapeDtypeStruct((bsz, t, w), BF16)]
    if emit_v:
        out_shape.append(jax.ShapeDtypeStruct((bsz, t, w), F32))
    return pl.pallas_call(
        _gmlp_kernel,
        out_shape=out_shape,
        grid=(bsz, t // tt),
        in_specs=[pl.BlockSpec((1, tt, w), lambda b, j: (b, j, 0)),
                  pl.BlockSpec((1, tt, w), lambda b, j: (b, j, 1)),
                  _const_spec((1, w)), _const_spec((N_HEADS * SPAT, SPAT)), _const_spec((SPAT, w))],
        out_specs=[pl.BlockSpec((1, tt, w), lambda b, j: (b, j, 0))] * len(out_shape),
        name="spatial_gate",
        compiler_params=_params(("parallel", "parallel"), est),
    )(din, din, normg, w_s, b_x)


def _post_kernel(x_ref, ya_ref, yb_ref, yc_ref, yd_ref, mod_ref, g_ref, wo_ref, wg_ref, wu_ref, wd_ref,
                 cw_ref, cb_ref, f0_ref, gfin_ref, o_ref, fc_ref, ext_ref, *, final, fchunk):
    bb, tt, d = x_ref.shape
    m = bb * tt
    gw = ya_ref.shape[2]
    f = wg_ref.shape[1]
    hist = CONV_F - 1
    off = SUBLANE
    mod = mod_ref[...]
    mix = None
    for j, y_ref in enumerate((ya_ref, yb_ref, yc_ref, yd_ref)):
        p = jnp.dot(y_ref[...].reshape(m, gw), wo_ref[j * gw:(j + 1) * gw, :], preferred_element_type=F32)
        mix = p if mix is None else mix + p
    x1 = x_ref[...] + mod[:, :, 2 * d:3 * d] * mix.reshape(bb, tt, d)
    h2 = _rmsnorm_rows(x1, g_ref[...]) * (1.0 + mod[:, :, 4 * d:5 * d]) + mod[:, :, 3 * d:4 * d]
    hb = h2.reshape(m, d).astype(BF16)

    @pl.when(pl.program_id(1) == 0)
    def _():
        ext_ref[:, off - hist:off, :] = f0_ref[...]

    down = None
    for c0 in range(0, f, fchunk):
        cols = slice(c0, c0 + fchunk)
        g = jnp.dot(hb, wg_ref[:, cols], preferred_element_type=F32).reshape(bb, tt, fchunk)
        u = jnp.dot(hb, wu_ref[:, cols], preferred_element_type=F32)
        ext_ref[:, off:off + tt, cols] = g
        acc = cb_ref[:, cols] + cw_ref[CONV_F - 1:CONV_F, cols] * g
        for j in range(CONV_F - 1):
            sh = CONV_F - 1 - j
            acc = acc + cw_ref[j:j + 1, cols] * ext_ref[:, off - sh:off - sh + tt, cols]
        act = (acc * jax.nn.sigmoid(acc)).reshape(m, fchunk) * u
        p = jnp.dot(act.astype(BF16), wd_ref[cols, :], preferred_element_type=F32)
        down = p if down is None else down + p
    tail = ext_ref[:, off + tt - hist:off + tt, :]
    ext_ref[:, off - hist:off, :] = tail
    fc_ref[...] = tail
    x2 = x1 + mod[:, :, 5 * d:6 * d] * down.reshape(bb, tt, d)
    if final:
        x2 = _rmsnorm_rows(x2, gfin_ref[...])
    o_ref[...] = x2


def _post(x, ys, mod, g_ffn, w_out, w_gate, w_up, w_down, conv_w, conv_b, f0, g_final, bb, tt, final):
    bsz, t, d = x.shape
    gw = ys[0].shape[2]
    f = w_gate.shape[1]
    m = bb * tt
    tok = lambda wd: pl.BlockSpec((bb, tt, wd), lambda i, j: (i, j, 0))
    fchunk = 2 * LANE
    assert f % fchunk == 0
    est = (2 * 2 * m * d * 4 + 2 * 4 * m * gw * 2 + (d * d + 3 * d * f) * 2 + bb * (tt + SUBLANE) * f * 4
           + 8 * m * fchunk * 4 + 6 * m * d * 4)
    return pl.pallas_call(
        functools.partial(_post_kernel, final=final, fchunk=fchunk),
        out_shape=[jax.ShapeDtypeStruct((bsz, t, d), F32), jax.ShapeDtypeStruct((bsz, CONV_F - 1, f), F32)],
        grid=(bsz // bb, t // tt),
        in_specs=[tok(d), tok(gw), tok(gw), tok(gw), tok(gw),
                  pl.BlockSpec((bb, 1, mod.shape[2]), lambda i, j: (i, 0, 0)),
                  _const_spec((1, 1, d)), _const_spec((d, d)), _const_spec((d, f)), _const_spec((d, f)),
                  _const_spec((f, d)), _const_spec((CONV_F, f)), _const_spec((1, f)),
                  pl.BlockSpec((bb, CONV_F - 1, f), lambda i, j: (i, 0, 0)),
                  _const_spec((1, 1, d))],
        out_specs=[tok(d), pl.BlockSpec((bb, CONV_F - 1, f), lambda i, j: (i, 0, 0))],
        scratch_shapes=[pltpu.VMEM((bb, tt + SUBLANE, f), F32)],
        name="post_ffn",
        compiler_params=_params(("parallel", "arbitrary"), est),
    )(x, *ys, mod, g_ffn.reshape(1, 1, d), w_out, w_gate, w_up, w_down, conv_w, conv_b, f0, g_final.reshape(1, 1, d))


def _mixpost_kernel(qkv_ref, qkvo_ref, cin_ref, din_ref, gate_ref, x_ref, mod_ref, *refs, nt, final, fchunk, n_hist):
    hist_refs = refs[:2 * n_hist]
    (bias_ref, bi_ref, bf_ref, bng_ref, ccw_ref, ccb_ref, dtb_ref, alog_ref, dskip_ref, cng_ref,
     dng_ref, dws_ref, dbx_ref, gffn_ref, wo_ref, wg_ref, wu_ref, wd_ref, fcw_ref, fcb_ref, gfin_ref,
     o_ref, cout_ref, n_ref, m_ref, conv_ref, s_ref, fc_ref,
     kk_ref, vv_ref, ybuf_ref, cext_ref, fext_ref, fhist_ref, c_ref) = refs[2 * n_hist:]
    tt, d = x_ref.shape[1], x_ref.shape[2]
    w = d // 4
    f = wg_ref.shape[1]
    band = (N_BAND + 1) * CHUNK
    off = SUBLANE
    s = pl.program_id(0)
    n_tiles = pl.num_programs(0) - 1
    live = s < n_tiles
    mj = jnp.minimum(s, n_tiles - 1) % nt
    pj = jnp.maximum(s - 1, 0) % nt
    wslot = s % 2
    rslot = 1 - wslot

    @pl.when(s == 0)
    def _():
        ybuf_ref[...] = jnp.zeros(ybuf_ref.shape, ybuf_ref.dtype)

    @pl.when(mj == 0)
    def _():
        c_ref[...] = jnp.zeros(c_ref.shape, F32)
        n_ref[...] = jnp.zeros(n_ref.shape, F32)
        m_ref[...] = jnp.zeros(m_ref.shape, F32)
        s_ref[...] = jnp.zeros(s_ref.shape, F32)
        cext_ref[0:off, :] = jnp.zeros((off, cext_ref.shape[1]), F32)

    @pl.when(pj == 0)
    def _():
        fhist_ref[...] = jnp.zeros(fhist_ref.shape, F32)

    ex_i, ex_f, ex_dt = (_head_expander(w, first) for first in (GATE_I, GATE_F, GATE_DT))

    def attention():
        scale = (w // N_HEADS) ** -0.5
        for i, (kh_ref, vh_ref) in enumerate(zip(hist_refs[0::2], hist_refs[1::2])):
            kk_ref[i * tt:(i + 1) * tt, :] = kh_ref[0].astype(BF16)
            vv_ref[i * tt:(i + 1) * tt, :] = vh_ref[0].astype(BF16)
        kk_ref[A_WIN:A_WIN + tt, :] = qkv_ref[0, :, w:2 * w].astype(BF16)
        vv_ref[A_WIN:A_WIN + tt, :] = qkv_ref[0, :, 2 * w:3 * w].astype(BF16)
        masks = _head_lane_masks(CHUNK, w)
        bias = bias_ref[...]
        slot = _iota((1, band), 1)
        q_all = qkv_ref[0, :, 0:w].astype(F32) * scale
        yield
        for i in range(tt // CHUNK):
            base = i * CHUNK
            pos = slot + base
            kvalid = pos >= A_WIN
            for back in range(1, A_WIN // tt + 1):
                kvalid = kvalid | ((pos >= A_WIN - back * tt) & (mj >= back))
            out = yield from _attend_chunk(q_all[base:base + CHUNK], kk_ref[base:base + band, :],
                                           vv_ref[base:base + band, :], bias, kvalid, masks, masks)
            ybuf_ref[wslot, base:base + CHUNK, 0:w] = out.astype(BF16)
            yield

    def mlstm():
        kscale = (w // N_HEADS) ** -0.5
        bias_i, bias_f, ng = bi_ref[...], bf_ref[...], bng_ref[...]
        for c in range(tt // SPAT):
            rows = slice(c * SPAT, (c + 1) * SPAT)
            col = lambda c: qkvo_ref[0, rows, c * w:(c + 1) * w].astype(F32)
            g_p = _bf16_pieces(gate_ref[0, rows, :], GATE_PIECES)
            yield
            y = yield from _mlstm_chunk(
                col(0), col(1) * kscale, col(2), col(3),
                _dot_pieces(g_p, ex_i) + bias_i, _dot_pieces(g_p, ex_f) + bias_f,
                ng, c_ref, n_ref, m_ref, 0, SPAT, live)
            ybuf_ref[wslot, rows, w:2 * w] = y.astype(BF16)
            yield

    def ssd():
        xw = cext_ref.shape[1]
        hist = CONV_C - 1
        cext_ref[off:off + tt, :] = cin_ref[0, :, 0:xw].astype(F32)
        acc = ccb_ref[...] + ccw_ref[CONV_C - 1:CONV_C, :] * cext_ref[off:off + tt, :]
        for j in range(CONV_C - 1):
            sh = CONV_C - 1 - j
            acc = acc + ccw_ref[j:j + 1, :] * cext_ref[off - sh:off - sh + tt, :]
            yield
        tail = cext_ref[off + tt - hist:off + tt, :]
        cext_ref[off - hist:off, :] = tail
        conv_ref[0] = tail
        cext_ref[off:off + tt, :] = acc * jax.nn.sigmoid(acc)
        yield
        a_x = -jnp.exp(alog_ref[...])
        dtb, dskip, ng = dtb_ref[...], dskip_ref[...], cng_ref[...]
        for c in range(tt // SPAT):
            rows = slice(c * SPAT, (c + 1) * SPAT)
            erows = slice(off + c * SPAT, off + (c + 1) * SPAT)
            dt_p = _bf16_pieces(gate_ref[0, rows, :], GATE_PIECES)
            yield
            dt = jax.nn.softplus(_dot_pieces(dt_p, ex_dt) + dtb)
            y = yield from _ssd_chunk(cext_ref[erows, 0:w], cext_ref[erows, w:2 * w], cext_ref[erows, 2 * w:3 * w],
                                      cin_ref[0, rows, xw:xw + w].astype(F32), dt, a_x, dskip, ng, s_ref, 0, live)
            ybuf_ref[wslot, rows, 2 * w:3 * w] = y.astype(BF16)
            yield

    def gmlp():
        u = jax.nn.gelu(din_ref[0, :, 0:w].astype(F32))
        vr = jax.nn.gelu(din_ref[0, :, w:2 * w].astype(F32))
        yield
        xc = vr - jnp.mean(vr, axis=-1, keepdims=True)
        vn = (xc * lax.rsqrt(jnp.mean(xc * xc, axis=-1, keepdims=True) + EPS) * dng_ref[...]).astype(BF16)
        rows = _iota((N_HEADS * SPAT, SPAT), 0) % SPAT
        wst = jnp.where(rows >= _iota((N_HEADS * SPAT, SPAT), 1), dws_ref[...], 0.0).astype(BF16)
        wcat = jnp.concatenate([wst[g * SPAT:(g + 1) * SPAT] for g in range(N_HEADS)], axis=1)
        lmask = _head_lane_masks(SPAT, w)
        zero = jnp.zeros((SPAT, w), BF16)
        yield
        for i in range(tt // SPAT):
            vi = vn[i * SPAT:(i + 1) * SPAT]
            v_groups = jnp.concatenate([jnp.where(lmask[g], vi, zero) for g in range(N_HEADS)], axis=0)
            yield
            fgate = dbx_ref[...] + jnp.dot(wcat, v_groups, preferred_element_type=F32)
            yield
            ybuf_ref[wslot, i * SPAT:(i + 1) * SPAT, 3 * w:4 * w] = (u[i * SPAT:(i + 1) * SPAT] * fgate).astype(BF16)
            yield

    def post():
        hist = CONV_F - 1
        mod = mod_ref[0]
        mix = jnp.dot(ybuf_ref[rslot], wo_ref[...], preferred_element_type=F32)
        yield
        x1 = x_ref[0] + mod[:, 2 * d:3 * d] * mix
        h2 = _rmsnorm_rows(x1, gffn_ref[...]) * (1.0 + mod[:, 4 * d:5 * d]) + mod[:, 3 * d:4 * d]
        hb = h2.astype(BF16)
        yield
        down = None
        act_b16, act_cols = None, None
        for ci, c0 in enumerate(range(0, f, fchunk)):
            cols = slice(c0, c0 + fchunk)
            g = jnp.dot(hb, wg_ref[:, cols], preferred_element_type=F32)
            u = jnp.dot(hb, wu_ref[:, cols], preferred_element_type=F32)
            if act_b16 is not None:
                p = jnp.dot(act_b16, wd_ref[act_cols, :], preferred_element_type=F32)
                down = p if down is None else down + p
            yield
            buf = ci % 2
            fext_ref[buf, off - hist:off, :] = fhist_ref[0, off - hist:off, cols]
            fext_ref[buf, off:off + tt, :] = g
            acc = fcb_ref[:, cols] + fcw_ref[CONV_F - 1:CONV_F, cols] * g
            for j in range(CONV_F - 1):
                sh = CONV_F - 1 - j
                acc = acc + fcw_ref[j:j + 1, cols] * fext_ref[buf, off - sh:off - sh + tt, :]
            fhist_ref[0, off - hist:off, cols] = g[tt - hist:tt]
            act_b16, act_cols = ((acc * jax.nn.sigmoid(acc)) * u).astype(BF16), cols
            yield
        down = down + jnp.dot(act_b16, wd_ref[act_cols, :], preferred_element_type=F32)
        fc_ref[...] = fhist_ref[:, off - hist:off, :]
        x2 = x1 + mod[:, 5 * d:6 * d] * down
        if final:
            x2 = _rmsnorm_rows(x2, gfin_ref[...])
        o_ref[0] = x2

    _interleave([post(), mlstm(), ssd(), attention(), gmlp()])

    @pl.when(mj == nt - 1)
    def _():
        cout_ref[0] = _blockdiag_to_heads(c_ref[0])


def _mix_post(qkv_a, qkvo_b, c_in, d_in, gates, x, mod, p, g_final, tt, final):
    bsz, t, d = x.shape
    w = d // 4
    nt = t // tt
    n_tiles = bsz * nt
    f = p["f_w_gate"].shape[1]
    xw = c_in.shape[2] - w
    ng, n_c, gp = p["ssm_shape"]
    band = (N_BAND + 1) * CHUNK
    fchunk = 2 * LANE
    assert f % fchunk == 0 and tt % SPAT == 0 and t % tt == 0
    mix_tile = lambda width: pl.BlockSpec(
        (1, tt, width), lambda s: (jnp.minimum(s, n_tiles - 1) // nt, jnp.minimum(s, n_tiles - 1) % nt, 0))
    post_tile = lambda width: pl.BlockSpec(
        (1, tt, width), lambda s: (jnp.maximum(s - 1, 0) // nt, jnp.maximum(s - 1, 0) % nt, 0))
    mix_state = lambda shape: pl.BlockSpec(
        (1,) + shape, lambda s: (jnp.minimum(s, n_tiles - 1) // nt,) + (0,) * len(shape))
    post_state = lambda shape: pl.BlockSpec(
        (1,) + shape, lambda s: (jnp.maximum(s - 1, 0) // nt,) + (0,) * len(shape))
    consts = [p["bias_tab"].reshape(N_HEADS * CHUNK, band), p["bi"], p["bf"], p["b_norm_g"], p["c_conv_w"],
              p["c_conv_b"], p["dtb"], p["a_log_x"], p["dskip"], p["c_norm_g"], p["d_norm_g"], p["d_w_s"], p["d_b_x"],
              p["g_ffn"].reshape(1, d), p["w_out"], p["f_w_gate"], p["f_w_up"], p["f_w_down"], p["f_conv_w"],
              p["f_conv_b"], g_final.reshape(1, d)]
    in_widths = (qkv_a.shape[2], qkvo_b.shape[2], c_in.shape[2], d_in.shape[2], gates.shape[2])
    n_hist = A_WIN // tt
    assert n_hist * tt == A_WIN

    def hist_tile(back, col):
        def index(s):
            m = jnp.minimum(s, n_tiles - 1)
            return (m // nt, jnp.maximum(m % nt - back, 0), col)
        return pl.BlockSpec((1, tt, w), index)

    hist_specs = [hist_tile(back, col) for back in range(n_hist, 0, -1) for col in (1, 2)]
    in_bytes = sum(a.shape[2] * a.dtype.itemsize for a in (qkv_a, qkvo_b, c_in, d_in, gates))
    est = ((d * d + 3 * d * f) * 2 + 2 * tt * in_bytes + 4 * tt * d * 4 + (tt + SUBLANE) * (2 * fchunk + xw) * 4
           + 2 * (A_WIN + tt) * w * 2 + 8 * tt * w * 2 + 12 * tt * d * 4 + 6 * w * w * 4
           + 4 * n_hist * tt * w * qkv_a.dtype.itemsize)
    return pl.pallas_call(
        functools.partial(_mixpost_kernel, nt=nt, final=final, fchunk=fchunk, n_hist=n_hist),
        out_shape=[jax.ShapeDtypeStruct((bsz, t, d), F32),
                   jax.ShapeDtypeStruct((bsz, w, w // N_HEADS), F32),
                   jax.ShapeDtypeStruct((bsz, 1, w), F32),
                   jax.ShapeDtypeStruct((bsz, 1, w), F32),
                   jax.ShapeDtypeStruct((bsz, CONV_C - 1, xw), F32),
                   jax.ShapeDtypeStruct((bsz, ng, n_c, gp), F32),
                   jax.ShapeDtypeStruct((bsz, CONV_F - 1, f), F32)],
        grid=(n_tiles + 1,),
        in_specs=[mix_tile(wd) for wd in in_widths] + [post_tile(d), post_state((1, mod.shape[2]))] + hist_specs
                 + [_const_spec(c.shape) for c in consts],
        out_specs=[post_tile(d), mix_state((w, w // N_HEADS)), mix_state((1, w)), mix_state((1, w)),
                   mix_state((CONV_C - 1, xw)), mix_state((ng, n_c, gp)), post_state((CONV_F - 1, f))],
        scratch_shapes=[pltpu.VMEM((A_WIN + tt, w), BF16), pltpu.VMEM((A_WIN + tt, w), BF16),
                        pltpu.VMEM((2, tt, d), BF16), pltpu.VMEM((tt + SUBLANE, xw), F32),
                        pltpu.VMEM((2, tt + SUBLANE, fchunk), F32), pltpu.VMEM((1, SUBLANE, f), F32),
                        pltpu.VMEM((1, w, w), F32)],
        name="mix_post",
        compiler_params=_params(("arbitrary",), est),
    )(qkv_a, qkvo_b, c_in, d_in, gates, x, mod, *([qkv_a] * len(hist_specs)), *consts)


def _layer(x, mod, st, p, tiles, final, g_final):
    bsz, t, d = x.shape
    gw = d // 4
    dh = gw // N_HEADS
    bb, tt, tmix, nb = tiles
    mod3 = mod.reshape(bsz, 1, mod.shape[1])
    dtypes = (BF16,) * 4 + (F32,) if st is None else (F32,) * 5
    qkv_a, qkvo_b, c_in, d_in, gates = _in_proj(x, mod3, p["g_mix"], p["w_in"], p["widths"], dtypes, bb, tt)

    if st is None:
        x_new, c_new, n_new, m_new, conv_new, ssm_new, fconv_new = _mix_post(
            qkv_a, qkvo_b, c_in, d_in, gates, x, mod3, p, g_final, tmix, final)
        vn = []
    else:
        y_a = _attention(qkv_a, st["a_k"], st["a_v"], p["bias_tab"], tmix)
        y_b, c_new, n_new, m_new = _mlstm(qkvo_b, gates, p["bi"], p["bf"], p["b_norm_g"], st["b_c"], st["b_n"],
                                          st["b_m"], nb, tmix)
        y_c, conv_new, ssm_new = _ssd(c_in, gates, p["c_conv_w"], p["c_conv_b"], p["dtb"], p["a_log_x"], p["dskip"],
                                      p["c_norm_g"], st["c_conv"], st["c_ssm"], nb, tmix)
        y_d, *vn = _gmlp(d_in, p["d_norm_g"], p["d_w_s"], p["d_b_x"], tmix, True)
        x_new, fconv_new = _post(x, (y_a, y_b, y_c, y_d), mod3, p["g_ffn"], p["w_out"], p["f_w_gate"], p["f_w_up"],
                                 p["f_w_down"], p["f_conv_w"], p["f_conv_b"], st["f_conv"], g_final, bb, tt, final)

    keep = min(A_WIN, t)
    new_k = qkv_a[:, t - keep:, gw:2 * gw].reshape(bsz, keep, N_HEADS, dh).astype(F32)
    new_v = qkv_a[:, t - keep:, 2 * gw:3 * gw].reshape(bsz, keep, N_HEADS, dh).astype(F32)
    c_heads = c_new.reshape(bsz, N_HEADS, dh, dh)
    n_heads = n_new.reshape(bsz, N_HEADS, dh)
    m_heads = m_new[:, 0, ::dh]
    ssm = jnp.swapaxes(ssm_new, 2, 3).reshape(bsz, N_HEADS, gw // N_HEADS, ssm_new.shape[2])
    outs = (new_k, new_v, c_heads, n_heads, m_heads, ssm, conv_new, fconv_new, *vn)
    return x_new, outs


def _prep_layer(l, w_in, w_out, a_rel_bias, b_i_bias, b_f_bias, b_norm_g, c_conv_w, c_conv_b, c_dt_bias, c_a_log,
                c_d_skip, c_norm_g, d_norm_g, d_w_s, d_b_s, f_w_gate, f_w_up, f_conv_w, f_conv_b, f_w_down,
                g_norm_mix, g_norm_ffn):
    d = w_in.shape[1]
    gw = d // 4
    nh = b_i_bias.shape[1]
    xbc_w = c_conv_w.shape[2]
    sizes = (gw,) * 7 + (nh, nh, gw, xbc_w, c_dt_bias.shape[1], gw, gw)
    offs = [0]
    for s in sizes:
        offs.append(offs[-1] + s)
    moves, dst = [], 0
    for i in (0, 1, 2, 3, 4, 5, 6, 10, 9, 12, 13):
        moves.append((offs[i], dst, sizes[i]))
        dst += sizes[i]
    assert sizes[7] == sizes[8] == sizes[11] == N_HEADS
    packed = (dst, ((offs[7], GATE_I, N_HEADS), (offs[8], GATE_F, N_HEADS), (offs[11], GATE_DT, N_HEADS)))
    widths = (3 * gw, 4 * gw, xbc_w + gw, 2 * gw, GATE_W)
    dst += GATE_W
    assert dst == sum(widths)
    heads_x = lambda v: jnp.repeat(v.astype(F32), gw // nh)[None, :]
    return dict(
        w_in=_regroup_cast(w_in, l, moves, packed, dst), widths=widths, g_mix=g_norm_mix[l], g_ffn=g_norm_ffn[l],
        w_out=w_out[l].astype(BF16), bias_tab=_bias_table(a_rel_bias[l]),
        bi=heads_x(b_i_bias[l]), bf=heads_x(b_f_bias[l]), b_norm_g=b_norm_g[l][None, :],
        c_conv_w=c_conv_w[l], c_conv_b=c_conv_b[l][None, :], dtb=heads_x(c_dt_bias[l]),
        a_log_x=heads_x(c_a_log[l]), dskip=heads_x(c_d_skip[l]), c_norm_g=c_norm_g[l][None, :],
        d_norm_g=d_norm_g[l][None, :], d_w_s=d_w_s[l].reshape(-1, d_w_s.shape[-1]),
        d_b_x=jnp.repeat(d_b_s[l].T, gw // d_b_s.shape[1], axis=1),
        f_w_gate=f_w_gate[l].astype(BF16), f_w_up=f_w_up[l].astype(BF16), f_w_down=f_w_down[l].astype(BF16),
        f_conv_w=f_conv_w[l], f_conv_b=f_conv_b[l][None, :],
    )


def kernel(x_prompt, x_sample, c_prompt, c_sample, cache_a_k, cache_a_v, state_b_c, state_b_n, state_b_m, state_c_ssm, state_c_conv, state_ffn_conv, w_ada, b_ada, g_norm_mix, g_norm_ffn, w_in, w_out, a_rel_bias, b_i_bias, b_f_bias, b_norm_g, c_conv_w, c_conv_b, c_dt_bias, c_a_log, c_d_skip, c_norm_g, d_norm_g, d_w_s, d_b_s, f_w_gate, f_w_up, f_conv_w, f_conv_b, f_w_down, g_final):
    depth = w_in.shape[0]
    bp, tp, d = x_prompt.shape
    bs, ts, _ = x_sample.shape
    gw = d // 4
    dh = gw // N_HEADS
    xbc_w = c_conv_w.shape[2]
    n_c = state_c_ssm.shape[-1]
    g_c = (xbc_w - gw) // 2 // n_c

    mod_all = _ada(jnp.concatenate([c_prompt, c_sample], axis=0), w_ada, b_ada)

    nb = max(n for n in (8, 4, 2, 1) if bs % n == 0)
    tiles_p = (1, min(A_WIN, tp), min(A_WIN, tp), 1)
    tiles_s = (bs, ts, ts, nb)

    xp, xs = x_prompt, x_sample
    p_states, s_states = [], []
    for l in range(depth):
        p = _prep_layer(l, w_in, w_out, a_rel_bias, b_i_bias, b_f_bias, b_norm_g, c_conv_w, c_conv_b, c_dt_bias,
                        c_a_log, c_d_skip, c_norm_g, d_norm_g, d_w_s, d_b_s, f_w_gate, f_w_up, f_conv_w, f_conv_b,
                        f_w_down, g_norm_mix, g_norm_ffn)
        p["ssm_shape"] = (g_c, n_c, gw // g_c)
        final = l == depth - 1
        xp, sp = _layer(xp, mod_all[l, :bp], None, p, tiles_p, final, g_final)
        st_s = dict(a_k=cache_a_k[l].reshape(bs, -1, gw), a_v=cache_a_v[l].reshape(bs, -1, gw),
                    b_c=state_b_c[l].reshape(bs, gw, dh), b_n=state_b_n[l].reshape(bs, 1, gw),
                    b_m=jnp.repeat(state_b_m[l], dh, axis=1)[:, None, :],
                    c_conv=state_c_conv[l],
                    c_ssm=jnp.swapaxes(state_c_ssm[l].reshape(bs, g_c, gw // g_c, n_c), 2, 3),
                    f_conv=state_ffn_conv[l])
        xs, ss = _layer(xs, mod_all[l, bp:], st_s, p, tiles_s, final, g_final)
        p_states.append(sp)
        s_states.append(ss)

    stack = lambda states, i: jnp.stack([s[i] for s in states])
    return (xp, xs,
            *(stack(p_states, i) for i in range(8)),
            *(stack(s_states, i) for i in range(9)))
```

```python
import functools

import jax
import jax.numpy as jnp
from jax import lax
from jax.experimental import pallas as pl
from jax.experimental.pallas import tpu as pltpu

F32 = jnp.float32
BF16 = jnp.bfloat16

EPS = 1e-6
NEG = -1e30

CHUNK = 64
N_BAND = 8
A_WIN = N_BAND * CHUNK
REL_CLIP = 128
N_HEADS = 4
SPAT = 128
CONV_C = 4
CONV_F = 3
LANE = 128
GATE_I, GATE_F, GATE_DT = 0, 4, 8
GATE_W = LANE
GATE_PIECES = 2
SUBLANE = 8
VMEM_CAP = 64 * 1024 * 1024


def _vmem_limit(nbytes):
    return int(min(max(nbytes, 16 * 1024 * 1024), VMEM_CAP - 8 * 1024 * 1024))


def _params(sem, nbytes):
    return pltpu.CompilerParams(dimension_semantics=sem, vmem_limit_bytes=_vmem_limit(nbytes))


def _const_spec(shape):
    nd = len(shape)
    return pl.BlockSpec(shape, lambda *_: (0,) * nd, pipeline_mode=pl.Buffered(1))


def _iota(shape, dim):
    return lax.broadcasted_iota(jnp.int32, shape, dim)


def _bf16_pieces(x, parts):
    out = []
    r = x
    for i in range(parts):
        hi = r.astype(BF16)
        out.append(hi)
        if i + 1 < parts:
            r = r - hi.astype(F32)
    return out


def _dot_pieces(pieces, e):
    acc = None
    for piece in pieces:
        d = jnp.dot(piece, e, preferred_element_type=F32)
        acc = d if acc is None else acc + d
    return acc


def _dot_pieces_left(e, pieces):
    acc = None
    for piece in pieces:
        d = jnp.dot(e, piece, preferred_element_type=F32)
        acc = d if acc is None else acc + d
    return acc


def _split_dot(x, e, parts):
    return _dot_pieces(_bf16_pieces(x, parts), e)


def _dot_nt(a, b):
    return lax.dot_general(a, b, (((1,), (1,)), ((), ())), preferred_element_type=F32)


def _dot_tn(a, b):
    return lax.dot_general(a, b, (((0,), (0,)), ((), ())), preferred_element_type=F32)


def _head_expander(width, first):
    dh = width // N_HEADS
    return (_iota((LANE, width), 1) // dh == _iota((LANE, width), 0) - first).astype(BF16)


def _head_lane_masks(rows, width):
    dh = width // N_HEADS
    lane = _iota((rows, width), 1)
    return [(lane >= h * dh) & (lane < (h + 1) * dh) for h in range(N_HEADS)]


def _tril(n):
    return _iota((n, n), 0) >= _iota((n, n), 1)


def _pad_rows(x, rows):
    if x.shape[0] == rows:
        return x
    return jnp.concatenate([x, jnp.zeros((rows - x.shape[0], x.shape[1]), x.dtype)], axis=0)


def _interleave(gens):
    results = [None] * len(gens)
    live = list(range(len(gens)))
    while live:
        for i in list(live):
            try:
                next(gens[i])
            except StopIteration as stop:
                results[i] = stop.value
                live.remove(i)
    return results


def _rmsnorm_rows(x, g):
    return x * lax.rsqrt(jnp.mean(x * x, axis=-1, keepdims=True) + EPS) * g


def _regroup_kernel(w_ref, o_ref, *, moves, packed):
    w = w_ref[0]
    rows = w.shape[0]
    for src, dst, n in moves:
        o_ref[:, dst:dst + n] = w[:, src:src + n].astype(o_ref.dtype)
    if packed is None:
        return
    dst, pieces = packed
    parts, lane = [], 0
    for src, first, n in pieces:
        if first > lane:
            parts.append(jnp.zeros((rows, first - lane), F32))
        parts.append(w[:, src:src + n])
        lane = first + n
    parts.append(jnp.zeros((rows, GATE_W - lane), F32))
    o_ref[:, dst:dst + GATE_W] = jnp.concatenate(parts, axis=1).astype(o_ref.dtype)


def _regroup_cast(w_all, layer, moves, packed, ncols, row_block=256):
    _, r, c = w_all.shape
    return pl.pallas_call(
        functools.partial(_regroup_kernel, moves=tuple(moves), packed=packed),
        out_shape=jax.ShapeDtypeStruct((r, ncols), BF16),
        grid=(r // row_block,),
        in_specs=[pl.BlockSpec((1, row_block, c), lambda i: (layer, i, 0))],
        out_specs=pl.BlockSpec((row_block, ncols), lambda i: (i, 0)),
        name="regroup_cast",
        compiler_params=_params(("parallel",), 2 * row_block * (c * 4 + ncols * 2) + row_block * c * 4),
    )(w_all)


def _layer_bf16(w_all, layer):
    c = w_all.shape[2]
    return _regroup_cast(w_all, layer, [(0, 0, c)], None, c)


def _ada_kernel(c_ref, w_ref, b_ref, o_ref):
    c = c_ref[...]
    h = (c * jax.nn.sigmoid(c)).astype(BF16)
    o_ref[0] = jnp.dot(h, w_ref[0].astype(BF16), preferred_element_type=F32) + b_ref[0]


def _ada(c_all, w_ada, b_ada):
    depth, d, n6 = w_ada.shape
    r = c_all.shape[0]
    tn = d
    return pl.pallas_call(
        _ada_kernel,
        out_shape=jax.ShapeDtypeStruct((depth, r, n6), F32),
        grid=(depth, n6 // tn),
        in_specs=[pl.BlockSpec((r, d), lambda l, j: (0, 0)),
                  pl.BlockSpec((1, d, tn), lambda l, j: (l, 0, j)),
                  pl.BlockSpec((1, 1, tn), lambda l, j: (l, 0, j))],
        out_specs=pl.BlockSpec((1, r, tn), lambda l, j: (l, 0, j)),
        name="ada_mod",
        compiler_params=_params(("parallel", "parallel"), 4 * (2 * d * tn * 4 + 2 * r * tn * 4 + r * d * 4)),
    )(c_all, w_ada, b_ada.reshape(depth, 1, n6))


def _bias_kernel(rb_ref, o_ref, *, lo, hi):
    nh, lq, lk = o_ref.shape
    idx = jnp.clip(A_WIN + _iota((lq, lk), 0) - _iota((lq, lk), 1), -REL_CLIP, REL_CLIP) + REL_CLIP
    for h in range(nh):
        def body(r, acc, h=h):
            return jnp.where(idx == r, rb_ref[h, r], acc)
        o_ref[h] = lax.fori_loop(lo, hi + 1, body, jnp.zeros((lq, lk), F32))


def _bias_table(rel_bias):
    nh = rel_bias.shape[0]
    band = (N_BAND + 1) * CHUNK
    lo = max(A_WIN - (band - 1), -REL_CLIP) + REL_CLIP
    hi = min(A_WIN + CHUNK - 1, REL_CLIP) + REL_CLIP
    return pl.pallas_call(
        functools.partial(_bias_kernel, lo=lo, hi=hi),
        out_shape=jax.ShapeDtypeStruct((nh, CHUNK, band), F32),
        in_specs=[pl.BlockSpec(memory_space=pltpu.SMEM)],
        out_specs=pl.BlockSpec(memory_space=pltpu.VMEM),
        name="rel_bias_table",
    )(rel_bias)


def _in_kernel(x_ref, mod_ref, g_ref, w_ref, *o_refs, col_starts):
    bb, tt, d = x_ref.shape
    x = x_ref[...]
    mod = mod_ref[...]
    h = _rmsnorm_rows(x, g_ref[...]) * (1.0 + mod[:, :, d:2 * d]) + mod[:, :, 0:d]
    hb = h.reshape(bb * tt, d).astype(BF16)
    for o_ref, (a, b) in zip(o_refs, col_starts):
        o_ref[...] = jnp.dot(hb, w_ref[:, a:b], preferred_element_type=F32).reshape(bb, tt, b - a).astype(o_ref.dtype)


def _in_proj(x, mod, g, w, widths, dtypes, bb, tt):
    bsz, t, d = x.shape
    ncols = w.shape[1]
    starts, a = [], 0
    for wd in widths:
        starts.append((a, a + wd))
        a += wd
    m = bb * tt
    est = 2 * m * d * 4 + 2 * d * ncols * 2 + 2 * m * ncols * 4 + 3 * m * d * 4 + m * max(widths) * 4
    return pl.pallas_call(
        functools.partial(_in_kernel, col_starts=tuple(starts)),
        out_shape=[jax.ShapeDtypeStruct((bsz, t, wd), dt) for wd, dt in zip(widths, dtypes)],
        grid=(bsz // bb, t // tt),
        in_specs=[pl.BlockSpec((bb, tt, d), lambda i, j: (i, j, 0)),
                  pl.BlockSpec((bb, 1, mod.shape[2]), lambda i, j: (i, 0, 0)),
                  _const_spec((1, 1, d)),
                  _const_spec((d, ncols))],
        out_specs=[pl.BlockSpec((bb, tt, wd), lambda i, j: (i, j, 0)) for wd in widths],
        name="in_proj",
        compiler_params=_params(("parallel", "parallel"), est),
    )(x, mod, g.reshape(1, 1, d), w)


def _attend_chunk(qc, kb, vb, bias, kvalid, masks_q, masks_o):
    lq = qc.shape[0]
    qs = jnp.concatenate([jnp.where(mk, qc, 0.0) for mk in masks_q], axis=0).astype(BF16)
    yield
    s = _dot_nt(qs, kb) + bias
    yield
    if kvalid is not None:
        s = jnp.where(kvalid, s, NEG)
    e = jnp.exp(s - jnp.max(s, axis=-1, keepdims=True))
    eb = e.astype(BF16)
    rinv = 1.0 / jnp.sum(e, axis=-1, keepdims=True)
    yield
    o = jnp.dot(eb, vb, preferred_element_type=F32)
    yield
    o = o * rinv
    out = jnp.where(masks_o[0], o[0:lq], 0.0)
    for h in range(1, N_HEADS):
        out = out + jnp.where(masks_o[h], o[h * lq:(h + 1) * lq], 0.0)
    return out


def _attn_kernel(q_ref, k_ref, v_ref, kh_ref, vh_ref, bias_ref, y_ref, kk_ref, vv_ref, *, hist_is_cache, t_valid):
    tq, w = q_ref.shape[1], q_ref.shape[2]
    tpad = kk_ref.shape[0] - A_WIN
    scale = (w // N_HEADS) ** -0.5
    kk_ref[0:A_WIN, :] = kh_ref[0].astype(BF16)
    vv_ref[0:A_WIN, :] = vh_ref[0].astype(BF16)
    kk_ref[A_WIN:A_WIN + tpad, :] = _pad_rows(k_ref[0], tpad).astype(BF16)
    vv_ref[A_WIN:A_WIN + tpad, :] = _pad_rows(v_ref[0], tpad).astype(BF16)
    masks = _head_lane_masks(CHUNK, w)
    bias = bias_ref[...]
    band = (N_BAND + 1) * CHUNK
    slot = _iota((1, band), 1)
    hist_ok = jnp.logical_or(pl.program_id(1) > 0, hist_is_cache)
    q_all = _pad_rows(q_ref[0], tpad) * scale
    gens = []
    for i in range(tpad // CHUNK):
        base = i * CHUNK
        pos = slot + base
        kvalid = (pos < A_WIN + t_valid) & ((pos >= A_WIN) | hist_ok)
        gens.append(_attend_chunk(q_all[base:base + CHUNK], kk_ref[base:base + band, :], vv_ref[base:base + band, :],
                                  bias, kvalid, masks, masks))
    for i, out in enumerate(_interleave(gens)):
        base = i * CHUNK
        rows = min(CHUNK, tq - base)
        y_ref[0, base:base + rows, :] = out[0:rows].astype(y_ref.dtype)


def _attention(qkv, k_hist, v_hist, bias_tab, tq):
    bsz, t, w3 = qkv.shape
    w = w3 // 3
    hist_is_cache = k_hist is not None
    nt = t // tq
    tpad = -(-tq // CHUNK) * CHUNK
    band = (N_BAND + 1) * CHUNK
    if hist_is_cache:
        assert nt == 1
        hist_specs = [pl.BlockSpec((1, A_WIN, w), lambda b, j: (b, 0, 0))] * 2
        hist_args = (k_hist, v_hist)
    else:
        assert tq == A_WIN
        hist_specs = [pl.BlockSpec((1, tq, w), lambda b, j: (b, jnp.maximum(j - 1, 0), 1)),
                      pl.BlockSpec((1, tq, w), lambda b, j: (b, jnp.maximum(j - 1, 0), 2))]
        hist_args = (qkv, qkv)
    est = 2 * 5 * tpad * w * 4 + 2 * tq * w * 4 + 2 * (A_WIN + tpad) * w * 2 + 12 * N_HEADS * CHUNK * band * 4
    return pl.pallas_call(
        functools.partial(_attn_kernel, hist_is_cache=hist_is_cache, t_valid=tq if nt == 1 else tpad),
        out_shape=jax.ShapeDtypeStruct((bsz, t, w), BF16),
        grid=(bsz, nt),
        in_specs=[pl.BlockSpec((1, tq, w), lambda b, j: (b, j, 0)),
                  pl.BlockSpec((1, tq, w), lambda b, j: (b, j, 1)),
                  pl.BlockSpec((1, tq, w), lambda b, j: (b, j, 2)),
                  *hist_specs,
                  _const_spec((N_HEADS * CHUNK, band))],
        out_specs=pl.BlockSpec((1, tq, w), lambda b, j: (b, j, 0)),
        scratch_shapes=[pltpu.VMEM((A_WIN + tpad, w), BF16), pltpu.VMEM((A_WIN + tpad, w), BF16)],
        name="band_attention",
        compiler_params=_params(("parallel", "parallel"), est),
    )(qkv, qkv, qkv, *hist_args, bias_tab.reshape(N_HEADS * CHUNK, band))


def _mlstm_chunk(q, k, v, og, gi, gf, normg, c_ref, n_ref, m_ref, bi, n_valid, live=None):
    L, w = q.shape
    dh = w // N_HEADS
    lmask = _head_lane_masks(L, w)
    tril = _tril(L)
    blockdiag = (_iota((w, w), 0) // dh) == (_iota((w, w), 1) // dh)
    bd = blockdiag.astype(BF16)

    m_prev = m_ref[bi]
    c_old = c_ref[bi]
    n_old = n_ref[bi]
    qb16 = q.astype(BF16)
    kb16 = k.astype(BF16)
    vb16 = v.astype(BF16)
    cb16 = c_old.astype(BF16)
    qmask = [jnp.where(lmask[h], q, 0.0).astype(BF16) for h in range(N_HEADS)]
    lf_p = _bf16_pieces(jax.nn.log_sigmoid(gf), 3)
    qn_p = _bf16_pieces(q * n_old, 2)
    yield
    b = _dot_pieces_left(tril.astype(BF16), lf_p)
    q_c = jnp.dot(qb16, cb16, preferred_element_type=F32)
    q_n = _dot_pieces(qn_p, bd)
    yield
    qk = [_dot_nt(qmask[h], kb16) for h in range(N_HEADS)]
    u = gi - b
    u_t = u.T
    yield
    cm = jnp.zeros((L, w), F32)
    for h in range(N_HEADS):
        cmh = jnp.max(jnp.where(tril, u_t[h * dh:h * dh + 1, :], NEG), axis=1, keepdims=True)
        cm = jnp.where(lmask[h], cmh, cm)
    mx = jnp.maximum(m_prev, cm)
    m_t = b + mx
    inter = jnp.exp(m_prev - mx)
    last = n_valid - 1
    m_last = m_t[last:last + 1, :]
    b_last = b[last:last + 1, :]
    decay = jnp.exp(b_last + m_prev - m_last)
    ws = jnp.exp(u + (b_last - m_last))
    if n_valid < L:
        ws = jnp.where(_iota((L, w), 0) < n_valid, ws, 0.0)
    kw = k * ws
    kwb16 = kw.astype(BF16)
    yield
    v_heads = jnp.concatenate([jnp.where(lmask[h], v, 0.0).astype(BF16) for h in range(N_HEADS)], axis=0)
    rs = jnp.zeros((L, w), F32)
    wq = []
    for h in range(N_HEADS):
        arg = jnp.where(tril, u_t[h * dh:h * dh + 1, :] - mx[:, h * dh:h * dh + 1], NEG)
        wqk = jnp.exp(arg) * qk[h]
        rs = jnp.where(lmask[h], jnp.sum(wqk, axis=1, keepdims=True), rs)
        wq.append(wqk.astype(BF16))
        yield
    num = jnp.dot(jnp.concatenate(wq, axis=1), v_heads, preferred_element_type=F32)
    upd = _dot_tn(kwb16, vb16)
    den = inter * q_n + rs
    hout = (inter * q_c + num) / jnp.maximum(jnp.abs(den), jnp.exp(-m_t))
    mu_p = _bf16_pieces(hout, 2)
    yield
    xc = hout - _dot_pieces(mu_p, bd) * (1.0 / dh)
    var_p = _bf16_pieces(xc * xc, 2)
    yield
    var = _dot_pieces(var_p, bd) * (1.0 / dh)
    yield
    y = jax.nn.sigmoid(og) * (xc * lax.rsqrt(var + EPS) * normg)
    keep = (lambda new, old: new) if live is None else (lambda new, old: jnp.where(live, new, old))
    c_ref[bi] = keep(decay * c_old + jnp.where(blockdiag, upd, 0.0), c_old)
    n_ref[bi] = keep(decay * n_old + jnp.sum(kw, axis=0, keepdims=True), n_old)
    m_ref[bi] = keep(m_last, m_prev)
    return y


def _heads_to_blockdiag(c):
    w, dh = c.shape
    tile = (_iota((dh, w), 1) % dh == _iota((dh, w), 0)).astype(BF16)
    blockdiag = (_iota((w, w), 0) // dh) == (_iota((w, w), 1) // dh)
    return jnp.where(blockdiag, _split_dot(c, tile, 3), 0.0)


def _blockdiag_to_heads(c_bd):
    w = c_bd.shape[0]
    dh = w // N_HEADS
    fold = (_iota((w, dh), 0) % dh == _iota((w, dh), 1)).astype(BF16)
    return _split_dot(c_bd, fold, 3)


def _mlstm_kernel(q_ref, k_ref, v_ref, o_ref, g_ref, bi_ref, bf_ref, ng_ref,
                  c0_ref, n0_ref, m0_ref, y_ref, cout_ref, n_ref, m_ref, c_ref):
    nb, tt, w = q_ref.shape
    kscale = (w // N_HEADS) ** -0.5

    @pl.when(pl.program_id(1) == 0)
    def _():
        for bi in range(nb):
            c_ref[bi] = _heads_to_blockdiag(c0_ref[bi])
        n_ref[...] = n0_ref[...]
        m_ref[...] = m0_ref[...]

    bias_i = bi_ref[...]
    bias_f = bf_ref[...]
    ng = ng_ref[...]
    ex_i, ex_f = _head_expander(w, GATE_I), _head_expander(w, GATE_F)

    def chunk(bi, rows, n_valid):
        pad = lambda r: _pad_rows(r[bi, rows, :], SPAT)
        g = pad(g_ref)
        return _mlstm_chunk(pad(q_ref), pad(k_ref) * kscale, pad(v_ref), pad(o_ref),
                            _split_dot(g, ex_i, 3) + bias_i, _split_dot(g, ex_f, 3) + bias_f,
                            ng, c_ref, n_ref, m_ref, bi, n_valid)

    if tt < SPAT:
        for bi, y in enumerate(_interleave([chunk(bi, slice(None), tt) for bi in range(nb)])):
            y_ref[bi] = y[0:tt].astype(y_ref.dtype)
    else:
        def body(i, carry):
            rows = pl.ds(pl.multiple_of(i * SPAT, SPAT), SPAT)
            for bi, y in enumerate(_interleave([chunk(bi, rows, SPAT) for bi in range(nb)])):
                y_ref[bi, rows, :] = y.astype(y_ref.dtype)
            return carry
        lax.fori_loop(0, tt // SPAT, body, 0)

    @pl.when(pl.program_id(1) == pl.num_programs(1) - 1)
    def _():
        for bi in range(nb):
            cout_ref[bi] = _blockdiag_to_heads(c_ref[bi])


def _mlstm(qkvo, gates, bias_i, bias_f, normg, c0, n0, m0, nb, tt):
    bsz, t, w4 = qkvo.shape
    w = w4 // 4
    dh = w // N_HEADS
    blk = lambda c: pl.BlockSpec((nb, tt, w), lambda b, j: (b, j, c))
    st = lambda shape: pl.BlockSpec((nb,) + shape, lambda b, j: (b, 0, 0))
    est = nb * (2 * 5 * tt * w * 4 + 4 * tt * LANE * 4 + 6 * w * w * 4 + 40 * SPAT * w * 4 + 24 * SPAT * SPAT * 4)
    return pl.pallas_call(
        _mlstm_kernel,
        out_shape=[jax.ShapeDtypeStruct((bsz, t, w), BF16),
                   jax.ShapeDtypeStruct((bsz, w, dh), F32),
                   jax.ShapeDtypeStruct((bsz, 1, w), F32),
                   jax.ShapeDtypeStruct((bsz, 1, w), F32)],
        grid=(bsz // nb, t // tt),
        in_specs=[blk(0), blk(1), blk(2), blk(3), pl.BlockSpec((nb, tt, GATE_W), lambda b, j: (b, j, 0)),
                  _const_spec((1, w)), _const_spec((1, w)), _const_spec((1, w)),
                  st((w, dh)), st((1, w)), st((1, w))],
        out_specs=[pl.BlockSpec((nb, tt, w), lambda b, j: (b, j, 0)), st((w, dh)), st((1, w)), st((1, w))],
        scratch_shapes=[pltpu.VMEM((nb, w, w), F32)],
        name="mlstm",
        compiler_params=_params(("parallel", "arbitrary"), est),
    )(qkvo, qkvo, qkvo, qkvo, gates, bias_i, bias_f, normg, c0, n0, m0)


def _ssd_chunk(xs, bm, cm, z, dt, a_x, dskip, normg, s_ref, bi, live=None):
    L, w = xs.shape
    dh = w // N_HEADS
    ng = s_ref.shape[1]
    gw = w // ng
    lmask = _head_lane_masks(L, w)
    tril = _tril(L)
    s_old = [s_ref[bi, g] for g in range(ng)]
    sb16 = [s.astype(BF16) for s in s_old]
    cmb = cm.astype(BF16)
    bmb = bm.astype(BF16)
    xdt = (xs * dt).astype(BF16)
    da_p = _bf16_pieces(dt * a_x, 3)
    yield
    cs = _dot_pieces_left(tril.astype(BF16), da_p)
    cb = [_dot_nt(cmb[:, g * gw:(g + 1) * gw], bmb[:, g * gw:(g + 1) * gw]) for g in range(ng)]
    y_in = jnp.concatenate([jnp.dot(cmb[:, g * gw:(g + 1) * gw], sb16[g], preferred_element_type=F32)
                            for g in range(ng)], axis=1)
    yield
    cs_t = cs.T
    cs_last = cs[L - 1:L, :]
    wl = jnp.exp(cs_last - cs) * dt
    wx = (xs * wl).astype(BF16)
    yield
    x_heads = jnp.concatenate([jnp.where(lmask[h], xdt, jnp.zeros_like(xdt)) for h in range(N_HEADS)], axis=0)
    mh = []
    for h in range(N_HEADS):
        dec = jnp.exp(jnp.where(tril, cs[:, h * dh:h * dh + 1] - cs_t[h * dh:h * dh + 1, :], NEG))
        mh.append((cb[h * ng // N_HEADS] * dec).astype(BF16))
        yield
    y = jnp.exp(cs) * y_in + jnp.dot(jnp.concatenate(mh, axis=1), x_heads, preferred_element_type=F32)
    yield
    dec_x = jnp.exp(cs_last)
    s_new = [dec_x[:, g * gw:(g + 1) * gw] * s_old[g] + _dot_tn(bmb[:, g * gw:(g + 1) * gw], wx[:, g * gw:(g + 1) * gw])
             for g in range(ng)]
    yield
    yc = y + dskip * xs
    out = _rmsnorm_rows(yc * (z * jax.nn.sigmoid(z)), normg)
    for g in range(ng):
        s_ref[bi, g] = s_new[g] if live is None else jnp.where(live, s_new[g], s_old[g])
    return out


def _ssd_kernel(xbc_ref, z_ref, gd_ref, cw_ref, cb_ref, dtb_ref, a_ref, dskip_ref, ng_ref, conv0_ref, s0_ref,
                y_ref, conv_ref, s_ref, ext_ref):
    nb, tt, w = z_ref.shape
    hist = CONV_C - 1
    off = SUBLANE

    @pl.when(pl.program_id(1) == 0)
    def _():
        s_ref[...] = s0_ref[...]
        ext_ref[:, off - hist:off, :] = conv0_ref[...]

    cw = cw_ref[...]
    for bi in range(nb):
        ext_ref[bi, off:off + tt, :] = xbc_ref[bi]
        acc = cb_ref[...] + cw[CONV_C - 1:CONV_C, :] * ext_ref[bi, off:off + tt, :]
        for j in range(CONV_C - 1):
            sh = CONV_C - 1 - j
            acc = acc + cw[j:j + 1, :] * ext_ref[bi, off - sh:off - sh + tt, :]
        tail = ext_ref[bi, off + tt - hist:off + tt, :]
        ext_ref[bi, off - hist:off, :] = tail
        conv_ref[bi] = tail
        ext_ref[bi, off:off + tt, :] = acc * jax.nn.sigmoid(acc)

    dtb = dtb_ref[...]
    a_x = -jnp.exp(a_ref[...])
    dskip = dskip_ref[...]
    ng = ng_ref[...]
    ex = _head_expander(w, GATE_DT)

    def chunk(bi, ext_rows, rows, n_valid):
        act = _pad_rows(ext_ref[bi, ext_rows, :], SPAT)
        dt = jax.nn.softplus(_split_dot(_pad_rows(gd_ref[bi, rows, :], SPAT), ex, 3) + dtb)
        if n_valid < SPAT:
            dt = jnp.where(_iota((SPAT, w), 0) < n_valid, dt, 0.0)
        return _ssd_chunk(act[:, 0:w], act[:, w:2 * w], act[:, 2 * w:3 * w], _pad_rows(z_ref[bi, rows, :], SPAT),
                          dt, a_x, dskip, ng, s_ref, bi)

    if tt < SPAT:
        gens = [chunk(bi, slice(off, off + tt), slice(None), tt) for bi in range(nb)]
        for bi, y in enumerate(_interleave(gens)):
            y_ref[bi] = y[0:tt].astype(y_ref.dtype)
    else:
        def body(i, carry):
            start = pl.multiple_of(i * SPAT, SPAT)
            ext_rows = pl.ds(pl.multiple_of(start + off, SUBLANE), SPAT)
            gens = [chunk(bi, ext_rows, pl.ds(start, SPAT), SPAT) for bi in range(nb)]
            for bi, y in enumerate(_interleave(gens)):
                y_ref[bi, pl.ds(start, SPAT), :] = y.astype(y_ref.dtype)
            return carry
        lax.fori_loop(0, tt // SPAT, body, 0)


def _ssd(cin, gates, conv_w, conv_b, dtb, a_log_x, dskip, normg, conv0, s0, nb, tt):
    bsz, t, wtot = cin.shape
    w = dskip.shape[1]
    xw = wtot - w
    ng, n_c, gp = s0.shape[1:]
    st = lambda shape: pl.BlockSpec((nb,) + shape, lambda b, j: (b,) + (0,) * len(shape))
    est = nb * (2 * 2 * tt * wtot * 4 + 2 * tt * w * 4 + (tt + SUBLANE) * xw * 4 + 3 * tt * xw * 4 + 40 * SPAT * w * 4)
    return pl.pallas_call(
        _ssd_kernel,
        out_shape=[jax.ShapeDtypeStruct((bsz, t, w), BF16),
                   jax.ShapeDtypeStruct((bsz, CONV_C - 1, xw), F32),
                   jax.ShapeDtypeStruct(s0.shape, F32)],
        grid=(bsz // nb, t // tt),
        in_specs=[pl.BlockSpec((nb, tt, xw), lambda b, j: (b, j, 0)),
                  pl.BlockSpec((nb, tt, w), lambda b, j: (b, j, xw // w)),
                  pl.BlockSpec((nb, tt, GATE_W), lambda b, j: (b, j, 0)),
                  _const_spec((CONV_C, xw)), _const_spec((1, xw)), _const_spec((1, w)), _const_spec((1, w)),
                  _const_spec((1, w)), _const_spec((1, w)),
                  st((CONV_C - 1, xw)), st((ng, n_c, gp))],
        out_specs=[pl.BlockSpec((nb, tt, w), lambda b, j: (b, j, 0)), st((CONV_C - 1, xw)), st((ng, n_c, gp))],
        scratch_shapes=[pltpu.VMEM((nb, tt + SUBLANE, xw), F32)],
        name="ssd",
        compiler_params=_params(("parallel", "arbitrary"), est),
    )(cin, cin, gates, conv_w, conv_b, dtb, a_log_x, dskip, normg, conv0, s0)


def _gmlp_kernel(u_ref, v_ref, ng_ref, ws_ref, bs_ref, y_ref, *vn_refs):
    tt, w = u_ref.shape[1], u_ref.shape[2]
    u = jax.nn.gelu(u_ref[0])
    vr = jax.nn.gelu(v_ref[0])
    mu = jnp.mean(vr, axis=-1, keepdims=True)
    xc = vr - mu
    vn = xc * lax.rsqrt(jnp.mean(xc * xc, axis=-1, keepdims=True) + EPS) * ng_ref[...]
    for vn_ref in vn_refs:
        vn_ref[0] = vn
    rows = _iota((N_HEADS * SPAT, SPAT), 0) % SPAT
    wst = jnp.where(rows >= _iota((N_HEADS * SPAT, SPAT), 1), ws_ref[...], 0.0).astype(BF16)
    lmask = _head_lane_masks(SPAT, w)
    bias = bs_ref[...]
    vpad = _pad_rows(vn, -(-tt // SPAT) * SPAT).astype(BF16)
    for i in range(vpad.shape[0] // SPAT):
        fs = jnp.dot(wst, vpad[i * SPAT:(i + 1) * SPAT], preferred_element_type=F32)
        f = bias
        for g in range(N_HEADS):
            f = f + jnp.where(lmask[g], fs[g * SPAT:(g + 1) * SPAT], 0.0)
        n = min(SPAT, tt - i * SPAT)
        y_ref[0, i * SPAT:i * SPAT + n, :] = (u[i * SPAT:i * SPAT + n] * f[0:n]).astype(y_ref.dtype)


def _gmlp(din, normg, w_s, b_x, tt, emit_v):
    bsz, t, w2 = din.shape
    w = w2 // 2
    est = 2 * 4 * tt * w * 4 + 8 * tt * w * 4 + 4 * N_HEADS * SPAT * (SPAT + w) * 4
    out_shape = [jax.ShapeDtypeStruct((bsz, t, w), BF16)]
    if emit_v:
        out_shape.append(jax.ShapeDtypeStruct((bsz, t, w), F32))
    return pl.pallas_call(
        _gmlp_kernel,
        out_shape=out_shape,
        grid=(bsz, t // tt),
        in_specs=[pl.BlockSpec((1, tt, w), lambda b, j: (b, j, 0)),
                  pl.BlockSpec((1, tt, w), lambda b, j: (b, j, 1)),
                  _const_spec((1, w)), _const_spec((N_HEADS * SPAT, SPAT)), _const_spec((SPAT, w))],
        out_specs=[pl.BlockSpec((1, tt, w), lambda b, j: (b, j, 0))] * len(out_shape),
        name="spatial_gate",
        compiler_params=_params(("parallel", "parallel"), est),
    )(din, din, normg, w_s, b_x)


def _post_kernel(x_ref, ya_ref, yb_ref, yc_ref, yd_ref, mod_ref, g_ref, wo_ref, wg_ref, wu_ref, wd_ref,
                 cw_ref, cb_ref, f0_ref, gfin_ref, o_ref, fc_ref, ext_ref, *, final, fchunk):
    bb, tt, d = x_ref.shape
    m = bb * tt
    gw = ya_ref.shape[2]
    f = wg_ref.shape[1]
    hist = CONV_F - 1
    off = SUBLANE
    mod = mod_ref[...]
    mix = None
    for j, y_ref in enumerate((ya_ref, yb_ref, yc_ref, yd_ref)):
        p = jnp.dot(y_ref[...].reshape(m, gw), wo_ref[j * gw:(j + 1) * gw, :], preferred_element_type=F32)
        mix = p if mix is None else mix + p
    x1 = x_ref[...] + mod[:, :, 2 * d:3 * d] * mix.reshape(bb, tt, d)
    h2 = _rmsnorm_rows(x1, g_ref[...]) * (1.0 + mod[:, :, 4 * d:5 * d]) + mod[:, :, 3 * d:4 * d]
    hb = h2.reshape(m, d).astype(BF16)

    @pl.when(pl.program_id(1) == 0)
    def _():
        ext_ref[:, off - hist:off, :] = f0_ref[...]

    down = None
    for c0 in range(0, f, fchunk):
        cols = slice(c0, c0 + fchunk)
        g = jnp.dot(hb, wg_ref[:, cols], preferred_element_type=F32).reshape(bb, tt, fchunk)
        u = jnp.dot(hb, wu_ref[:, cols], preferred_element_type=F32)
        ext_ref[:, off:off + tt, cols] = g
        acc = cb_ref[:, cols] + cw_ref[CONV_F - 1:CONV_F, cols] * g
        for j in range(CONV_F - 1):
            sh = CONV_F - 1 - j
            acc = acc + cw_ref[j:j + 1, cols] * ext_ref[:, off - sh:off - sh + tt, cols]
        act = (acc * jax.nn.sigmoid(acc)).reshape(m, fchunk) * u
        p = jnp.dot(act.astype(BF16), wd_ref[cols, :], preferred_element_type=F32)
        down = p if down is None else down + p
    tail = ext_ref[:, off + tt - hist:off + tt, :]
    ext_ref[:, off - hist:off, :] = tail
    fc_ref[...] = tail
    x2 = x1 + mod[:, :, 5 * d:6 * d] * down.reshape(bb, tt, d)
    if final:
        x2 = _rmsnorm_rows(x2, gfin_ref[...])
    o_ref[...] = x2


def _post(x, ys, mod, g_ffn, w_out, w_gate, w_up, w_down, conv_w, conv_b, f0, g_final, bb, tt, final):
    bsz, t, d = x.shape
    gw = ys[0].shape[2]
    f = w_gate.shape[1]
    m = bb * tt
    tok = lambda wd: pl.BlockSpec((bb, tt, wd), lambda i, j: (i, j, 0))
    fchunk = 2 * LANE
    assert f % fchunk == 0
    est = (2 * 2 * m * d * 4 + 2 * 4 * m * gw * 2 + (d * d + 3 * d * f) * 2 + bb * (tt + SUBLANE) * f * 4
           + 8 * m * fchunk * 4 + 6 * m * d * 4)
    return pl.pallas_call(
        functools.partial(_post_kernel, final=final, fchunk=fchunk),
        out_shape=[jax.ShapeDtypeStruct((bsz, t, d), F32), jax.ShapeDtypeStruct((bsz, CONV_F - 1, f), F32)],
        grid=(bsz // bb, t // tt),
        in_specs=[tok(d), tok(gw), tok(gw), tok(gw), tok(gw),
                  pl.BlockSpec((bb, 1, mod.shape[2]), lambda i, j: (i, 0, 0)),
                  _const_spec((1, 1, d)), _const_spec((d, d)), _const_spec((d, f)), _const_spec((d, f)),
                  _const_spec((f, d)), _const_spec((CONV_F, f)), _const_spec((1, f)),
                  pl.BlockSpec((bb, CONV_F - 1, f), lambda i, j: (i, 0, 0)),
                  _const_spec((1, 1, d))],
        out_specs=[tok(d), pl.BlockSpec((bb, CONV_F - 1, f), lambda i, j: (i, 0, 0))],
        scratch_shapes=[pltpu.VMEM((bb, tt + SUBLANE, f), F32)],
        name="post_ffn",
        compiler_params=_params(("parallel", "arbitrary"), est),
    )(x, *ys, mod, g_ffn.reshape(1, 1, d), w_out, w_gate, w_up, w_down, conv_w, conv_b, f0, g_final.reshape(1, 1, d))


def _mixpost_kernel(qkv_ref, qkvo_ref, cin_ref, din_ref, gate_ref, x_ref, mod_ref, *refs, nt, final, fchunk, n_hist):
    hist_refs = refs[:2 * n_hist]
    (bias_ref, bi_ref, bf_ref, bng_ref, ccw_ref, ccb_ref, dtb_ref, alog_ref, dskip_ref, cng_ref,
     dng_ref, dws_ref, dbx_ref, gffn_ref, wo_ref, wg_ref, wu_ref, wd_ref, fcw_ref, fcb_ref, gfin_ref,
     o_ref, cout_ref, n_ref, m_ref, conv_ref, s_ref, fc_ref,
     kk_ref, vv_ref, ybuf_ref, cext_ref, fext_ref, fhist_ref, c_ref) = refs[2 * n_hist:]
    tt, d = x_ref.shape[1], x_ref.shape[2]
    w = d // 4
    f = wg_ref.shape[1]
    band = (N_BAND + 1) * CHUNK
    off = SUBLANE
    s = pl.program_id(0)
    n_tiles = pl.num_programs(0) - 1
    live = s < n_tiles
    mj = jnp.minimum(s, n_tiles - 1) % nt
    pj = jnp.maximum(s - 1, 0) % nt
    wslot = s % 2
    rslot = 1 - wslot

    @pl.when(s == 0)
    def _():
        ybuf_ref[...] = jnp.zeros(ybuf_ref.shape, ybuf_ref.dtype)

    @pl.when(mj == 0)
    def _():
        c_ref[...] = jnp.zeros(c_ref.shape, F32)
        n_ref[...] = jnp.zeros(n_ref.shape, F32)
        m_ref[...] = jnp.zeros(m_ref.shape, F32)
        s_ref[...] = jnp.zeros(s_ref.shape, F32)
        cext_ref[0:off, :] = jnp.zeros((off, cext_ref.shape[1]), F32)

    @pl.when(pj == 0)
    def _():
        fhist_ref[...] = jnp.zeros(fhist_ref.shape, F32)

    ex_i, ex_f, ex_dt = (_head_expander(w, first) for first in (GATE_I, GATE_F, GATE_DT))

    def attention():
        scale = (w // N_HEADS) ** -0.5
        for i, (kh_ref, vh_ref) in enumerate(zip(hist_refs[0::2], hist_refs[1::2])):
            kk_ref[i * tt:(i + 1) * tt, :] = kh_ref[0].astype(BF16)
            vv_ref[i * tt:(i + 1) * tt, :] = vh_ref[0].astype(BF16)
        kk_ref[A_WIN:A_WIN + tt, :] = qkv_ref[0, :, w:2 * w].astype(BF16)
        vv_ref[A_WIN:A_WIN + tt, :] = qkv_ref[0, :, 2 * w:3 * w].astype(BF16)
        masks = _head_lane_masks(CHUNK, w)
        bias = bias_ref[...]
        slot = _iota((1, band), 1)
        q_all = qkv_ref[0, :, 0:w].astype(F32) * scale
        yield
        for i in range(tt // CHUNK):
            base = i * CHUNK
            pos = slot + base
            kvalid = pos >= A_WIN
            for back in range(1, A_WIN // tt + 1):
                kvalid = kvalid | ((pos >= A_WIN - back * tt) & (mj >= back))
            out = yield from _attend_chunk(q_all[base:base + CHUNK], kk_ref[base:base + band, :],
                                           vv_ref[base:base + band, :], bias, kvalid, masks, masks)
            ybuf_ref[wslot, base:base + CHUNK, 0:w] = out.astype(BF16)
            yield

    def mlstm():
        kscale = (w // N_HEADS) ** -0.5
        bias_i, bias_f, ng = bi_ref[...], bf_ref[...], bng_ref[...]
        for c in range(tt // SPAT):
            rows = slice(c * SPAT, (c + 1) * SPAT)
            col = lambda c: qkvo_ref[0, rows, c * w:(c + 1) * w].astype(F32)
            g_p = _bf16_pieces(gate_ref[0, rows, :], GATE_PIECES)
            yield
            y = yield from _mlstm_chunk(
                col(0), col(1) * kscale, col(2), col(3),
                _dot_pieces(g_p, ex_i) + bias_i, _dot_pieces(g_p, ex_f) + bias_f,
                ng, c_ref, n_ref, m_ref, 0, SPAT, live)
            ybuf_ref[wslot, rows, w:2 * w] = y.astype(BF16)
            yield

    def ssd():
        xw = cext_ref.shape[1]
        hist = CONV_C - 1
        cext_ref[off:off + tt, :] = cin_ref[0, :, 0:xw].astype(F32)
        acc = ccb_ref[...] + ccw_ref[CONV_C - 1:CONV_C, :] * cext_ref[off:off + tt, :]
        for j in range(CONV_C - 1):
            sh = CONV_C - 1 - j
            acc = acc + ccw_ref[j:j + 1, :] * cext_ref[off - sh:off - sh + tt, :]
            yield
        tail = cext_ref[off + tt - hist:off + tt, :]
        cext_ref[off - hist:off, :] = tail
        conv_ref[0] = tail
        cext_ref[off:off + tt, :] = acc * jax.nn.sigmoid(acc)
        yield
        a_x = -jnp.exp(alog_ref[...])
        dtb, dskip, ng = dtb_ref[...], dskip_ref[...], cng_ref[...]
        for c in range(tt // SPAT):
            rows = slice(c * SPAT, (c + 1) * SPAT)
            erows = slice(off + c * SPAT, off + (c + 1) * SPAT)
            dt_p = _bf16_pieces(gate_ref[0, rows, :], GATE_PIECES)
            yield
            dt = jax.nn.softplus(_dot_pieces(dt_p, ex_dt) + dtb)
            y = yield from _ssd_chunk(cext_ref[erows, 0:w], cext_ref[erows, w:2 * w], cext_ref[erows, 2 * w:3 * w],
                                      cin_ref[0, rows, xw:xw + w].astype(F32), dt, a_x, dskip, ng, s_ref, 0, live)
            ybuf_ref[wslot, rows, 2 * w:3 * w] = y.astype(BF16)
            yield

    def gmlp():
        u = jax.nn.gelu(din_ref[0, :, 0:w].astype(F32))
        vr = jax.nn.gelu(din_ref[0, :, w:2 * w].astype(F32))
        yield
        xc = vr - jnp.mean(vr, axis=-1, keepdims=True)
        vn = (xc * lax.rsqrt(jnp.mean(xc * xc, axis=-1, keepdims=True) + EPS) * dng_ref[...]).astype(BF16)
        rows = _iota((N_HEADS * SPAT, SPAT), 0) % SPAT
        wst = jnp.where(rows >= _iota((N_HEADS * SPAT, SPAT), 1), dws_ref[...], 0.0).astype(BF16)
        wcat = jnp.concatenate([wst[g * SPAT:(g + 1) * SPAT] for g in range(N_HEADS)], axis=1)
        lmask = _head_lane_masks(SPAT, w)
        zero = jnp.zeros((SPAT, w), BF16)
        yield
        for i in range(tt // SPAT):
            vi = vn[i * SPAT:(i + 1) * SPAT]
            v_groups = jnp.concatenate([jnp.where(lmask[g], vi, zero) for g in range(N_HEADS)], axis=0)
            yield
            fgate = dbx_ref[...] + jnp.dot(wcat, v_groups, preferred_element_type=F32)
            yield
            ybuf_ref[wslot, i * SPAT:(i + 1) * SPAT, 3 * w:4 * w] = (u[i * SPAT:(i + 1) * SPAT] * fgate).astype(BF16)
            yield

    def post():
        hist = CONV_F - 1
        mod = mod_ref[0]
        mix = jnp.dot(ybuf_ref[rslot], wo_ref[...], preferred_element_type=F32)
        yield
        x1 = x_ref[0] + mod[:, 2 * d:3 * d] * mix
        h2 = _rmsnorm_rows(x1, gffn_ref[...]) * (1.0 + mod[:, 4 * d:5 * d]) + mod[:, 3 * d:4 * d]
        hb = h2.astype(BF16)
        yield
        down = None
        act_b16, act_cols = None, None
        for ci, c0 in enumerate(range(0, f, fchunk)):
            cols = slice(c0, c0 + fchunk)
            g = jnp.dot(hb, wg_ref[:, cols], preferred_element_type=F32)
            u = jnp.dot(hb, wu_ref[:, cols], preferred_element_type=F32)
            if act_b16 is not None:
                p = jnp.dot(act_b16, wd_ref[act_cols, :], preferred_element_type=F32)
                down = p if down is None else down + p
            yield
            buf = ci % 2
            fext_ref[buf, off - hist:off, :] = fhist_ref[0, off - hist:off, cols]
            fext_ref[buf, off:off + tt, :] = g
            acc = fcb_ref[:, cols] + fcw_ref[CONV_F - 1:CONV_F, cols] * g
            for j in range(CONV_F - 1):
                sh = CONV_F - 1 - j
                acc = acc + fcw_ref[j:j + 1, cols] * fext_ref[buf, off - sh:off - sh + tt, :]
            fhist_ref[0, off - hist:off, cols] = g[tt - hist:tt]
            act_b16, act_cols = ((acc * jax.nn.sigmoid(acc)) * u).astype(BF16), cols
            yield
        down = down + jnp.dot(act_b16, wd_ref[act_cols, :], preferred_element_type=F32)
        fc_ref[...] = fhist_ref[:, off - hist:off, :]
        x2 = x1 + mod[:, 5 * d:6 * d] * down
        if final:
            x2 = _rmsnorm_rows(x2, gfin_ref[...])
        o_ref[0] = x2

    _interleave([post(), mlstm(), ssd(), attention(), gmlp()])

    @pl.when(mj == nt - 1)
    def _():
        cout_ref[0] = _blockdiag_to_heads(c_ref[0])


def _mix_post(qkv_a, qkvo_b, c_in, d_in, gates, x, mod, p, g_final, tt, final):
    bsz, t, d = x.shape
    w = d // 4
    nt = t // tt
    n_tiles = bsz * nt
    f = p["f_w_gate"].shape[1]
    xw = c_in.shape[2] - w
    ng, n_c, gp = p["ssm_shape"]
    band = (N_BAND + 1) * CHUNK
    fchunk = 2 * LANE
    assert f % fchunk == 0 and tt % SPAT == 0 and t % tt == 0
    mix_tile = lambda width: pl.BlockSpec(
        (1, tt, width), lambda s: (jnp.minimum(s, n_tiles - 1) // nt, jnp.minimum(s, n_tiles - 1) % nt, 0))
    post_tile = lambda width: pl.BlockSpec(
        (1, tt, width), lambda s: (jnp.maximum(s - 1, 0) // nt, jnp.maximum(s - 1, 0) % nt, 0))
    mix_state = lambda shape: pl.BlockSpec(
        (1,) + shape, lambda s: (jnp.minimum(s, n_tiles - 1) // nt,) + (0,) * len(shape))
    post_state = lambda shape: pl.BlockSpec(
        (1,) + shape, lambda s: (jnp.maximum(s - 1, 0) // nt,) + (0,) * len(shape))
    consts = [p["bias_tab"].reshape(N_HEADS * CHUNK, band), p["bi"], p["bf"], p["b_norm_g"], p["c_conv_w"],
              p["c_conv_b"], p["dtb"], p["a_log_x"], p["dskip"], p["c_norm_g"], p["d_norm_g"], p["d_w_s"], p["d_b_x"],
              p["g_ffn"].reshape(1, d), p["w_out"], p["f_w_gate"], p["f_w_up"], p["f_w_down"], p["f_conv_w"],
              p["f_conv_b"], g_final.reshape(1, d)]
    in_widths = (qkv_a.shape[2], qkvo_b.shape[2], c_in.shape[2], d_in.shape[2], gates.shape[2])
    n_hist = A_WIN // tt
    assert n_hist * tt == A_WIN

    def hist_tile(back, col):
        def index(s):
            m = jnp.minimum(s, n_tiles - 1)
            return (m // nt, jnp.maximum(m % nt - back, 0), col)
        return pl.BlockSpec((1, tt, w), index)

    hist_specs = [hist_tile(back, col) for back in range(n_hist, 0, -1) for col in (1, 2)]
    in_bytes = sum(a.shape[2] * a.dtype.itemsize for a in (qkv_a, qkvo_b, c_in, d_in, gates))
    est = ((d * d + 3 * d * f) * 2 + 2 * tt * in_bytes + 4 * tt * d * 4 + (tt + SUBLANE) * (2 * fchunk + xw) * 4
           + 2 * (A_WIN + tt) * w * 2 + 8 * tt * w * 2 + 12 * tt * d * 4 + 6 * w * w * 4
           + 4 * n_hist * tt * w * qkv_a.dtype.itemsize)
    return pl.pallas_call(
        functools.partial(_mixpost_kernel, nt=nt, final=final, fchunk=fchunk, n_hist=n_hist),
        out_shape=[jax.ShapeDtypeStruct((bsz, t, d), F32),
                   jax.ShapeDtypeStruct((bsz, w, w // N_HEADS), F32),
                   jax.ShapeDtypeStruct((bsz, 1, w), F32),
                   jax.ShapeDtypeStruct((bsz, 1, w), F32),
                   jax.ShapeDtypeStruct((bsz, CONV_C - 1, xw), F32),
                   jax.ShapeDtypeStruct((bsz, ng, n_c, gp), F32),
                   jax.ShapeDtypeStruct((bsz, CONV_F - 1, f), F32)],
        grid=(n_tiles + 1,),
        in_specs=[mix_tile(wd) for wd in in_widths] + [post_tile(d), post_state((1, mod.shape[2]))] + hist_specs
                 + [_const_spec(c.shape) for c in consts],
        out_specs=[post_tile(d), mix_state((w, w // N_HEADS)), mix_state((1, w)), mix_state((1, w)),
                   mix_state((CONV_C - 1, xw)), mix_state((ng, n_c, gp)), post_state((CONV_F - 1, f))],
        scratch_shapes=[pltpu.VMEM((A_WIN + tt, w), BF16), pltpu.VMEM((A_WIN + tt, w), BF16),
                        pltpu.VMEM((2, tt, d), BF16), pltpu.VMEM((tt + SUBLANE, xw), F32),
                        pltpu.VMEM((2, tt + SUBLANE, fchunk), F32), pltpu.VMEM((1, SUBLANE, f), F32),
                        pltpu.VMEM((1, w, w), F32)],
        name="mix_post",
        compiler_params=_params(("arbitrary",), est),
    )(qkv_a, qkvo_b, c_in, d_in, gates, x, mod, *([qkv_a] * len(hist_specs)), *consts)


def _layer(x, mod, st, p, tiles, final, g_final):
    bsz, t, d = x.shape
    gw = d // 4
    dh = gw // N_HEADS
    bb, tt, tmix, nb = tiles
    mod3 = mod.reshape(bsz, 1, mod.shape[1])
    dtypes = (BF16,) * 4 + (F32,) if st is None else (F32,) * 5
    qkv_a, qkvo_b, c_in, d_in, gates = _in_proj(x, mod3, p["g_mix"], p["w_in"], p["widths"], dtypes, bb, tt)

    if st is None:
        x_new, c_new, n_new, m_new, conv_new, ssm_new, fconv_new = _mix_post(
            qkv_a, qkvo_b, c_in, d_in, gates, x, mod3, p, g_final, tmix, final)
        vn = []
    else:
        y_a = _attention(qkv_a, st["a_k"], st["a_v"], p["bias_tab"], tmix)
        y_b, c_new, n_new, m_new = _mlstm(qkvo_b, gates, p["bi"], p["bf"], p["b_norm_g"], st["b_c"], st["b_n"],
                                          st["b_m"], nb, tmix)
        y_c, conv_new, ssm_new = _ssd(c_in, gates, p["c_conv_w"], p["c_conv_b"], p["dtb"], p["a_log_x"], p["dskip"],
                                      p["c_norm_g"], st["c_conv"], st["c_ssm"], nb, tmix)
        y_d, *vn = _gmlp(d_in, p["d_norm_g"], p["d_w_s"], p["d_b_x"], tmix, True)
        x_new, fconv_new = _post(x, (y_a, y_b, y_c, y_d), mod3, p["g_ffn"], p["w_out"], p["f_w_gate"], p["f_w_up"],
                                 p["f_w_down"], p["f_conv_w"], p["f_conv_b"], st["f_conv"], g_final, bb, tt, final)

    keep = min(A_WIN, t)
    new_k = qkv_a[:, t - keep:, gw:2 * gw].reshape(bsz, keep, N_HEADS, dh).astype(F32)
    new_v = qkv_a[:, t - keep:, 2 * gw:3 * gw].reshape(bsz, keep, N_HEADS, dh).astype(F32)
    c_heads = c_new.reshape(bsz, N_HEADS, dh, dh)
    n_heads = n_new.reshape(bsz, N_HEADS, dh)
    m_heads = m_new[:, 0, ::dh]
    ssm = jnp.swapaxes(ssm_new, 2, 3).reshape(bsz, N_HEADS, gw // N_HEADS, ssm_new.shape[2])
    outs = (new_k, new_v, c_heads, n_heads, m_heads, ssm, conv_new, fconv_new, *vn)
    return x_new, outs


def _prep_layer(l, w_in, w_out, a_rel_bias, b_i_bias, b_f_bias, b_norm_g, c_conv_w, c_conv_b, c_dt_bias, c_a_log,
                c_d_skip, c_norm_g, d_norm_g, d_w_s, d_b_s, f_w_gate, f_w_up, f_conv_w, f_conv_b, f_w_down,
                g_norm_mix, g_norm_ffn):
    d = w_in.shape[1]
    gw = d // 4
    nh = b_i_bias.shape[1]
    xbc_w = c_conv_w.shape[2]
    sizes = (gw,) * 7 + (nh, nh, gw, xbc_w, c_dt_bias.shape[1], gw, gw)
    offs = [0]
    for s in sizes:
        offs.append(offs[-1] + s)
    moves, dst = [], 0
    for i in (0, 1, 2, 3, 4, 5, 6, 10, 9, 12, 13):
        moves.append((offs[i], dst, sizes[i]))
        dst += sizes[i]
    assert sizes[7] == sizes[8] == sizes[11] == N_HEADS
    packed = (dst, ((offs[7], GATE_I, N_HEADS), (offs[8], GATE_F, N_HEADS), (offs[11], GATE_DT, N_HEADS)))
    widths = (3 * gw, 4 * gw, xbc_w + gw, 2 * gw, GATE_W)
    dst += GATE_W
    assert dst == sum(widths)
    heads_x = lambda v: jnp.repeat(v.astype(F32), gw // nh)[None, :]
    return dict(
        w_in=_regroup_cast(w_in, l, moves, packed, dst), widths=widths, g_mix=g_norm_mix[l], g_ffn=g_norm_ffn[l],
        w_out=_layer_bf16(w_out, l), bias_tab=_bias_table(a_rel_bias[l]),
        bi=heads_x(b_i_bias[l]), bf=heads_x(b_f_bias[l]), b_norm_g=b_norm_g[l][None, :],
        c_conv_w=c_conv_w[l], c_conv_b=c_conv_b[l][None, :], dtb=heads_x(c_dt_bias[l]),
        a_log_x=heads_x(c_a_log[l]), dskip=heads_x(c_d_skip[l]), c_norm_g=c_norm_g[l][None, :],
        d_norm_g=d_norm_g[l][None, :], d_w_s=d_w_s[l].reshape(-1, d_w_s.shape[-1]),
        d_b_x=jnp.repeat(d_b_s[l].T, gw // d_b_s.shape[1], axis=1),
        f_w_gate=_layer_bf16(f_w_gate, l), f_w_up=_layer_bf16(f_w_up, l), f_w_down=_layer_bf16(f_w_down, l),
        f_conv_w=f_conv_w[l], f_conv_b=f_conv_b[l][None, :],
    )


def kernel(x_prompt, x_sample, c_prompt, c_sample, cache_a_k, cache_a_v, state_b_c, state_b_n, state_b_m, state_c_ssm, state_c_conv, state_ffn_conv, w_ada, b_ada, g_norm_mix, g_norm_ffn, w_in, w_out, a_rel_bias, b_i_bias, b_f_bias, b_norm_g, c_conv_w, c_conv_b, c_dt_bias, c_a_log, c_d_skip, c_norm_g, d_norm_g, d_w_s, d_b_s, f_w_gate, f_w_up, f_conv_w, f_conv_b, f_w_down, g_final):
    depth = w_in.shape[0]
    bp, tp, d = x_prompt.shape
    bs, ts, _ = x_sample.shape
    gw = d // 4
    dh = gw // N_HEADS
    xbc_w = c_conv_w.shape[2]
    n_c = state_c_ssm.shape[-1]
    g_c = (xbc_w - gw) // 2 // n_c

    mod_all = _ada(jnp.concatenate([c_prompt, c_sample], axis=0), w_ada, b_ada)

    nb = max(n for n in (8, 4, 2, 1) if bs % n == 0)
    tiles_p = (1, min(A_WIN, tp), min(A_WIN, tp), 1)
    tiles_s = (bs, ts, ts, nb)

    xp, xs = x_prompt, x_sample
    p_states, s_states = [], []
    for l in range(depth):
        p = _prep_layer(l, w_in, w_out, a_rel_bias, b_i_bias, b_f_bias, b_norm_g, c_conv_w, c_conv_b, c_dt_bias,
                        c_a_log, c_d_skip, c_norm_g, d_norm_g, d_w_s, d_b_s, f_w_gate, f_w_up, f_conv_w, f_conv_b,
                        f_w_down, g_norm_mix, g_norm_ffn)
        p["ssm_shape"] = (g_c, n_c, gw // g_c)
        final = l == depth - 1
        xp, sp = _layer(xp, mod_all[l, :bp], None, p, tiles_p, final, g_final)
        st_s = dict(a_k=cache_a_k[l].reshape(bs, -1, gw), a_v=cache_a_v[l].reshape(bs, -1, gw),
                    b_c=state_b_c[l].reshape(bs, gw, dh), b_n=state_b_n[l].reshape(bs, 1, gw),
                    b_m=jnp.repeat(state_b_m[l], dh, axis=1)[:, None, :],
                    c_conv=state_c_conv[l],
                    c_ssm=jnp.swapaxes(state_c_ssm[l].reshape(bs, g_c, gw // g_c, n_c), 2, 3),
                    f_conv=state_ffn_conv[l])
        xs, ss = _layer(xs, mod_all[l, bp:], st_s, p, tiles_s, final, g_final)
        p_states.append(sp)
        s_states.append(ss)

    stack = lambda states, i: jnp.stack([s[i] for s in states])
    return (xp, xs,
            *(stack(p_states, i) for i in range(8)),
            *(stack(s_states, i) for i in range(9)))
```

```python
import functools

import jax
import jax.numpy as jnp
from jax import lax
from jax.experimental import pallas as pl
from jax.experimental.pallas import tpu as pltpu

F32 = jnp.float32
BF16 = jnp.bfloat16

EPS = 1e-6
NEG = -1e30

CHUNK = 64
N_BAND = 8
A_WIN = N_BAND * CHUNK
REL_CLIP = 128
N_HEADS = 4
SPAT = 128
CONV_C = 4
CONV_F = 3
LANE = 128
GATE_I, GATE_F, GATE_DT = 0, 4, 8
GATE_W = LANE
GATE_PIECES = 2
SUBLANE = 8
VMEM_CAP = 64 * 1024 * 1024


def _vmem_limit(nbytes):
    return int(min(max(nbytes, 16 * 1024 * 1024), VMEM_CAP - 8 * 1024 * 1024))


def _params(sem, nbytes):
    return pltpu.CompilerParams(dimension_semantics=sem, vmem_limit_bytes=_vmem_limit(nbytes))


def _const_spec(shape):
    nd = len(shape)
    return pl.BlockSpec(shape, lambda *_: (0,) * nd, pipeline_mode=pl.Buffered(1))


def _iota(shape, dim):
    return lax.broadcasted_iota(jnp.int32, shape, dim)


def _bf16_pieces(x, parts):
    out = []
    r = x
    for i in range(parts):
        hi = r.astype(BF16)
        out.append(hi)
        if i + 1 < parts:
            r = r - hi.astype(F32)
    return out


def _dot_pieces(pieces, e):
    acc = None
    for piece in pieces:
        d = jnp.dot(piece, e, preferred_element_type=F32)
        acc = d if acc is None else acc + d
    return acc


def _dot_pieces_left(e, pieces):
    acc = None
    for piece in pieces:
        d = jnp.dot(e, piece, preferred_element_type=F32)
        acc = d if acc is None else acc + d
    return acc


def _split_dot(x, e, parts):
    return _dot_pieces(_bf16_pieces(x, parts), e)


def _dot_nt(a, b):
    return lax.dot_general(a, b, (((1,), (1,)), ((), ())), preferred_element_type=F32)


def _dot_tn(a, b):
    return lax.dot_general(a, b, (((0,), (0,)), ((), ())), preferred_element_type=F32)


def _head_expander(width, first):
    dh = width // N_HEADS
    return (_iota((LANE, width), 1) // dh == _iota((LANE, width), 0) - first).astype(BF16)


def _head_lane_masks(rows, width):
    dh = width // N_HEADS
    lane = _iota((rows, width), 1)
    return [(lane >= h * dh) & (lane < (h + 1) * dh) for h in range(N_HEADS)]


def _tril(n):
    return _iota((n, n), 0) >= _iota((n, n), 1)


def _pad_rows(x, rows):
    if x.shape[0] == rows:
        return x
    return jnp.concatenate([x, jnp.zeros((rows - x.shape[0], x.shape[1]), x.dtype)], axis=0)


def _interleave(gens):
    results = [None] * len(gens)
    live = list(range(len(gens)))
    while live:
        for i in list(live):
            try:
                next(gens[i])
            except StopIteration as stop:
                results[i] = stop.value
                live.remove(i)
    return results


def _rmsnorm_rows(x, g):
    return x * lax.rsqrt(jnp.mean(x * x, axis=-1, keepdims=True) + EPS) * g


def _regroup_kernel(w_ref, o_ref, *, moves, packed):
    w = w_ref[0]
    rows = w.shape[0]
    for src, dst, n in moves:
        o_ref[:, dst:dst + n] = w[:, src:src + n].astype(o_ref.dtype)
    dst, pieces = packed
    parts, lane = [], 0
    for src, first, n in pieces:
        if first > lane:
            parts.append(jnp.zeros((rows, first - lane), F32))
        parts.append(w[:, src:src + n])
        lane = first + n
    parts.append(jnp.zeros((rows, GATE_W - lane), F32))
    o_ref[:, dst:dst + GATE_W] = jnp.concatenate(parts, axis=1).astype(o_ref.dtype)


def _regroup_cast(w_all, layer, moves, packed, ncols, row_block=256):
    _, r, c = w_all.shape
    return pl.pallas_call(
        functools.partial(_regroup_kernel, moves=tuple(moves), packed=packed),
        out_shape=jax.ShapeDtypeStruct((r, ncols), BF16),
        grid=(r // row_block,),
        in_specs=[pl.BlockSpec((1, row_block, c), lambda i: (layer, i, 0))],
        out_specs=pl.BlockSpec((row_block, ncols), lambda i: (i, 0)),
        name="regroup_cast",
        compiler_params=_params(("parallel",), 2 * row_block * (c * 4 + ncols * 2) + row_block * c * 4),
    )(w_all)


def _ada_kernel(c_ref, w_ref, b_ref, o_ref):
    c = c_ref[...]
    h = (c * jax.nn.sigmoid(c)).astype(BF16)
    o_ref[0] = jnp.dot(h, w_ref[0].astype(BF16), preferred_element_type=F32) + b_ref[0]


def _ada(c_all, w_ada, b_ada):
    depth, d, n6 = w_ada.shape
    r = c_all.shape[0]
    tn = d
    return pl.pallas_call(
        _ada_kernel,
        out_shape=jax.ShapeDtypeStruct((depth, r, n6), F32),
        grid=(depth, n6 // tn),
        in_specs=[pl.BlockSpec((r, d), lambda l, j: (0, 0)),
                  pl.BlockSpec((1, d, tn), lambda l, j: (l, 0, j)),
                  pl.BlockSpec((1, 1, tn), lambda l, j: (l, 0, j))],
        out_specs=pl.BlockSpec((1, r, tn), lambda l, j: (l, 0, j)),
        name="ada_mod",
        compiler_params=_params(("parallel", "parallel"), 4 * (2 * d * tn * 4 + 2 * r * tn * 4 + r * d * 4)),
    )(c_all, w_ada, b_ada.reshape(depth, 1, n6))


def _bias_kernel(rb_ref, o_ref, *, lo, hi):
    nh, lq, lk = o_ref.shape
    idx = jnp.clip(A_WIN + _iota((lq, lk), 0) - _iota((lq, lk), 1), -REL_CLIP, REL_CLIP) + REL_CLIP
    for h in range(nh):
        def body(r, acc, h=h):
            return jnp.where(idx == r, rb_ref[h, r], acc)
        o_ref[h] = lax.fori_loop(lo, hi + 1, body, jnp.zeros((lq, lk), F32))


def _bias_table(rel_bias):
    nh = rel_bias.shape[0]
    band = (N_BAND + 1) * CHUNK
    lo = max(A_WIN - (band - 1), -REL_CLIP) + REL_CLIP
    hi = min(A_WIN + CHUNK - 1, REL_CLIP) + REL_CLIP
    return pl.pallas_call(
        functools.partial(_bias_kernel, lo=lo, hi=hi),
        out_shape=jax.ShapeDtypeStruct((nh, CHUNK, band), F32),
        in_specs=[pl.BlockSpec(memory_space=pltpu.SMEM)],
        out_specs=pl.BlockSpec(memory_space=pltpu.VMEM),
        name="rel_bias_table",
    )(rel_bias)


def _in_kernel(x_ref, mod_ref, g_ref, w_ref, *o_refs, col_starts):
    bb, tt, d = x_ref.shape
    x = x_ref[...]
    mod = mod_ref[...]
    h = _rmsnorm_rows(x, g_ref[...]) * (1.0 + mod[:, :, d:2 * d]) + mod[:, :, 0:d]
    hb = h.reshape(bb * tt, d).astype(BF16)
    for o_ref, (a, b) in zip(o_refs, col_starts):
        o_ref[...] = jnp.dot(hb, w_ref[:, a:b], preferred_element_type=F32).reshape(bb, tt, b - a).astype(o_ref.dtype)


def _in_proj(x, mod, g, w, widths, dtypes, bb, tt):
    bsz, t, d = x.shape
    ncols = w.shape[1]
    starts, a = [], 0
    for wd in widths:
        starts.append((a, a + wd))
        a += wd
    m = bb * tt
    est = 2 * m * d * 4 + 2 * d * ncols * 2 + 2 * m * ncols * 4 + 3 * m * d * 4 + m * max(widths) * 4
    return pl.pallas_call(
        functools.partial(_in_kernel, col_starts=tuple(starts)),
        out_shape=[jax.ShapeDtypeStruct((bsz, t, wd), dt) for wd, dt in zip(widths, dtypes)],
        grid=(bsz // bb, t // tt),
        in_specs=[pl.BlockSpec((bb, tt, d), lambda i, j: (i, j, 0)),
                  pl.BlockSpec((bb, 1, mod.shape[2]), lambda i, j: (i, 0, 0)),
                  _const_spec((1, 1, d)),
                  _const_spec((d, ncols))],
        out_specs=[pl.BlockSpec((bb, tt, wd), lambda i, j: (i, j, 0)) for wd in widths],
        name="in_proj",
        compiler_params=_params(("parallel", "parallel"), est),
    )(x, mod, g.reshape(1, 1, d), w)


def _attend_chunk(qc, kb, vb, bias, kvalid, masks_q, masks_o):
    lq = qc.shape[0]
    qs = jnp.concatenate([jnp.where(mk, qc, 0.0) for mk in masks_q], axis=0).astype(BF16)
    yield
    s = _dot_nt(qs, kb) + bias
    yield
    if kvalid is not None:
        s = jnp.where(kvalid, s, NEG)
    e = jnp.exp(s - jnp.max(s, axis=-1, keepdims=True))
    eb = e.astype(BF16)
    rinv = 1.0 / jnp.sum(e, axis=-1, keepdims=True)
    yield
    o = jnp.dot(eb, vb, preferred_element_type=F32)
    yield
    o = o * rinv
    out = jnp.where(masks_o[0], o[0:lq], 0.0)
    for h in range(1, N_HEADS):
        out = out + jnp.where(masks_o[h], o[h * lq:(h + 1) * lq], 0.0)
    return out


def _attn_kernel(q_ref, k_ref, v_ref, kh_ref, vh_ref, bias_ref, y_ref, kk_ref, vv_ref, *, hist_is_cache, t_valid):
    tq, w = q_ref.shape[1], q_ref.shape[2]
    tpad = kk_ref.shape[0] - A_WIN
    scale = (w // N_HEADS) ** -0.5
    kk_ref[0:A_WIN, :] = kh_ref[0].astype(BF16)
    vv_ref[0:A_WIN, :] = vh_ref[0].astype(BF16)
    kk_ref[A_WIN:A_WIN + tpad, :] = _pad_rows(k_ref[0], tpad).astype(BF16)
    vv_ref[A_WIN:A_WIN + tpad, :] = _pad_rows(v_ref[0], tpad).astype(BF16)
    masks = _head_lane_masks(CHUNK, w)
    bias = bias_ref[...]
    band = (N_BAND + 1) * CHUNK
    slot = _iota((1, band), 1)
    hist_ok = jnp.logical_or(pl.program_id(1) > 0, hist_is_cache)
    q_all = _pad_rows(q_ref[0], tpad) * scale
    gens = []
    for i in range(tpad // CHUNK):
        base = i * CHUNK
        pos = slot + base
        kvalid = (pos < A_WIN + t_valid) & ((pos >= A_WIN) | hist_ok)
        gens.append(_attend_chunk(q_all[base:base + CHUNK], kk_ref[base:base + band, :], vv_ref[base:base + band, :],
                                  bias, kvalid, masks, masks))
    for i, out in enumerate(_interleave(gens)):
        base = i * CHUNK
        rows = min(CHUNK, tq - base)
        y_ref[0, base:base + rows, :] = out[0:rows].astype(y_ref.dtype)


def _attention(qkv, k_hist, v_hist, bias_tab, tq):
    bsz, t, w3 = qkv.shape
    w = w3 // 3
    hist_is_cache = k_hist is not None
    nt = t // tq
    tpad = -(-tq // CHUNK) * CHUNK
    band = (N_BAND + 1) * CHUNK
    if hist_is_cache:
        assert nt == 1
        hist_specs = [pl.BlockSpec((1, A_WIN, w), lambda b, j: (b, 0, 0))] * 2
        hist_args = (k_hist, v_hist)
    else:
        assert tq == A_WIN
        hist_specs = [pl.BlockSpec((1, tq, w), lambda b, j: (b, jnp.maximum(j - 1, 0), 1)),
                      pl.BlockSpec((1, tq, w), lambda b, j: (b, jnp.maximum(j - 1, 0), 2))]
        hist_args = (qkv, qkv)
    est = 2 * 5 * tpad * w * 4 + 2 * tq * w * 4 + 2 * (A_WIN + tpad) * w * 2 + 12 * N_HEADS * CHUNK * band * 4
    return pl.pallas_call(
        functools.partial(_attn_kernel, hist_is_cache=hist_is_cache, t_valid=tq if nt == 1 else tpad),
        out_shape=jax.ShapeDtypeStruct((bsz, t, w), BF16),
        grid=(bsz, nt),
        in_specs=[pl.BlockSpec((1, tq, w), lambda b, j: (b, j, 0)),
                  pl.BlockSpec((1, tq, w), lambda b, j: (b, j, 1)),
                  pl.BlockSpec((1, tq, w), lambda b, j: (b, j, 2)),
                  *hist_specs,
                  _const_spec((N_HEADS * CHUNK, band))],
        out_specs=pl.BlockSpec((1, tq, w), lambda b, j: (b, j, 0)),
        scratch_shapes=[pltpu.VMEM((A_WIN + tpad, w), BF16), pltpu.VMEM((A_WIN + tpad, w), BF16)],
        name="band_attention",
        compiler_params=_params(("parallel", "parallel"), est),
    )(qkv, qkv, qkv, *hist_args, bias_tab.reshape(N_HEADS * CHUNK, band))


def _attn_cached_kernel(q_ref, k_ref, v_ref, kt_ref, vt_ref, bias_ref, y_ref):
    tq, w = q_ref.shape[1], q_ref.shape[2]
    scale = (w // N_HEADS) ** -0.5
    masks = _head_lane_masks(CHUNK, w)
    qc = _pad_rows(q_ref[0], CHUNK) * scale
    qs = jnp.concatenate([jnp.where(mk, qc, 0.0) for mk in masks], axis=0).astype(BF16)
    k_new = _pad_rows(k_ref[0], CHUNK).astype(BF16)
    v_new = _pad_rows(v_ref[0], CHUNK).astype(BF16)
    s = jnp.concatenate([jnp.dot(qs, kt_ref[0].astype(BF16), preferred_element_type=F32), _dot_nt(qs, k_new)],
                        axis=1) + bias_ref[...]
    s = jnp.where(_iota((1, A_WIN + CHUNK), 1) < A_WIN + tq, s, NEG)
    e = jnp.exp(s - jnp.max(s, axis=-1, keepdims=True))
    eb = e.astype(BF16)
    o = _dot_nt(eb[:, 0:A_WIN], vt_ref[0].astype(BF16)) + jnp.dot(eb[:, A_WIN:], v_new, preferred_element_type=F32)
    o = o * (1.0 / jnp.sum(e, axis=-1, keepdims=True))
    out = jnp.where(masks[0], o[0:CHUNK], 0.0)
    for h in range(1, N_HEADS):
        out = out + jnp.where(masks[h], o[h * CHUNK:(h + 1) * CHUNK], 0.0)
    y_ref[0] = out[0:tq].astype(y_ref.dtype)


def _attention_cached(qkv, kt_hist, vt_hist, bias_tab):
    bsz, t, w3 = qkv.shape
    w = w3 // 3
    band = (N_BAND + 1) * CHUNK
    assert t <= CHUNK
    est = 2 * (3 * t * w * 4 + 2 * w * A_WIN * 4) + 12 * N_HEADS * CHUNK * band * 4
    return pl.pallas_call(
        _attn_cached_kernel,
        out_shape=jax.ShapeDtypeStruct((bsz, t, w), BF16),
        grid=(bsz,),
        in_specs=[pl.BlockSpec((1, t, w), lambda b: (b, 0, 0)),
                  pl.BlockSpec((1, t, w), lambda b: (b, 0, 1)),
                  pl.BlockSpec((1, t, w), lambda b: (b, 0, 2)),
                  pl.BlockSpec((1, w, A_WIN), lambda b: (b, 0, 0)),
                  pl.BlockSpec((1, w, A_WIN), lambda b: (b, 0, 0)),
                  _const_spec((N_HEADS * CHUNK, band))],
        out_specs=pl.BlockSpec((1, t, w), lambda b: (b, 0, 0)),
        name="band_attention_cached",
        compiler_params=_params(("parallel",), est),
    )(qkv, qkv, qkv, kt_hist, vt_hist, bias_tab.reshape(N_HEADS * CHUNK, band))


def _mlstm_chunk(q, k, v, og, gi, gf, normg, c_ref, n_ref, m_ref, bi, n_valid, live=None):
    L, w = q.shape
    dh = w // N_HEADS
    lmask = _head_lane_masks(L, w)
    tril = _tril(L)
    blockdiag = (_iota((w, w), 0) // dh) == (_iota((w, w), 1) // dh)
    bd = blockdiag.astype(BF16)

    m_prev = m_ref[bi]
    c_old = c_ref[bi]
    n_old = n_ref[bi]
    qb16 = q.astype(BF16)
    kb16 = k.astype(BF16)
    vb16 = v.astype(BF16)
    cb16 = c_old.astype(BF16)
    qmask = [jnp.where(lmask[h], q, 0.0).astype(BF16) for h in range(N_HEADS)]
    lf_p = _bf16_pieces(jax.nn.log_sigmoid(gf), 3)
    qn_p = _bf16_pieces(q * n_old, 2)
    yield
    b = _dot_pieces_left(tril.astype(BF16), lf_p)
    q_c = jnp.dot(qb16, cb16, preferred_element_type=F32)
    q_n = _dot_pieces(qn_p, bd)
    yield
    qk = [_dot_nt(qmask[h], kb16) for h in range(N_HEADS)]
    u = gi - b
    u_t = u.T
    yield
    cm = jnp.zeros((L, w), F32)
    for h in range(N_HEADS):
        cmh = jnp.max(jnp.where(tril, u_t[h * dh:h * dh + 1, :], NEG), axis=1, keepdims=True)
        cm = jnp.where(lmask[h], cmh, cm)
    mx = jnp.maximum(m_prev, cm)
    m_t = b + mx
    inter = jnp.exp(m_prev - mx)
    last = n_valid - 1
    m_last = m_t[last:last + 1, :]
    b_last = b[last:last + 1, :]
    decay = jnp.exp(b_last + m_prev - m_last)
    ws = jnp.exp(u + (b_last - m_last))
    if n_valid < L:
        ws = jnp.where(_iota((L, w), 0) < n_valid, ws, 0.0)
    kw = k * ws
    kwb16 = kw.astype(BF16)
    yield
    v_heads = jnp.concatenate([jnp.where(lmask[h], v, 0.0).astype(BF16) for h in range(N_HEADS)], axis=0)
    rs = jnp.zeros((L, w), F32)
    wq = []
    for h in range(N_HEADS):
        arg = jnp.where(tril, u_t[h * dh:h * dh + 1, :] - mx[:, h * dh:h * dh + 1], NEG)
        wqk = jnp.exp(arg) * qk[h]
        rs = jnp.where(lmask[h], jnp.sum(wqk, axis=1, keepdims=True), rs)
        wq.append(wqk.astype(BF16))
        yield
    num = jnp.dot(jnp.concatenate(wq, axis=1), v_heads, preferred_element_type=F32)
    upd = _dot_tn(kwb16, vb16)
    den = inter * q_n + rs
    hout = (inter * q_c + num) / jnp.maximum(jnp.abs(den), jnp.exp(-m_t))
    mu_p = _bf16_pieces(hout, 2)
    yield
    xc = hout - _dot_pieces(mu_p, bd) * (1.0 / dh)
    var_p = _bf16_pieces(xc * xc, 2)
    yield
    var = _dot_pieces(var_p, bd) * (1.0 / dh)
    yield
    y = jax.nn.sigmoid(og) * (xc * lax.rsqrt(var + EPS) * normg)
    keep = (lambda new, old: new) if live is None else (lambda new, old: jnp.where(live, new, old))
    c_ref[bi] = keep(decay * c_old + jnp.where(blockdiag, upd, 0.0), c_old)
    n_ref[bi] = keep(decay * n_old + jnp.sum(kw, axis=0, keepdims=True), n_old)
    m_ref[bi] = keep(m_last, m_prev)
    return y


def _heads_to_blockdiag(c):
    w, dh = c.shape
    tile = (_iota((dh, w), 1) % dh == _iota((dh, w), 0)).astype(BF16)
    blockdiag = (_iota((w, w), 0) // dh) == (_iota((w, w), 1) // dh)
    return jnp.where(blockdiag, _split_dot(c, tile, 3), 0.0)


def _blockdiag_to_heads(c_bd):
    w = c_bd.shape[0]
    dh = w // N_HEADS
    fold = (_iota((w, dh), 0) % dh == _iota((w, dh), 1)).astype(BF16)
    return _split_dot(c_bd, fold, 3)


def _mlstm_kernel(q_ref, k_ref, v_ref, o_ref, g_ref, bi_ref, bf_ref, ng_ref,
                  c0_ref, n0_ref, m0_ref, y_ref, cout_ref, n_ref, m_ref, c_ref):
    nb, tt, w = q_ref.shape
    kscale = (w // N_HEADS) ** -0.5

    @pl.when(pl.program_id(1) == 0)
    def _():
        for bi in range(nb):
            c_ref[bi] = _heads_to_blockdiag(c0_ref[bi])
        n_ref[...] = n0_ref[...]
        m_ref[...] = m0_ref[...]

    bias_i = bi_ref[...]
    bias_f = bf_ref[...]
    ng = ng_ref[...]
    ex_i, ex_f = _head_expander(w, GATE_I), _head_expander(w, GATE_F)

    def chunk(bi, rows, n_valid):
        pad = lambda r: _pad_rows(r[bi, rows, :], SPAT)
        g = pad(g_ref)
        return _mlstm_chunk(pad(q_ref), pad(k_ref) * kscale, pad(v_ref), pad(o_ref),
                            _split_dot(g, ex_i, 3) + bias_i, _split_dot(g, ex_f, 3) + bias_f,
                            ng, c_ref, n_ref, m_ref, bi, n_valid)

    if tt < SPAT:
        for bi, y in enumerate(_interleave([chunk(bi, slice(None), tt) for bi in range(nb)])):
            y_ref[bi] = y[0:tt].astype(y_ref.dtype)
    else:
        def body(i, carry):
            rows = pl.ds(pl.multiple_of(i * SPAT, SPAT), SPAT)
            for bi, y in enumerate(_interleave([chunk(bi, rows, SPAT) for bi in range(nb)])):
                y_ref[bi, rows, :] = y.astype(y_ref.dtype)
            return carry
        lax.fori_loop(0, tt // SPAT, body, 0)

    @pl.when(pl.program_id(1) == pl.num_programs(1) - 1)
    def _():
        for bi in range(nb):
            cout_ref[bi] = _blockdiag_to_heads(c_ref[bi])


def _mlstm(qkvo, gates, bias_i, bias_f, normg, c0, n0, m0, nb, tt):
    bsz, t, w4 = qkvo.shape
    w = w4 // 4
    dh = w // N_HEADS
    blk = lambda c: pl.BlockSpec((nb, tt, w), lambda b, j: (b, j, c))
    st = lambda shape: pl.BlockSpec((nb,) + shape, lambda b, j: (b, 0, 0))
    est = nb * (2 * 5 * tt * w * 4 + 4 * tt * LANE * 4 + 6 * w * w * 4 + 40 * SPAT * w * 4 + 24 * SPAT * SPAT * 4)
    return pl.pallas_call(
        _mlstm_kernel,
        out_shape=[jax.ShapeDtypeStruct((bsz, t, w), BF16),
                   jax.ShapeDtypeStruct((bsz, w, dh), F32),
                   jax.ShapeDtypeStruct((bsz, 1, w), F32),
                   jax.ShapeDtypeStruct((bsz, 1, w), F32)],
        grid=(bsz // nb, t // tt),
        in_specs=[blk(0), blk(1), blk(2), blk(3), pl.BlockSpec((nb, tt, GATE_W), lambda b, j: (b, j, 0)),
                  _const_spec((1, w)), _const_spec((1, w)), _const_spec((1, w)),
                  st((w, dh)), st((1, w)), st((1, w))],
        out_specs=[pl.BlockSpec((nb, tt, w), lambda b, j: (b, j, 0)), st((w, dh)), st((1, w)), st((1, w))],
        scratch_shapes=[pltpu.VMEM((nb, w, w), F32)],
        name="mlstm",
        compiler_params=_params(("parallel", "arbitrary"), est),
    )(qkvo, qkvo, qkvo, qkvo, gates, bias_i, bias_f, normg, c0, n0, m0)


def _ssd_chunk(xs, bm, cm, z, dt, a_x, dskip, normg, s_ref, bi, live=None):
    L, w = xs.shape
    dh = w // N_HEADS
    ng = s_ref.shape[1]
    gw = w // ng
    lmask = _head_lane_masks(L, w)
    tril = _tril(L)
    s_old = [s_ref[bi, g] for g in range(ng)]
    sb16 = [s.astype(BF16) for s in s_old]
    cmb = cm.astype(BF16)
    bmb = bm.astype(BF16)
    xdt = (xs * dt).astype(BF16)
    da_p = _bf16_pieces(dt * a_x, 3)
    yield
    cs = _dot_pieces_left(tril.astype(BF16), da_p)
    cb = [_dot_nt(cmb[:, g * gw:(g + 1) * gw], bmb[:, g * gw:(g + 1) * gw]) for g in range(ng)]
    y_in = jnp.concatenate([jnp.dot(cmb[:, g * gw:(g + 1) * gw], sb16[g], preferred_element_type=F32)
                            for g in range(ng)], axis=1)
    yield
    cs_t = cs.T
    cs_last = cs[L - 1:L, :]
    wl = jnp.exp(cs_last - cs) * dt
    wx = (xs * wl).astype(BF16)
    yield
    x_heads = jnp.concatenate([jnp.where(lmask[h], xdt, jnp.zeros_like(xdt)) for h in range(N_HEADS)], axis=0)
    mh = []
    for h in range(N_HEADS):
        dec = jnp.exp(jnp.where(tril, cs[:, h * dh:h * dh + 1] - cs_t[h * dh:h * dh + 1, :], NEG))
        mh.append((cb[h * ng // N_HEADS] * dec).astype(BF16))
        yield
    y = jnp.exp(cs) * y_in + jnp.dot(jnp.concatenate(mh, axis=1), x_heads, preferred_element_type=F32)
    yield
    dec_x = jnp.exp(cs_last)
    s_new = [dec_x[:, g * gw:(g + 1) * gw] * s_old[g] + _dot_tn(bmb[:, g * gw:(g + 1) * gw], wx[:, g * gw:(g + 1) * gw])
             for g in range(ng)]
    yield
    yc = y + dskip * xs
    out = _rmsnorm_rows(yc * (z * jax.nn.sigmoid(z)), normg)
    for g in range(ng):
        s_ref[bi, g] = s_new[g] if live is None else jnp.where(live, s_new[g], s_old[g])
    return out


def _ssd_kernel(xbc_ref, z_ref, gd_ref, cw_ref, cb_ref, dtb_ref, a_ref, dskip_ref, ng_ref, conv0_ref, s0_ref,
                y_ref, conv_ref, s_ref, ext_ref):
    nb, tt, w = z_ref.shape
    hist = CONV_C - 1
    off = SUBLANE

    @pl.when(pl.program_id(1) == 0)
    def _():
        s_ref[...] = s0_ref[...]
        ext_ref[:, off - hist:off, :] = conv0_ref[...]

    cw = cw_ref[...]
    for bi in range(nb):
        ext_ref[bi, off:off + tt, :] = xbc_ref[bi]
        acc = cb_ref[...] + cw[CONV_C - 1:CONV_C, :] * ext_ref[bi, off:off + tt, :]
        for j in range(CONV_C - 1):
            sh = CONV_C - 1 - j
            acc = acc + cw[j:j + 1, :] * ext_ref[bi, off - sh:off - sh + tt, :]
        tail = ext_ref[bi, off + tt - hist:off + tt, :]
        ext_ref[bi, off - hist:off, :] = tail
        conv_ref[bi] = tail
        ext_ref[bi, off:off + tt, :] = acc * jax.nn.sigmoid(acc)

    dtb = dtb_ref[...]
    a_x = -jnp.exp(a_ref[...])
    dskip = dskip_ref[...]
    ng = ng_ref[...]
    ex = _head_expander(w, GATE_DT)

    def chunk(bi, ext_rows, rows, n_valid):
        act = _pad_rows(ext_ref[bi, ext_rows, :], SPAT)
        dt = jax.nn.softplus(_split_dot(_pad_rows(gd_ref[bi, rows, :], SPAT), ex, 3) + dtb)
        if n_valid < SPAT:
            dt = jnp.where(_iota((SPAT, w), 0) < n_valid, dt, 0.0)
        return _ssd_chunk(act[:, 0:w], act[:, w:2 * w], act[:, 2 * w:3 * w], _pad_rows(z_ref[bi, rows, :], SPAT),
                          dt, a_x, dskip, ng, s_ref, bi)

    if tt < SPAT:
        gens = [chunk(bi, slice(off, off + tt), slice(None), tt) for bi in range(nb)]
        for bi, y in enumerate(_interleave(gens)):
            y_ref[bi] = y[0:tt].astype(y_ref.dtype)
    else:
        def body(i, carry):
            start = pl.multiple_of(i * SPAT, SPAT)
            ext_rows = pl.ds(pl.multiple_of(start + off, SUBLANE), SPAT)
            gens = [chunk(bi, ext_rows, pl.ds(start, SPAT), SPAT) for bi in range(nb)]
            for bi, y in enumerate(_interleave(gens)):
                y_ref[bi, pl.ds(start, SPAT), :] = y.astype(y_ref.dtype)
            return carry
        lax.fori_loop(0, tt // SPAT, body, 0)


def _ssd(cin, gates, conv_w, conv_b, dtb, a_log_x, dskip, normg, conv0, s0, nb, tt):
    bsz, t, wtot = cin.shape
    w = dskip.shape[1]
    xw = wtot - w
    ng, n_c, gp = s0.shape[1:]
    st = lambda shape: pl.BlockSpec((nb,) + shape, lambda b, j: (b,) + (0,) * len(shape))
    est = nb * (2 * 2 * tt * wtot * 4 + 2 * tt * w * 4 + (tt + SUBLANE) * xw * 4 + 3 * tt * xw * 4 + 40 * SPAT * w * 4)
    return pl.pallas_call(
        _ssd_kernel,
        out_shape=[jax.ShapeDtypeStruct((bsz, t, w), BF16),
                   jax.ShapeDtypeStruct((bsz, CONV_C - 1, xw), F32),
                   jax.ShapeDtypeStruct(s0.shape, F32)],
        grid=(bsz // nb, t // tt),
        in_specs=[pl.BlockSpec((nb, tt, xw), lambda b, j: (b, j, 0)),
                  pl.BlockSpec((nb, tt, w), lambda b, j: (b, j, xw // w)),
                  pl.BlockSpec((nb, tt, GATE_W), lambda b, j: (b, j, 0)),
                  _const_spec((CONV_C, xw)), _const_spec((1, xw)), _const_spec((1, w)), _const_spec((1, w)),
                  _const_spec((1, w)), _const_spec((1, w)),
                  st((CONV_C - 1, xw)), st((ng, n_c, gp))],
        out_specs=[pl.BlockSpec((nb, tt, w), lambda b, j: (b, j, 0)), st((CONV_C - 1, xw)), st((ng, n_c, gp))],
        scratch_shapes=[pltpu.VMEM((nb, tt + SUBLANE, xw), F32)],
        name="ssd",
        compiler_params=_params(("parallel", "arbitrary"), est),
    )(cin, cin, gates, conv_w, conv_b, dtb, a_log_x, dskip, normg, conv0, s0)


def _gmlp_kernel(u_ref, v_ref, ng_ref, ws_ref, bs_ref, y_ref, *vn_refs):
    tt, w = u_ref.shape[1], u_ref.shape[2]
    u = jax.nn.gelu(u_ref[0])
    vr = jax.nn.gelu(v_ref[0])
    mu = jnp.mean(vr, axis=-1, keepdims=True)
    xc = vr - mu
    vn = xc * lax.rsqrt(jnp.mean(xc * xc, axis=-1, keepdims=True) + EPS) * ng_ref[...]
    for vn_ref in vn_refs:
        vn_ref[0] = vn
    rows = _iota((N_HEADS * SPAT, SPAT), 0) % SPAT
    wst = jnp.where(rows >= _iota((N_HEADS * SPAT, SPAT), 1), ws_ref[...], 0.0).astype(BF16)
    lmask = _head_lane_masks(SPAT, w)
    bias = bs_ref[...]
    vpad = _pad_rows(vn, -(-tt // SPAT) * SPAT).astype(BF16)
    for i in range(vpad.shape[0] // SPAT):
        fs = jnp.dot(wst, vpad[i * SPAT:(i + 1) * SPAT], preferred_element_type=F32)
        f = bias
        for g in range(N_HEADS):
            f = f + jnp.where(lmask[g], fs[g * SPAT:(g + 1) * SPAT], 0.0)
        n = min(SPAT, tt - i * SPAT)
        y_ref[0, i * SPAT:i * SPAT + n, :] = (u[i * SPAT:i * SPAT + n] * f[0:n]).astype(y_ref.dtype)


def _gmlp(din, normg, w_s, b_x, tt, emit_v):
    bsz, t, w2 = din.shape
    w = w2 // 2
    est = 2 * 4 * tt * w * 4 + 8 * tt * w * 4 + 4 * N_HEADS * SPAT * (SPAT + w) * 4
    out_shape = [jax.ShapeDtypeStruct((bsz, t, w), BF16)]
    if emit_v:
        out_shape.append(jax.ShapeDtypeStruct((bsz, t, w), F32))
    return pl.pallas_call(
        _gmlp_kernel,
        out_shape=out_shape,
        grid=(bsz, t // tt),
        in_specs=[pl.BlockSpec((1, tt, w), lambda b, j: (b, j, 0)),
                  pl.BlockSpec((1, tt, w), lambda b, j: (b, j, 1)),
                  _const_spec((1, w)), _const_spec((N_HEADS * SPAT, SPAT)), _const_spec((SPAT, w))],
        out_specs=[pl.BlockSpec((1, tt, w), lambda b, j: (b, j, 0))] * len(out_shape),
        name="spatial_gate",
        compiler_params=_params(("parallel", "parallel"), est),
    )(din, din, normg, w_s, b_x)


def _post_kernel(x_ref, ya_ref, yb_ref, yc_ref, yd_ref, mod_ref, g_ref, wo_ref, wg_ref, wu_ref, wd_ref,
                 cw_ref, cb_ref, f0_ref, gfin_ref, o_ref, fc_ref, ext_ref, *, final, fchunk):
    bb, tt, d = x_ref.shape
    m = bb * tt
    gw = ya_ref.shape[2]
    f = wg_ref.shape[1]
    hist = CONV_F - 1
    off = SUBLANE
    mod = mod_ref[...]
    mix = None
    for j, y_ref in enumerate((ya_ref, yb_ref, yc_ref, yd_ref)):
        p = jnp.dot(y_ref[...].reshape(m, gw), wo_ref[j * gw:(j + 1) * gw, :], preferred_element_type=F32)
        mix = p if mix is None else mix + p
    x1 = x_ref[...] + mod[:, :, 2 * d:3 * d] * mix.reshape(bb, tt, d)
    h2 = _rmsnorm_rows(x1, g_ref[...]) * (1.0 + mod[:, :, 4 * d:5 * d]) + mod[:, :, 3 * d:4 * d]
    hb = h2.reshape(m, d).astype(BF16)

    @pl.when(pl.program_id(1) == 0)
    def _():
        ext_ref[:, off - hist:off, :] = f0_ref[...]

    down = None
    for c0 in range(0, f, fchunk):
        cols = slice(c0, c0 + fchunk)
        g = jnp.dot(hb, wg_ref[:, cols], preferred_element_type=F32).reshape(bb, tt, fchunk)
        u = jnp.dot(hb, wu_ref[:, cols], preferred_element_type=F32)
        ext_ref[:, off:off + tt, cols] = g
        acc = cb_ref[:, cols] + cw_ref[CONV_F - 1:CONV_F, cols] * g
        for j in range(CONV_F - 1):
            sh = CONV_F - 1 - j
            acc = acc + cw_ref[j:j + 1, cols] * ext_ref[:, off - sh:off - sh + tt, cols]
        act = (acc * jax.nn.sigmoid(acc)).reshape(m, fchunk) * u
        p = jnp.dot(act.astype(BF16), wd_ref[cols, :], preferred_element_type=F32)
        down = p if down is None else down + p
    tail = ext_ref[:, off + tt - hist:off + tt, :]
    ext_ref[:, off - hist:off, :] = tail
    fc_ref[...] = tail
    x2 = x1 + mod[:, :, 5 * d:6 * d] * down.reshape(bb, tt, d)
    if final:
        x2 = _rmsnorm_rows(x2, gfin_ref[...])
    o_ref[...] = x2


def _post(x, ys, mod, g_ffn, w_out, w_gate, w_up, w_down, conv_w, conv_b, f0, g_final, bb, tt, final):
    bsz, t, d = x.shape
    gw = ys[0].shape[2]
    f = w_gate.shape[1]
    m = bb * tt
    tok = lambda wd: pl.BlockSpec((bb, tt, wd), lambda i, j: (i, j, 0))
    fchunk = 2 * LANE
    assert f % fchunk == 0
    est = (2 * 2 * m * d * 4 + 2 * 4 * m * gw * 2 + (d * d + 3 * d * f) * 2 + bb * (tt + SUBLANE) * f * 4
           + 8 * m * fchunk * 4 + 6 * m * d * 4)
    return pl.pallas_call(
        functools.partial(_post_kernel, final=final, fchunk=fchunk),
        out_shape=[jax.ShapeDtypeStruct((bsz, t, d), F32), jax.ShapeDtypeStruct((bsz, CONV_F - 1, f), F32)],
        grid=(bsz // bb, t // tt),
        in_specs=[tok(d), tok(gw), tok(gw), tok(gw), tok(gw),
                  pl.BlockSpec((bb, 1, mod.shape[2]), lambda i, j: (i, 0, 0)),
                  _const_spec((1, 1, d)), _const_spec((d, d)), _const_spec((d, f)), _const_spec((d, f)),
                  _const_spec((f, d)), _const_spec((CONV_F, f)), _const_spec((1, f)),
                  pl.BlockSpec((bb, CONV_F - 1, f), lambda i, j: (i, 0, 0)),
                  _const_spec((1, 1, d))],
        out_specs=[tok(d), pl.BlockSpec((bb, CONV_F - 1, f), lambda i, j: (i, 0, 0))],
        scratch_shapes=[pltpu.VMEM((bb, tt + SUBLANE, f), F32)],
        name="post_ffn",
        compiler_params=_params(("parallel", "arbitrary"), est),
    )(x, *ys, mod, g_ffn.reshape(1, 1, d), w_out, w_gate, w_up, w_down, conv_w, conv_b, f0, g_final.reshape(1, 1, d))


def _mixpost_kernel(qkv_ref, qkvo_ref, cin_ref, din_ref, gate_ref, x_ref, mod_ref, *refs, nt, final, fchunk, n_hist):
    hist_refs = refs[:2 * n_hist]
    (bias_ref, bi_ref, bf_ref, bng_ref, ccw_ref, ccb_ref, dtb_ref, alog_ref, dskip_ref, cng_ref,
     dng_ref, dws_ref, dbx_ref, gffn_ref, wo_ref, wg_ref, wu_ref, wd_ref, fcw_ref, fcb_ref, gfin_ref,
     o_ref, cout_ref, n_ref, m_ref, conv_ref, s_ref, fc_ref,
     kk_ref, vv_ref, ybuf_ref, cext_ref, fext_ref, fhist_ref, c_ref) = refs[2 * n_hist:]
    tt, d = x_ref.shape[1], x_ref.shape[2]
    w = d // 4
    f = wg_ref.shape[1]
    band = (N_BAND + 1) * CHUNK
    off = SUBLANE
    s = pl.program_id(0)
    n_tiles = pl.num_programs(0) - 1
    live = s < n_tiles
    mj = jnp.minimum(s, n_tiles - 1) % nt
    pj = jnp.maximum(s - 1, 0) % nt
    wslot = s % 2
    rslot = 1 - wslot

    @pl.when(s == 0)
    def _():
        ybuf_ref[...] = jnp.zeros(ybuf_ref.shape, ybuf_ref.dtype)

    @pl.when(mj == 0)
    def _():
        c_ref[...] = jnp.zeros(c_ref.shape, F32)
        n_ref[...] = jnp.zeros(n_ref.shape, F32)
        m_ref[...] = jnp.zeros(m_ref.shape, F32)
        s_ref[...] = jnp.zeros(s_ref.shape, F32)
        cext_ref[0:off, :] = jnp.zeros((off, cext_ref.shape[1]), F32)

    @pl.when(pj == 0)
    def _():
        fhist_ref[...] = jnp.zeros(fhist_ref.shape, F32)

    ex_i, ex_f, ex_dt = (_head_expander(w, first) for first in (GATE_I, GATE_F, GATE_DT))

    def attention():
        scale = (w // N_HEADS) ** -0.5
        for i, (kh_ref, vh_ref) in enumerate(zip(hist_refs[0::2], hist_refs[1::2])):
            kk_ref[i * tt:(i + 1) * tt, :] = kh_ref[0].astype(BF16)
            vv_ref[i * tt:(i + 1) * tt, :] = vh_ref[0].astype(BF16)
        kk_ref[A_WIN:A_WIN + tt, :] = qkv_ref[0, :, w:2 * w].astype(BF16)
        vv_ref[A_WIN:A_WIN + tt, :] = qkv_ref[0, :, 2 * w:3 * w].astype(BF16)
        masks = _head_lane_masks(CHUNK, w)
        bias = bias_ref[...]
        slot = _iota((1, band), 1)
        q_all = qkv_ref[0, :, 0:w].astype(F32) * scale
        yield
        for i in range(tt // CHUNK):
            base = i * CHUNK
            pos = slot + base
            kvalid = pos >= A_WIN
            for back in range(1, A_WIN // tt + 1):
                kvalid = kvalid | ((pos >= A_WIN - back * tt) & (mj >= back))
            out = yield from _attend_chunk(q_all[base:base + CHUNK], kk_ref[base:base + band, :],
                                           vv_ref[base:base + band, :], bias, kvalid, masks, masks)
            ybuf_ref[wslot, base:base + CHUNK, 0:w] = out.astype(BF16)
            yield

    def mlstm():
        kscale = (w // N_HEADS) ** -0.5
        bias_i, bias_f, ng = bi_ref[...], bf_ref[...], bng_ref[...]
        for c in range(tt // SPAT):
            rows = slice(c * SPAT, (c + 1) * SPAT)
            col = lambda c: qkvo_ref[0, rows, c * w:(c + 1) * w].astype(F32)
            g_p = _bf16_pieces(gate_ref[0, rows, :], GATE_PIECES)
            yield
            y = yield from _mlstm_chunk(
                col(0), col(1) * kscale, col(2), col(3),
                _dot_pieces(g_p, ex_i) + bias_i, _dot_pieces(g_p, ex_f) + bias_f,
                ng, c_ref, n_ref, m_ref, 0, SPAT, live)
            ybuf_ref[wslot, rows, w:2 * w] = y.astype(BF16)
            yield

    def ssd():
        xw = cext_ref.shape[1]
        hist = CONV_C - 1
        cext_ref[off:off + tt, :] = cin_ref[0, :, 0:xw].astype(F32)
        acc = ccb_ref[...] + ccw_ref[CONV_C - 1:CONV_C, :] * cext_ref[off:off + tt, :]
        for j in range(CONV_C - 1):
            sh = CONV_C - 1 - j
            acc = acc + ccw_ref[j:j + 1, :] * cext_ref[off - sh:off - sh + tt, :]
            yield
        tail = cext_ref[off + tt - hist:off + tt, :]
        cext_ref[off - hist:off, :] = tail
        conv_ref[0] = tail
        cext_ref[off:off + tt, :] = acc * jax.nn.sigmoid(acc)
        yield
        a_x = -jnp.exp(alog_ref[...])
        dtb, dskip, ng = dtb_ref[...], dskip_ref[...], cng_ref[...]
        for c in range(tt // SPAT):
            rows = slice(c * SPAT, (c + 1) * SPAT)
            erows = slice(off + c * SPAT, off + (c + 1) * SPAT)
            dt_p = _bf16_pieces(gate_ref[0, rows, :], GATE_PIECES)
            yield
            dt = jax.nn.softplus(_dot_pieces(dt_p, ex_dt) + dtb)
            y = yield from _ssd_chunk(cext_ref[erows, 0:w], cext_ref[erows, w:2 * w], cext_ref[erows, 2 * w:3 * w],
                                      cin_ref[0, rows, xw:xw + w].astype(F32), dt, a_x, dskip, ng, s_ref, 0, live)
            ybuf_ref[wslot, rows, 2 * w:3 * w] = y.astype(BF16)
            yield

    def gmlp():
        u = jax.nn.gelu(din_ref[0, :, 0:w].astype(F32))
        vr = jax.nn.gelu(din_ref[0, :, w:2 * w].astype(F32))
        yield
        xc = vr - jnp.mean(vr, axis=-1, keepdims=True)
        vn = (xc * lax.rsqrt(jnp.mean(xc * xc, axis=-1, keepdims=True) + EPS) * dng_ref[...]).astype(BF16)
        rows = _iota((N_HEADS * SPAT, SPAT), 0) % SPAT
        wst = jnp.where(rows >= _iota((N_HEADS * SPAT, SPAT), 1), dws_ref[...], 0.0).astype(BF16)
        wcat = jnp.concatenate([wst[g * SPAT:(g + 1) * SPAT] for g in range(N_HEADS)], axis=1)
        lmask = _head_lane_masks(SPAT, w)
        zero = jnp.zeros((SPAT, w), BF16)
        yield
        for i in range(tt // SPAT):
            vi = vn[i * SPAT:(i + 1) * SPAT]
            v_groups = jnp.concatenate([jnp.where(lmask[g], vi, zero) for g in range(N_HEADS)], axis=0)
            yield
            fgate = dbx_ref[...] + jnp.dot(wcat, v_groups, preferred_element_type=F32)
            yield
            ybuf_ref[wslot, i * SPAT:(i + 1) * SPAT, 3 * w:4 * w] = (u[i * SPAT:(i + 1) * SPAT] * fgate).astype(BF16)
            yield

    def post():
        hist = CONV_F - 1
        mod = mod_ref[0]
        mix = jnp.dot(ybuf_ref[rslot], wo_ref[...], preferred_element_type=F32)
        yield
        x1 = x_ref[0] + mod[:, 2 * d:3 * d] * mix
        h2 = _rmsnorm_rows(x1, gffn_ref[...]) * (1.0 + mod[:, 4 * d:5 * d]) + mod[:, 3 * d:4 * d]
        hb = h2.astype(BF16)
        yield
        down = None
        act_b16, act_cols = None, None
        for ci, c0 in enumerate(range(0, f, fchunk)):
            cols = slice(c0, c0 + fchunk)
            g = jnp.dot(hb, wg_ref[:, cols], preferred_element_type=F32)
            u = jnp.dot(hb, wu_ref[:, cols], preferred_element_type=F32)
            if act_b16 is not None:
                p = jnp.dot(act_b16, wd_ref[act_cols, :], preferred_element_type=F32)
                down = p if down is None else down + p
            yield
            buf = ci % 2
            fext_ref[buf, off - hist:off, :] = fhist_ref[0, off - hist:off, cols]
            fext_ref[buf, off:off + tt, :] = g
            acc = fcb_ref[:, cols] + fcw_ref[CONV_F - 1:CONV_F, cols] * g
            for j in range(CONV_F - 1):
                sh = CONV_F - 1 - j
                acc = acc + fcw_ref[j:j + 1, cols] * fext_ref[buf, off - sh:off - sh + tt, :]
            fhist_ref[0, off - hist:off, cols] = g[tt - hist:tt]
            act_b16, act_cols = ((acc * jax.nn.sigmoid(acc)) * u).astype(BF16), cols
            yield
        down = down + jnp.dot(act_b16, wd_ref[act_cols, :], preferred_element_type=F32)
        fc_ref[...] = fhist_ref[:, off - hist:off, :]
        x2 = x1 + mod[:, 5 * d:6 * d] * down
        if final:
            x2 = _rmsnorm_rows(x2, gfin_ref[...])
        o_ref[0] = x2

    _interleave([post(), mlstm(), ssd(), attention(), gmlp()])

    @pl.when(mj == nt - 1)
    def _():
        cout_ref[0] = _blockdiag_to_heads(c_ref[0])


def _mix_post(qkv_a, qkvo_b, c_in, d_in, gates, x, mod, p, g_final, tt, final):
    bsz, t, d = x.shape
    w = d // 4
    nt = t // tt
    n_tiles = bsz * nt
    f = p["f_w_gate"].shape[1]
    xw = c_in.shape[2] - w
    ng, n_c, gp = p["ssm_shape"]
    band = (N_BAND + 1) * CHUNK
    fchunk = 2 * LANE
    assert f % fchunk == 0 and tt % SPAT == 0 and t % tt == 0
    mix_tile = lambda width: pl.BlockSpec(
        (1, tt, width), lambda s: (jnp.minimum(s, n_tiles - 1) // nt, jnp.minimum(s, n_tiles - 1) % nt, 0))
    post_tile = lambda width: pl.BlockSpec(
        (1, tt, width), lambda s: (jnp.maximum(s - 1, 0) // nt, jnp.maximum(s - 1, 0) % nt, 0))
    mix_state = lambda shape: pl.BlockSpec(
        (1,) + shape, lambda s: (jnp.minimum(s, n_tiles - 1) // nt,) + (0,) * len(shape))
    post_state = lambda shape: pl.BlockSpec(
        (1,) + shape, lambda s: (jnp.maximum(s - 1, 0) // nt,) + (0,) * len(shape))
    consts = [p["bias_tab"].reshape(N_HEADS * CHUNK, band), p["bi"], p["bf"], p["b_norm_g"], p["c_conv_w"],
              p["c_conv_b"], p["dtb"], p["a_log_x"], p["dskip"], p["c_norm_g"], p["d_norm_g"], p["d_w_s"], p["d_b_x"],
              p["g_ffn"].reshape(1, d), p["w_out"], p["f_w_gate"], p["f_w_up"], p["f_w_down"], p["f_conv_w"],
              p["f_conv_b"], g_final.reshape(1, d)]
    in_widths = (qkv_a.shape[2], qkvo_b.shape[2], c_in.shape[2], d_in.shape[2], gates.shape[2])
    n_hist = A_WIN // tt
    assert n_hist * tt == A_WIN

    def hist_tile(back, col):
        def index(s):
            m = jnp.minimum(s, n_tiles - 1)
            return (m // nt, jnp.maximum(m % nt - back, 0), col)
        return pl.BlockSpec((1, tt, w), index)

    hist_specs = [hist_tile(back, col) for back in range(n_hist, 0, -1) for col in (1, 2)]
    in_bytes = sum(a.shape[2] * a.dtype.itemsize for a in (qkv_a, qkvo_b, c_in, d_in, gates))
    est = ((d * d + 3 * d * f) * 2 + 2 * tt * in_bytes + 4 * tt * d * 4 + (tt + SUBLANE) * (2 * fchunk + xw) * 4
           + 2 * (A_WIN + tt) * w * 2 + 8 * tt * w * 2 + 12 * tt * d * 4 + 6 * w * w * 4
           + 4 * n_hist * tt * w * qkv_a.dtype.itemsize)
    return pl.pallas_call(
        functools.partial(_mixpost_kernel, nt=nt, final=final, fchunk=fchunk, n_hist=n_hist),
        out_shape=[jax.ShapeDtypeStruct((bsz, t, d), F32),
                   jax.ShapeDtypeStruct((bsz, w, w // N_HEADS), F32),
                   jax.ShapeDtypeStruct((bsz, 1, w), F32),
                   jax.ShapeDtypeStruct((bsz, 1, w), F32),
                   jax.ShapeDtypeStruct((bsz, CONV_C - 1, xw), F32),
                   jax.ShapeDtypeStruct((bsz, ng, n_c, gp), F32),
                   jax.ShapeDtypeStruct((bsz, CONV_F - 1, f), F32)],
        grid=(n_tiles + 1,),
        in_specs=[mix_tile(wd) for wd in in_widths] + [post_tile(d), post_state((1, mod.shape[2]))] + hist_specs
                 + [_const_spec(c.shape) for c in consts],
        out_specs=[post_tile(d), mix_state((w, w // N_HEADS)), mix_state((1, w)), mix_state((1, w)),
                   mix_state((CONV_C - 1, xw)), mix_state((ng, n_c, gp)), post_state((CONV_F - 1, f))],
        scratch_shapes=[pltpu.VMEM((A_WIN + tt, w), BF16), pltpu.VMEM((A_WIN + tt, w), BF16),
                        pltpu.VMEM((2, tt, d), BF16), pltpu.VMEM((tt + SUBLANE, xw), F32),
                        pltpu.VMEM((2, tt + SUBLANE, fchunk), F32), pltpu.VMEM((1, SUBLANE, f), F32),
                        pltpu.VMEM((1, w, w), F32)],
        name="mix_post",
        compiler_params=_params(("arbitrary",), est),
    )(qkv_a, qkvo_b, c_in, d_in, gates, x, mod, *([qkv_a] * len(hist_specs)), *consts)


def _layer(x, mod, st, p, tiles, final, g_final):
    bsz, t, d = x.shape
    gw = d // 4
    dh = gw // N_HEADS
    bb, tt, tmix, nb = tiles
    mod3 = mod.reshape(bsz, 1, mod.shape[1])
    dtypes = (BF16,) * 4 + (F32,) if st is None else (F32,) * 5
    qkv_a, qkvo_b, c_in, d_in, gates = _in_proj(x, mod3, p["g_mix"], p["w_in"], p["widths"], dtypes, bb, tt)

    if st is None:
        x_new, c_new, n_new, m_new, conv_new, ssm_new, fconv_new = _mix_post(
            qkv_a, qkvo_b, c_in, d_in, gates, x, mod3, p, g_final, tmix, final)
        vn = []
    else:
        y_a = _attention_cached(qkv_a, st["a_kt"], st["a_vt"], p["bias_tab"])
        y_b, c_new, n_new, m_new = _mlstm(qkvo_b, gates, p["bi"], p["bf"], p["b_norm_g"], st["b_c"], st["b_n"],
                                          st["b_m"], nb, tmix)
        y_c, conv_new, ssm_new = _ssd(c_in, gates, p["c_conv_w"], p["c_conv_b"], p["dtb"], p["a_log_x"], p["dskip"],
                                      p["c_norm_g"], st["c_conv"], st["c_ssm"], nb, tmix)
        y_d, *vn = _gmlp(d_in, p["d_norm_g"], p["d_w_s"], p["d_b_x"], tmix, True)
        x_new, fconv_new = _post(x, (y_a, y_b, y_c, y_d), mod3, p["g_ffn"], p["w_out"], p["f_w_gate"], p["f_w_up"],
                                 p["f_w_down"], p["f_conv_w"], p["f_conv_b"], st["f_conv"], g_final, bb, tt, final)

    keep = min(A_WIN, t)
    new_k = qkv_a[:, t - keep:, gw:2 * gw].reshape(bsz, keep, N_HEADS, dh).astype(F32)
    new_v = qkv_a[:, t - keep:, 2 * gw:3 * gw].reshape(bsz, keep, N_HEADS, dh).astype(F32)
    c_heads = c_new.reshape(bsz, N_HEADS, dh, dh)
    n_heads = n_new.reshape(bsz, N_HEADS, dh)
    m_heads = m_new[:, 0, ::dh]
    ssm = jnp.swapaxes(ssm_new, 2, 3).reshape(bsz, N_HEADS, gw // N_HEADS, ssm_new.shape[2])
    outs = (new_k, new_v, c_heads, n_heads, m_heads, ssm, conv_new, fconv_new, *vn)
    return x_new, outs


def _prep_layer(l, w_in, w_out, a_rel_bias, b_i_bias, b_f_bias, b_norm_g, c_conv_w, c_conv_b, c_dt_bias, c_a_log,
                c_d_skip, c_norm_g, d_norm_g, d_w_s, d_b_s, f_w_gate, f_w_up, f_conv_w, f_conv_b, f_w_down,
                g_norm_mix, g_norm_ffn):
    d = w_in.shape[1]
    gw = d // 4
    nh = b_i_bias.shape[1]
    xbc_w = c_conv_w.shape[2]
    sizes = (gw,) * 7 + (nh, nh, gw, xbc_w, c_dt_bias.shape[1], gw, gw)
    offs = [0]
    for s in sizes:
        offs.append(offs[-1] + s)
    moves, dst = [], 0
    for i in (0, 1, 2, 3, 4, 5, 6, 10, 9, 12, 13):
        moves.append((offs[i], dst, sizes[i]))
        dst += sizes[i]
    assert sizes[7] == sizes[8] == sizes[11] == N_HEADS
    packed = (dst, ((offs[7], GATE_I, N_HEADS), (offs[8], GATE_F, N_HEADS), (offs[11], GATE_DT, N_HEADS)))
    widths = (3 * gw, 4 * gw, xbc_w + gw, 2 * gw, GATE_W)
    dst += GATE_W
    assert dst == sum(widths)
    heads_x = lambda v: jnp.repeat(v.astype(F32), gw // nh)[None, :]
    return dict(
        w_in=_regroup_cast(w_in, l, moves, packed, dst), widths=widths, g_mix=g_norm_mix[l], g_ffn=g_norm_ffn[l],
        w_out=w_out[l].astype(BF16), bias_tab=_bias_table(a_rel_bias[l]),
        bi=heads_x(b_i_bias[l]), bf=heads_x(b_f_bias[l]), b_norm_g=b_norm_g[l][None, :],
        c_conv_w=c_conv_w[l], c_conv_b=c_conv_b[l][None, :], dtb=heads_x(c_dt_bias[l]),
        a_log_x=heads_x(c_a_log[l]), dskip=heads_x(c_d_skip[l]), c_norm_g=c_norm_g[l][None, :],
        d_norm_g=d_norm_g[l][None, :], d_w_s=d_w_s[l].reshape(-1, d_w_s.shape[-1]),
        d_b_x=jnp.repeat(d_b_s[l].T, gw // d_b_s.shape[1], axis=1),
        f_w_gate=f_w_gate[l].astype(BF16), f_w_up=f_w_up[l].astype(BF16), f_w_down=f_w_down[l].astype(BF16),
        f_conv_w=f_conv_w[l], f_conv_b=f_conv_b[l][None, :],
    )


def kernel(x_prompt, x_sample, c_prompt, c_sample, cache_a_k, cache_a_v, state_b_c, state_b_n, state_b_m, state_c_ssm, state_c_conv, state_ffn_conv, w_ada, b_ada, g_norm_mix, g_norm_ffn, w_in, w_out, a_rel_bias, b_i_bias, b_f_bias, b_norm_g, c_conv_w, c_conv_b, c_dt_bias, c_a_log, c_d_skip, c_norm_g, d_norm_g, d_w_s, d_b_s, f_w_gate, f_w_up, f_conv_w, f_conv_b, f_w_down, g_final):
    depth = w_in.shape[0]
    bp, tp, d = x_prompt.shape
    bs, ts, _ = x_sample.shape
    gw = d // 4
    dh = gw // N_HEADS
    xbc_w = c_conv_w.shape[2]
    n_c = state_c_ssm.shape[-1]
    g_c = (xbc_w - gw) // 2 // n_c

    mod_all = _ada(jnp.concatenate([c_prompt, c_sample], axis=0), w_ada, b_ada)

    nb = max(n for n in (8, 4, 2, 1) if bs % n == 0)
    tiles_p = (1, min(A_WIN, tp), min(A_WIN, tp), 1)
    tiles_s = (bs, ts, ts, nb)

    xp, xs = x_prompt, x_sample
    p_states, s_states = [], []
    for l in range(depth):
        p = _prep_layer(l, w_in, w_out, a_rel_bias, b_i_bias, b_f_bias, b_norm_g, c_conv_w, c_conv_b, c_dt_bias,
                        c_a_log, c_d_skip, c_norm_g, d_norm_g, d_w_s, d_b_s, f_w_gate, f_w_up, f_conv_w, f_conv_b,
                        f_w_down, g_norm_mix, g_norm_ffn)
        p["ssm_shape"] = (g_c, n_c, gw // g_c)
        final = l == depth - 1
        xp, sp = _layer(xp, mod_all[l, :bp], None, p, tiles_p, final, g_final)
        st_s = dict(a_kt=jnp.transpose(cache_a_k[l], (0, 2, 3, 1)).reshape(bs, gw, -1),
                    a_vt=jnp.transpose(cache_a_v[l], (0, 2, 3, 1)).reshape(bs, gw, -1),
                    b_c=state_b_c[l].reshape(bs, gw, dh), b_n=state_b_n[l].reshape(bs, 1, gw),
                    b_m=jnp.repeat(state_b_m[l], dh, axis=1)[:, None, :],
                    c_conv=state_c_conv[l],
                    c_ssm=jnp.swapaxes(state_c_ssm[l].reshape(bs, g_c, gw // g_c, n_c), 2, 3),
                    f_conv=state_ffn_conv[l])
        xs, ss = _layer(xs, mod_all[l, bp:], st_s, p, tiles_s, final, g_final)
        p_states.append(sp)
        s_states.append(ss)

    stack = lambda states, i: jnp.stack([s[i] for s in states])
    return (xp, xs,
            *(stack(p_states, i) for i in range(8)),
            *(stack(s_states, i) for i in range(9)))
```

```python
import functools

import jax
import jax.numpy as jnp
from jax import lax
from jax.experimental import pallas as pl
from jax.experimental.pallas import tpu as pltpu

F32 = jnp.float32
BF16 = jnp.bfloat16

EPS = 1e-6
NEG = -1e30

CHUNK = 64
N_BAND = 8
A_WIN = N_BAND * CHUNK
REL_CLIP = 128
N_HEADS = 4
SPAT = 128
CONV_C = 4
CONV_F = 3
LANE = 128
GATE_I, GATE_F, GATE_DT = 0, 4, 8
GATE_W = LANE
GATE_PIECES = 2
SUBLANE = 8
VMEM_CAP = 64 * 1024 * 1024


def _vmem_limit(nbytes):
    return int(min(max(nbytes, 16 * 1024 * 1024), VMEM_CAP - 8 * 1024 * 1024))


def _params(sem, nbytes):
    return pltpu.CompilerParams(dimension_semantics=sem, vmem_limit_bytes=_vmem_limit(nbytes))


def _const_spec(shape):
    nd = len(shape)
    return pl.BlockSpec(shape, lambda *_: (0,) * nd, pipeline_mode=pl.Buffered(1))


def _iota(shape, dim):
    return lax.broadcasted_iota(jnp.int32, shape, dim)


def _bf16_pieces(x, parts):
    out = []
    r = x
    for i in range(parts):
        hi = r.astype(BF16)
        out.append(hi)
        if i + 1 < parts:
            r = r - hi.astype(F32)
    return out


def _dot_pieces(pieces, e):
    acc = None
    for piece in pieces:
        d = jnp.dot(piece, e, preferred_element_type=F32)
        acc = d if acc is None else acc + d
    return acc


def _dot_pieces_left(e, pieces):
    acc = None
    for piece in pieces:
        d = jnp.dot(e, piece, preferred_element_type=F32)
        acc = d if acc is None else acc + d
    return acc


def _split_dot(x, e, parts):
    return _dot_pieces(_bf16_pieces(x, parts), e)


def _dot_nt(a, b):
    return lax.dot_general(a, b, (((1,), (1,)), ((), ())), preferred_element_type=F32)


def _dot_tn(a, b):
    return lax.dot_general(a, b, (((0,), (0,)), ((), ())), preferred_element_type=F32)


def _head_expander(width, first):
    dh = width // N_HEADS
    return (_iota((LANE, width), 1) // dh == _iota((LANE, width), 0) - first).astype(BF16)


def _head_lane_masks(rows, width):
    dh = width // N_HEADS
    lane = _iota((rows, width), 1)
    return [(lane >= h * dh) & (lane < (h + 1) * dh) for h in range(N_HEADS)]


def _tril(n):
    return _iota((n, n), 0) >= _iota((n, n), 1)


def _pad_rows(x, rows):
    if x.shape[0] == rows:
        return x
    return jnp.concatenate([x, jnp.zeros((rows - x.shape[0], x.shape[1]), x.dtype)], axis=0)


def _interleave(gens):
    results = [None] * len(gens)
    live = list(range(len(gens)))
    while live:
        for i in list(live):
            try:
                next(gens[i])
            except StopIteration as stop:
                results[i] = stop.value
                live.remove(i)
    return results


def _rmsnorm_rows(x, g):
    return x * lax.rsqrt(jnp.mean(x * x, axis=-1, keepdims=True) + EPS) * g


def _regroup_kernel(w_ref, o_ref, *, moves, packed):
    w = w_ref[0]
    rows = w.shape[0]
    for src, dst, n in moves:
        o_ref[:, dst:dst + n] = w[:, src:src + n].astype(o_ref.dtype)
    dst, pieces = packed
    parts, lane = [], 0
    for src, first, n in pieces:
        if first > lane:
            parts.append(jnp.zeros((rows, first - lane), F32))
        parts.append(w[:, src:src + n])
        lane = first + n
    parts.append(jnp.zeros((rows, GATE_W - lane), F32))
    o_ref[:, dst:dst + GATE_W] = jnp.concatenate(parts, axis=1).astype(o_ref.dtype)


def _regroup_cast(w_all, layer, moves, packed, ncols, row_block=256):
    _, r, c = w_all.shape
    return pl.pallas_call(
        functools.partial(_regroup_kernel, moves=tuple(moves), packed=packed),
        out_shape=jax.ShapeDtypeStruct((r, ncols), BF16),
        grid=(r // row_block,),
        in_specs=[pl.BlockSpec((1, row_block, c), lambda i: (layer, i, 0))],
        out_specs=pl.BlockSpec((row_block, ncols), lambda i: (i, 0)),
        name="regroup_cast",
        compiler_params=_params(("parallel",), 2 * row_block * (c * 4 + ncols * 2) + row_block * c * 4),
    )(w_all)


def _ada_kernel(c_ref, w_ref, b_ref, o_ref):
    c = c_ref[...]
    h = (c * jax.nn.sigmoid(c)).astype(BF16)
    o_ref[0] = jnp.dot(h, w_ref[0].astype(BF16), preferred_element_type=F32) + b_ref[0]


def _ada(c_all, w_ada, b_ada):
    depth, d, n6 = w_ada.shape
    r = c_all.shape[0]
    tn = d
    return pl.pallas_call(
        _ada_kernel,
        out_shape=jax.ShapeDtypeStruct((depth, r, n6), F32),
        grid=(depth, n6 // tn),
        in_specs=[pl.BlockSpec((r, d), lambda l, j: (0, 0)),
                  pl.BlockSpec((1, d, tn), lambda l, j: (l, 0, j)),
                  pl.BlockSpec((1, 1, tn), lambda l, j: (l, 0, j))],
        out_specs=pl.BlockSpec((1, r, tn), lambda l, j: (l, 0, j)),
        name="ada_mod",
        compiler_params=_params(("parallel", "parallel"), 4 * (2 * d * tn * 4 + 2 * r * tn * 4 + r * d * 4)),
    )(c_all, w_ada, b_ada.reshape(depth, 1, n6))


def _bias_kernel(rb_ref, o_ref, *, lo, hi):
    nh, lq, lk = o_ref.shape
    idx = jnp.clip(A_WIN + _iota((lq, lk), 0) - _iota((lq, lk), 1), -REL_CLIP, REL_CLIP) + REL_CLIP
    for h in range(nh):
        def body(r, acc, h=h):
            return jnp.where(idx == r, rb_ref[h, r], acc)
        o_ref[h] = lax.fori_loop(lo, hi + 1, body, jnp.zeros((lq, lk), F32))


def _bias_table(rel_bias):
    nh = rel_bias.shape[0]
    band = (N_BAND + 1) * CHUNK
    lo = max(A_WIN - (band - 1), -REL_CLIP) + REL_CLIP
    hi = min(A_WIN + CHUNK - 1, REL_CLIP) + REL_CLIP
    return pl.pallas_call(
        functools.partial(_bias_kernel, lo=lo, hi=hi),
        out_shape=jax.ShapeDtypeStruct((nh, CHUNK, band), F32),
        in_specs=[pl.BlockSpec(memory_space=pltpu.SMEM)],
        out_specs=pl.BlockSpec(memory_space=pltpu.VMEM),
        name="rel_bias_table",
    )(rel_bias)


def _in_kernel(x_ref, mod_ref, g_ref, w_ref, *o_refs, col_starts):
    bb, tt, d = x_ref.shape
    x = x_ref[...]
    mod = mod_ref[...]
    h = _rmsnorm_rows(x, g_ref[...]) * (1.0 + mod[:, :, d:2 * d]) + mod[:, :, 0:d]
    hb = h.reshape(bb * tt, d).astype(BF16)
    for o_ref, (a, b) in zip(o_refs, col_starts):
        o_ref[...] = jnp.dot(hb, w_ref[:, a:b], preferred_element_type=F32).reshape(bb, tt, b - a).astype(o_ref.dtype)


def _in_proj(x, mod, g, w, widths, dtypes, bb, tt):
    bsz, t, d = x.shape
    ncols = w.shape[1]
    starts, a = [], 0
    for wd in widths:
        starts.append((a, a + wd))
        a += wd
    m = bb * tt
    est = 2 * m * d * 4 + 2 * d * ncols * 2 + 2 * m * ncols * 4 + 3 * m * d * 4 + m * max(widths) * 4
    return pl.pallas_call(
        functools.partial(_in_kernel, col_starts=tuple(starts)),
        out_shape=[jax.ShapeDtypeStruct((bsz, t, wd), dt) for wd, dt in zip(widths, dtypes)],
        grid=(bsz // bb, t // tt),
        in_specs=[pl.BlockSpec((bb, tt, d), lambda i, j: (i, j, 0)),
                  pl.BlockSpec((bb, 1, mod.shape[2]), lambda i, j: (i, 0, 0)),
                  _const_spec((1, 1, d)),
                  _const_spec((d, ncols))],
        out_specs=[pl.BlockSpec((bb, tt, wd), lambda i, j: (i, j, 0)) for wd in widths],
        name="in_proj",
        compiler_params=_params(("parallel", "parallel"), est),
    )(x, mod, g.reshape(1, 1, d), w)


def _attend_chunk(qc, kb, vb, bias, kvalid, masks_q, masks_o):
    lq = qc.shape[0]
    qs = jnp.concatenate([jnp.where(mk, qc, 0.0) for mk in masks_q], axis=0).astype(BF16)
    yield
    s = _dot_nt(qs, kb) + bias
    yield
    if kvalid is not None:
        s = jnp.where(kvalid, s, NEG)
    e = jnp.exp(s - jnp.max(s, axis=-1, keepdims=True))
    eb = e.astype(BF16)
    rinv = 1.0 / jnp.sum(e, axis=-1, keepdims=True)
    yield
    o = jnp.dot(eb, vb, preferred_element_type=F32)
    yield
    o = o * rinv
    out = jnp.where(masks_o[0], o[0:lq], 0.0)
    for h in range(1, N_HEADS):
        out = out + jnp.where(masks_o[h], o[h * lq:(h + 1) * lq], 0.0)
    return out


def _attn_cached_kernel(q_ref, k_ref, v_ref, kt_ref, vt_ref, bias_ref, y_ref):
    tq, w = q_ref.shape[1], q_ref.shape[2]
    scale = (w // N_HEADS) ** -0.5
    masks = _head_lane_masks(CHUNK, w)
    qc = _pad_rows(q_ref[0], CHUNK) * scale
    qs = jnp.concatenate([jnp.where(mk, qc, 0.0) for mk in masks], axis=0).astype(BF16)
    k_new = _pad_rows(k_ref[0], CHUNK).astype(BF16)
    v_new = _pad_rows(v_ref[0], CHUNK).astype(BF16)
    s = jnp.concatenate([jnp.dot(qs, kt_ref[0, 0].astype(BF16), preferred_element_type=F32), _dot_nt(qs, k_new)],
                        axis=1) + bias_ref[...]
    s = jnp.where(_iota((1, A_WIN + CHUNK), 1) < A_WIN + tq, s, NEG)
    e = jnp.exp(s - jnp.max(s, axis=-1, keepdims=True))
    eb = e.astype(BF16)
    o = _dot_nt(eb[:, 0:A_WIN], vt_ref[0, 0].astype(BF16)) + jnp.dot(eb[:, A_WIN:], v_new, preferred_element_type=F32)
    o = o * (1.0 / jnp.sum(e, axis=-1, keepdims=True))
    out = jnp.where(masks[0], o[0:CHUNK], 0.0)
    for h in range(1, N_HEADS):
        out = out + jnp.where(masks[h], o[h * CHUNK:(h + 1) * CHUNK], 0.0)
    y_ref[0] = out[0:tq].astype(y_ref.dtype)


def _attention_cached(qkv, kt_hist, vt_hist, layer, bias_tab):
    bsz, t, w3 = qkv.shape
    w = w3 // 3
    band = (N_BAND + 1) * CHUNK
    assert t <= CHUNK
    est = 2 * (3 * t * w * 4 + 2 * w * A_WIN * 4) + 12 * N_HEADS * CHUNK * band * 4
    return pl.pallas_call(
        _attn_cached_kernel,
        out_shape=jax.ShapeDtypeStruct((bsz, t, w), BF16),
        grid=(bsz,),
        in_specs=[pl.BlockSpec((1, t, w), lambda b: (b, 0, 0)),
                  pl.BlockSpec((1, t, w), lambda b: (b, 0, 1)),
                  pl.BlockSpec((1, t, w), lambda b: (b, 0, 2)),
                  pl.BlockSpec((1, 1, w, A_WIN), lambda b: (layer, b, 0, 0)),
                  pl.BlockSpec((1, 1, w, A_WIN), lambda b: (layer, b, 0, 0)),
                  _const_spec((N_HEADS * CHUNK, band))],
        out_specs=pl.BlockSpec((1, t, w), lambda b: (b, 0, 0)),
        name="band_attention_cached",
        compiler_params=_params(("parallel",), est),
    )(qkv, qkv, qkv, kt_hist, vt_hist, bias_tab.reshape(N_HEADS * CHUNK, band))


def _mlstm_chunk(q, k, v, og, gi, gf, normg, c_ref, n_ref, m_ref, bi, n_valid, live=None):
    L, w = q.shape
    dh = w // N_HEADS
    lmask = _head_lane_masks(L, w)
    tril = _tril(L)
    blockdiag = (_iota((w, w), 0) // dh) == (_iota((w, w), 1) // dh)
    bd = blockdiag.astype(BF16)

    m_prev = m_ref[bi]
    c_old = c_ref[bi]
    n_old = n_ref[bi]
    qb16 = q.astype(BF16)
    kb16 = k.astype(BF16)
    vb16 = v.astype(BF16)
    cb16 = c_old.astype(BF16)
    qmask = [jnp.where(lmask[h], q, 0.0).astype(BF16) for h in range(N_HEADS)]
    lf_p = _bf16_pieces(jax.nn.log_sigmoid(gf), 3)
    qn_p = _bf16_pieces(q * n_old, 2)
    yield
    b = _dot_pieces_left(tril.astype(BF16), lf_p)
    q_c = jnp.dot(qb16, cb16, preferred_element_type=F32)
    q_n = _dot_pieces(qn_p, bd)
    yield
    qk = [_dot_nt(qmask[h], kb16) for h in range(N_HEADS)]
    u = gi - b
    u_t = u.T
    yield
    cm = jnp.zeros((L, w), F32)
    for h in range(N_HEADS):
        cmh = jnp.max(jnp.where(tril, u_t[h * dh:h * dh + 1, :], NEG), axis=1, keepdims=True)
        cm = jnp.where(lmask[h], cmh, cm)
    mx = jnp.maximum(m_prev, cm)
    m_t = b + mx
    inter = jnp.exp(m_prev - mx)
    last = n_valid - 1
    m_last = m_t[last:last + 1, :]
    b_last = b[last:last + 1, :]
    decay = jnp.exp(b_last + m_prev - m_last)
    ws = jnp.exp(u + (b_last - m_last))
    if n_valid < L:
        ws = jnp.where(_iota((L, w), 0) < n_valid, ws, 0.0)
    kw = k * ws
    kwb16 = kw.astype(BF16)
    yield
    v_heads = jnp.concatenate([jnp.where(lmask[h], v, 0.0).astype(BF16) for h in range(N_HEADS)], axis=0)
    rs = jnp.zeros((L, w), F32)
    wq = []
    for h in range(N_HEADS):
        arg = jnp.where(tril, u_t[h * dh:h * dh + 1, :] - mx[:, h * dh:h * dh + 1], NEG)
        wqk = jnp.exp(arg) * qk[h]
        rs = jnp.where(lmask[h], jnp.sum(wqk, axis=1, keepdims=True), rs)
        wq.append(wqk.astype(BF16))
        yield
    num = jnp.dot(jnp.concatenate(wq, axis=1), v_heads, preferred_element_type=F32)
    upd = _dot_tn(kwb16, vb16)
    den = inter * q_n + rs
    hout = (inter * q_c + num) / jnp.maximum(jnp.abs(den), jnp.exp(-m_t))
    mu_p = _bf16_pieces(hout, 2)
    yield
    xc = hout - _dot_pieces(mu_p, bd) * (1.0 / dh)
    var_p = _bf16_pieces(xc * xc, 2)
    yield
    var = _dot_pieces(var_p, bd) * (1.0 / dh)
    yield
    y = jax.nn.sigmoid(og) * (xc * lax.rsqrt(var + EPS) * normg)
    keep = (lambda new, old: new) if live is None else (lambda new, old: jnp.where(live, new, old))
    c_ref[bi] = keep(decay * c_old + jnp.where(blockdiag, upd, 0.0), c_old)
    n_ref[bi] = keep(decay * n_old + jnp.sum(kw, axis=0, keepdims=True), n_old)
    m_ref[bi] = keep(m_last, m_prev)
    return y


def _heads_to_blockdiag(c):
    w, dh = c.shape
    tile = (_iota((dh, w), 1) % dh == _iota((dh, w), 0)).astype(BF16)
    blockdiag = (_iota((w, w), 0) // dh) == (_iota((w, w), 1) // dh)
    return jnp.where(blockdiag, _split_dot(c, tile, 3), 0.0)


def _blockdiag_to_heads(c_bd):
    w = c_bd.shape[0]
    dh = w // N_HEADS
    fold = (_iota((w, dh), 0) % dh == _iota((w, dh), 1)).astype(BF16)
    return _split_dot(c_bd, fold, 3)


def _mlstm_kernel(q_ref, k_ref, v_ref, o_ref, g_ref, bi_ref, bf_ref, ng_ref,
                  c0_ref, n0_ref, m0_ref, y_ref, cout_ref, n_ref, m_ref, c_ref):
    nb, tt, w = q_ref.shape
    kscale = (w // N_HEADS) ** -0.5

    @pl.when(pl.program_id(1) == 0)
    def _():
        for bi in range(nb):
            c_ref[bi] = _heads_to_blockdiag(c0_ref[bi])
        n_ref[...] = n0_ref[...]
        m_ref[...] = m0_ref[...]

    bias_i = bi_ref[...]
    bias_f = bf_ref[...]
    ng = ng_ref[...]
    ex_i, ex_f = _head_expander(w, GATE_I), _head_expander(w, GATE_F)

    def chunk(bi, rows, n_valid):
        pad = lambda r: _pad_rows(r[bi, rows, :], SPAT)
        g = pad(g_ref)
        return _mlstm_chunk(pad(q_ref), pad(k_ref) * kscale, pad(v_ref), pad(o_ref),
                            _split_dot(g, ex_i, 3) + bias_i, _split_dot(g, ex_f, 3) + bias_f,
                            ng, c_ref, n_ref, m_ref, bi, n_valid)

    if tt < SPAT:
        for bi, y in enumerate(_interleave([chunk(bi, slice(None), tt) for bi in range(nb)])):
            y_ref[bi] = y[0:tt].astype(y_ref.dtype)
    else:
        def body(i, carry):
            rows = pl.ds(pl.multiple_of(i * SPAT, SPAT), SPAT)
            for bi, y in enumerate(_interleave([chunk(bi, rows, SPAT) for bi in range(nb)])):
                y_ref[bi, rows, :] = y.astype(y_ref.dtype)
            return carry
        lax.fori_loop(0, tt // SPAT, body, 0)

    @pl.when(pl.program_id(1) == pl.num_programs(1) - 1)
    def _():
        for bi in range(nb):
            cout_ref[bi] = _blockdiag_to_heads(c_ref[bi])


def _mlstm(qkvo, gates, bias_i, bias_f, normg, c0, n0, m0, nb, tt):
    bsz, t, w4 = qkvo.shape
    w = w4 // 4
    dh = w // N_HEADS
    blk = lambda c: pl.BlockSpec((nb, tt, w), lambda b, j: (b, j, c))
    st = lambda shape: pl.BlockSpec((nb,) + shape, lambda b, j: (b, 0, 0))
    est = nb * (2 * 5 * tt * w * 4 + 4 * tt * LANE * 4 + 6 * w * w * 4 + 40 * SPAT * w * 4 + 24 * SPAT * SPAT * 4)
    return pl.pallas_call(
        _mlstm_kernel,
        out_shape=[jax.ShapeDtypeStruct((bsz, t, w), BF16),
                   jax.ShapeDtypeStruct((bsz, w, dh), F32),
                   jax.ShapeDtypeStruct((bsz, 1, w), F32),
                   jax.ShapeDtypeStruct((bsz, 1, w), F32)],
        grid=(bsz // nb, t // tt),
        in_specs=[blk(0), blk(1), blk(2), blk(3), pl.BlockSpec((nb, tt, GATE_W), lambda b, j: (b, j, 0)),
                  _const_spec((1, w)), _const_spec((1, w)), _const_spec((1, w)),
                  st((w, dh)), st((1, w)), st((1, w))],
        out_specs=[pl.BlockSpec((nb, tt, w), lambda b, j: (b, j, 0)), st((w, dh)), st((1, w)), st((1, w))],
        scratch_shapes=[pltpu.VMEM((nb, w, w), F32)],
        name="mlstm",
        compiler_params=_params(("parallel", "arbitrary"), est),
    )(qkvo, qkvo, qkvo, qkvo, gates, bias_i, bias_f, normg, c0, n0, m0)


def _ssd_chunk(xs, bm, cm, z, dt, a_x, dskip, normg, s_ref, bi, live=None):
    L, w = xs.shape
    dh = w // N_HEADS
    ng = s_ref.shape[1]
    gw = w // ng
    lmask = _head_lane_masks(L, w)
    tril = _tril(L)
    s_old = [s_ref[bi, g] for g in range(ng)]
    sb16 = [s.astype(BF16) for s in s_old]
    cmb = cm.astype(BF16)
    bmb = bm.astype(BF16)
    xdt = (xs * dt).astype(BF16)
    da_p = _bf16_pieces(dt * a_x, 3)
    yield
    cs = _dot_pieces_left(tril.astype(BF16), da_p)
    cb = [_dot_nt(cmb[:, g * gw:(g + 1) * gw], bmb[:, g * gw:(g + 1) * gw]) for g in range(ng)]
    y_in = jnp.concatenate([jnp.dot(cmb[:, g * gw:(g + 1) * gw], sb16[g], preferred_element_type=F32)
                            for g in range(ng)], axis=1)
    yield
    cs_t = cs.T
    cs_last = cs[L - 1:L, :]
    wl = jnp.exp(cs_last - cs) * dt
    wx = (xs * wl).astype(BF16)
    yield
    x_heads = jnp.concatenate([jnp.where(lmask[h], xdt, jnp.zeros_like(xdt)) for h in range(N_HEADS)], axis=0)
    mh = []
    for h in range(N_HEADS):
        dec = jnp.exp(jnp.where(tril, cs[:, h * dh:h * dh + 1] - cs_t[h * dh:h * dh + 1, :], NEG))
        mh.append((cb[h * ng // N_HEADS] * dec).astype(BF16))
        yield
    y = jnp.exp(cs) * y_in + jnp.dot(jnp.concatenate(mh, axis=1), x_heads, preferred_element_type=F32)
    yield
    dec_x = jnp.exp(cs_last)
    s_new = [dec_x[:, g * gw:(g + 1) * gw] * s_old[g] + _dot_tn(bmb[:, g * gw:(g + 1) * gw], wx[:, g * gw:(g + 1) * gw])
             for g in range(ng)]
    yield
    yc = y + dskip * xs
    out = _rmsnorm_rows(yc * (z * jax.nn.sigmoid(z)), normg)
    for g in range(ng):
        s_ref[bi, g] = s_new[g] if live is None else jnp.where(live, s_new[g], s_old[g])
    return out


def _ssd_kernel(xbc_ref, z_ref, gd_ref, cw_ref, cb_ref, dtb_ref, a_ref, dskip_ref, ng_ref, conv0_ref, s0_ref,
                y_ref, conv_ref, s_ref, ext_ref):
    nb, tt, w = z_ref.shape
    hist = CONV_C - 1
    off = SUBLANE

    @pl.when(pl.program_id(1) == 0)
    def _():
        s_ref[...] = s0_ref[...]
        ext_ref[:, off - hist:off, :] = conv0_ref[...]

    cw = cw_ref[...]
    for bi in range(nb):
        ext_ref[bi, off:off + tt, :] = xbc_ref[bi]
        acc = cb_ref[...] + cw[CONV_C - 1:CONV_C, :] * ext_ref[bi, off:off + tt, :]
        for j in range(CONV_C - 1):
            sh = CONV_C - 1 - j
            acc = acc + cw[j:j + 1, :] * ext_ref[bi, off - sh:off - sh + tt, :]
        tail = ext_ref[bi, off + tt - hist:off + tt, :]
        ext_ref[bi, off - hist:off, :] = tail
        conv_ref[bi] = tail
        ext_ref[bi, off:off + tt, :] = acc * jax.nn.sigmoid(acc)

    dtb = dtb_ref[...]
    a_x = -jnp.exp(a_ref[...])
    dskip = dskip_ref[...]
    ng = ng_ref[...]
    ex = _head_expander(w, GATE_DT)

    def chunk(bi, ext_rows, rows, n_valid):
        act = _pad_rows(ext_ref[bi, ext_rows, :], SPAT)
        dt = jax.nn.softplus(_split_dot(_pad_rows(gd_ref[bi, rows, :], SPAT), ex, 3) + dtb)
        if n_valid < SPAT:
            dt = jnp.where(_iota((SPAT, w), 0) < n_valid, dt, 0.0)
        return _ssd_chunk(act[:, 0:w], act[:, w:2 * w], act[:, 2 * w:3 * w], _pad_rows(z_ref[bi, rows, :], SPAT),
                          dt, a_x, dskip, ng, s_ref, bi)

    if tt < SPAT:
        gens = [chunk(bi, slice(off, off + tt), slice(None), tt) for bi in range(nb)]
        for bi, y in enumerate(_interleave(gens)):
            y_ref[bi] = y[0:tt].astype(y_ref.dtype)
    else:
        def body(i, carry):
            start = pl.multiple_of(i * SPAT, SPAT)
            ext_rows = pl.ds(pl.multiple_of(start + off, SUBLANE), SPAT)
            gens = [chunk(bi, ext_rows, pl.ds(start, SPAT), SPAT) for bi in range(nb)]
            for bi, y in enumerate(_interleave(gens)):
                y_ref[bi, pl.ds(start, SPAT), :] = y.astype(y_ref.dtype)
            return carry
        lax.fori_loop(0, tt // SPAT, body, 0)


def _ssd(cin, gates, conv_w, conv_b, dtb, a_log_x, dskip, normg, conv0, s0, nb, tt):
    bsz, t, wtot = cin.shape
    w = dskip.shape[1]
    xw = wtot - w
    ng, n_c, gp = s0.shape[1:]
    st = lambda shape: pl.BlockSpec((nb,) + shape, lambda b, j: (b,) + (0,) * len(shape))
    est = nb * (2 * 2 * tt * wtot * 4 + 2 * tt * w * 4 + (tt + SUBLANE) * xw * 4 + 3 * tt * xw * 4 + 40 * SPAT * w * 4)
    return pl.pallas_call(
        _ssd_kernel,
        out_shape=[jax.ShapeDtypeStruct((bsz, t, w), BF16),
                   jax.ShapeDtypeStruct((bsz, CONV_C - 1, xw), F32),
                   jax.ShapeDtypeStruct(s0.shape, F32)],
        grid=(bsz // nb, t // tt),
        in_specs=[pl.BlockSpec((nb, tt, xw), lambda b, j: (b, j, 0)),
                  pl.BlockSpec((nb, tt, w), lambda b, j: (b, j, xw // w)),
                  pl.BlockSpec((nb, tt, GATE_W), lambda b, j: (b, j, 0)),
                  _const_spec((CONV_C, xw)), _const_spec((1, xw)), _const_spec((1, w)), _const_spec((1, w)),
                  _const_spec((1, w)), _const_spec((1, w)),
                  st((CONV_C - 1, xw)), st((ng, n_c, gp))],
        out_specs=[pl.BlockSpec((nb, tt, w), lambda b, j: (b, j, 0)), st((CONV_C - 1, xw)), st((ng, n_c, gp))],
        scratch_shapes=[pltpu.VMEM((nb, tt + SUBLANE, xw), F32)],
        name="ssd",
        compiler_params=_params(("parallel", "arbitrary"), est),
    )(cin, cin, gates, conv_w, conv_b, dtb, a_log_x, dskip, normg, conv0, s0)


def _gmlp_kernel(u_ref, v_ref, ng_ref, ws_ref, bs_ref, y_ref, *vn_refs):
    tt, w = u_ref.shape[1], u_ref.shape[2]
    u = jax.nn.gelu(u_ref[0])
    vr = jax.nn.gelu(v_ref[0])
    mu = jnp.mean(vr, axis=-1, keepdims=True)
    xc = vr - mu
    vn = xc * lax.rsqrt(jnp.mean(xc * xc, axis=-1, keepdims=True) + EPS) * ng_ref[...]
    for vn_ref in vn_refs:
        vn_ref[0] = vn
    rows = _iota((N_HEADS * SPAT, SPAT), 0) % SPAT
    wst = jnp.where(rows >= _iota((N_HEADS * SPAT, SPAT), 1), ws_ref[...], 0.0).astype(BF16)
    lmask = _head_lane_masks(SPAT, w)
    bias = bs_ref[...]
    vpad = _pad_rows(vn, -(-tt // SPAT) * SPAT).astype(BF16)
    for i in range(vpad.shape[0] // SPAT):
        fs = jnp.dot(wst, vpad[i * SPAT:(i + 1) * SPAT], preferred_element_type=F32)
        f = bias
        for g in range(N_HEADS):
            f = f + jnp.where(lmask[g], fs[g * SPAT:(g + 1) * SPAT], 0.0)
        n = min(SPAT, tt - i * SPAT)
        y_ref[0, i * SPAT:i * SPAT + n, :] = (u[i * SPAT:i * SPAT + n] * f[0:n]).astype(y_ref.dtype)


def _gmlp(din, normg, w_s, b_x, tt, emit_v):
    bsz, t, w2 = din.shape
    w = w2 // 2
    est = 2 * 4 * tt * w * 4 + 8 * tt * w * 4 + 4 * N_HEADS * SPAT * (SPAT + w) * 4
    out_shape = [jax.ShapeDtypeStruct((bsz, t, w), BF16)]
    if emit_v:
        out_shape.append(jax.ShapeDtypeStruct((bsz, t, w), F32))
    return pl.pallas_call(
        _gmlp_kernel,
        out_shape=out_shape,
        grid=(bsz, t // tt),
        in_specs=[pl.BlockSpec((1, tt, w), lambda b, j: (b, j, 0)),
                  pl.BlockSpec((1, tt, w), lambda b, j: (b, j, 1)),
                  _const_spec((1, w)), _const_spec((N_HEADS * SPAT, SPAT)), _const_spec((SPAT, w))],
        out_specs=[pl.BlockSpec((1, tt, w), lambda b, j: (b, j, 0))] * len(out_shape),
        name="spatial_gate",
        compiler_params=_params(("parallel", "parallel"), est),
    )(din, din, normg, w_s, b_x)


def _post_kernel(x_ref, ya_ref, yb_ref, yc_ref, yd_ref, mod_ref, g_ref, wo_ref, wg_ref, wu_ref, wd_ref,
                 cw_ref, cb_ref, f0_ref, gfin_ref, o_ref, fc_ref, ext_ref, *, final, fchunk):
    bb, tt, d = x_ref.shape
    m = bb * tt
    gw = ya_ref.shape[2]
    f = wg_ref.shape[1]
    hist = CONV_F - 1
    off = SUBLANE
    mod = mod_ref[...]
    mix = None
    for j, y_ref in enumerate((ya_ref, yb_ref, yc_ref, yd_ref)):
        p = jnp.dot(y_ref[...].reshape(m, gw), wo_ref[j * gw:(j + 1) * gw, :], preferred_element_type=F32)
        mix = p if mix is None else mix + p
    x1 = x_ref[...] + mod[:, :, 2 * d:3 * d] * mix.reshape(bb, tt, d)
    h2 = _rmsnorm_rows(x1, g_ref[...]) * (1.0 + mod[:, :, 4 * d:5 * d]) + mod[:, :, 3 * d:4 * d]
    hb = h2.reshape(m, d).astype(BF16)

    @pl.when(pl.program_id(1) == 0)
    def _():
        ext_ref[:, off - hist:off, :] = f0_ref[...]

    down = None
    for c0 in range(0, f, fchunk):
        cols = slice(c0, c0 + fchunk)
        g = jnp.dot(hb, wg_ref[:, cols], preferred_element_type=F32).reshape(bb, tt, fchunk)
        u = jnp.dot(hb, wu_ref[:, cols], preferred_element_type=F32)
        ext_ref[:, off:off + tt, cols] = g
        acc = cb_ref[:, cols] + cw_ref[CONV_F - 1:CONV_F, cols] * g
        for j in range(CONV_F - 1):
            sh = CONV_F - 1 - j
            acc = acc + cw_ref[j:j + 1, cols] * ext_ref[:, off - sh:off - sh + tt, cols]
        act = (acc * jax.nn.sigmoid(acc)).reshape(m, fchunk) * u
        p = jnp.dot(act.astype(BF16), wd_ref[cols, :], preferred_element_type=F32)
        down = p if down is None else down + p
    tail = ext_ref[:, off + tt - hist:off + tt, :]
    ext_ref[:, off - hist:off, :] = tail
    fc_ref[...] = tail
    x2 = x1 + mod[:, :, 5 * d:6 * d] * down.reshape(bb, tt, d)
    if final:
        x2 = _rmsnorm_rows(x2, gfin_ref[...])
    o_ref[...] = x2


def _post(x, ys, mod, g_ffn, w_out, w_gate, w_up, w_down, conv_w, conv_b, f0, g_final, bb, tt, final):
    bsz, t, d = x.shape
    gw = ys[0].shape[2]
    f = w_gate.shape[1]
    m = bb * tt
    tok = lambda wd: pl.BlockSpec((bb, tt, wd), lambda i, j: (i, j, 0))
    fchunk = 2 * LANE
    assert f % fchunk == 0
    est = (2 * 2 * m * d * 4 + 2 * 4 * m * gw * 2 + (d * d + 3 * d * f) * 2 + bb * (tt + SUBLANE) * f * 4
           + 8 * m * fchunk * 4 + 6 * m * d * 4)
    return pl.pallas_call(
        functools.partial(_post_kernel, final=final, fchunk=fchunk),
        out_shape=[jax.ShapeDtypeStruct((bsz, t, d), F32), jax.ShapeDtypeStruct((bsz, CONV_F - 1, f), F32)],
        grid=(bsz // bb, t // tt),
        in_specs=[tok(d), tok(gw), tok(gw), tok(gw), tok(gw),
                  pl.BlockSpec((bb, 1, mod.shape[2]), lambda i, j: (i, 0, 0)),
                  _const_spec((1, 1, d)), _const_spec((d, d)), _const_spec((d, f)), _const_spec((d, f)),
                  _const_spec((f, d)), _const_spec((CONV_F, f)), _const_spec((1, f)),
                  pl.BlockSpec((bb, CONV_F - 1, f), lambda i, j: (i, 0, 0)),
                  _const_spec((1, 1, d))],
        out_specs=[tok(d), pl.BlockSpec((bb, CONV_F - 1, f), lambda i, j: (i, 0, 0))],
        scratch_shapes=[pltpu.VMEM((bb, tt + SUBLANE, f), F32)],
        name="post_ffn",
        compiler_params=_params(("parallel", "arbitrary"), est),
    )(x, *ys, mod, g_ffn.reshape(1, 1, d), w_out, w_gate, w_up, w_down, conv_w, conv_b, f0, g_final.reshape(1, 1, d))


def _mixpost_kernel(qkv_ref, qkvo_ref, cin_ref, din_ref, gate_ref, x_ref, mod_ref, *refs, nt, final, fchunk, n_hist):
    hist_refs = refs[:2 * n_hist]
    (bias_ref, bi_ref, bf_ref, bng_ref, ccw_ref, ccb_ref, dtb_ref, alog_ref, dskip_ref, cng_ref,
     dng_ref, dws_ref, dbx_ref, gffn_ref, wo_ref, wg_ref, wu_ref, wd_ref, fcw_ref, fcb_ref, gfin_ref,
     o_ref, cout_ref, n_ref, m_ref, conv_ref, s_ref, fc_ref,
     kk_ref, vv_ref, ybuf_ref, cext_ref, fext_ref, fhist_ref, c_ref) = refs[2 * n_hist:]
    tt, d = x_ref.shape[1], x_ref.shape[2]
    w = d // 4
    f = wg_ref.shape[1]
    band = (N_BAND + 1) * CHUNK
    off = SUBLANE
    s = pl.program_id(0)
    n_tiles = pl.num_programs(0) - 1
    live = s < n_tiles
    mj = jnp.minimum(s, n_tiles - 1) % nt
    pj = jnp.maximum(s - 1, 0) % nt
    wslot = s % 2
    rslot = 1 - wslot

    @pl.when(s == 0)
    def _():
        ybuf_ref[...] = jnp.zeros(ybuf_ref.shape, ybuf_ref.dtype)

    @pl.when(mj == 0)
    def _():
        c_ref[...] = jnp.zeros(c_ref.shape, F32)
        n_ref[...] = jnp.zeros(n_ref.shape, F32)
        m_ref[...] = jnp.zeros(m_ref.shape, F32)
        s_ref[...] = jnp.zeros(s_ref.shape, F32)
        cext_ref[0:off, :] = jnp.zeros((off, cext_ref.shape[1]), F32)

    @pl.when(pj == 0)
    def _():
        fhist_ref[...] = jnp.zeros(fhist_ref.shape, F32)

    ex_i, ex_f, ex_dt = (_head_expander(w, first) for first in (GATE_I, GATE_F, GATE_DT))

    def attention():
        scale = (w // N_HEADS) ** -0.5
        for i, (kh_ref, vh_ref) in enumerate(zip(hist_refs[0::2], hist_refs[1::2])):
            kk_ref[i * tt:(i + 1) * tt, :] = kh_ref[0].astype(BF16)
            vv_ref[i * tt:(i + 1) * tt, :] = vh_ref[0].astype(BF16)
        kk_ref[A_WIN:A_WIN + tt, :] = qkv_ref[0, :, w:2 * w].astype(BF16)
        vv_ref[A_WIN:A_WIN + tt, :] = qkv_ref[0, :, 2 * w:3 * w].astype(BF16)
        masks = _head_lane_masks(CHUNK, w)
        bias = bias_ref[...]
        slot = _iota((1, band), 1)
        q_all = qkv_ref[0, :, 0:w].astype(F32) * scale
        yield
        for i in range(tt // CHUNK):
            base = i * CHUNK
            pos = slot + base
            kvalid = pos >= A_WIN
            for back in range(1, A_WIN // tt + 1):
                kvalid = kvalid | ((pos >= A_WIN - back * tt) & (mj >= back))
            out = yield from _attend_chunk(q_all[base:base + CHUNK], kk_ref[base:base + band, :],
                                           vv_ref[base:base + band, :], bias, kvalid, masks, masks)
            ybuf_ref[wslot, base:base + CHUNK, 0:w] = out.astype(BF16)
            yield

    def mlstm():
        kscale = (w // N_HEADS) ** -0.5
        bias_i, bias_f, ng = bi_ref[...], bf_ref[...], bng_ref[...]
        for c in range(tt // SPAT):
            rows = slice(c * SPAT, (c + 1) * SPAT)
            col = lambda c: qkvo_ref[0, rows, c * w:(c + 1) * w].astype(F32)
            g_p = _bf16_pieces(gate_ref[0, rows, :], GATE_PIECES)
            yield
            y = yield from _mlstm_chunk(
                col(0), col(1) * kscale, col(2), col(3),
                _dot_pieces(g_p, ex_i) + bias_i, _dot_pieces(g_p, ex_f) + bias_f,
                ng, c_ref, n_ref, m_ref, 0, SPAT, live)
            ybuf_ref[wslot, rows, w:2 * w] = y.astype(BF16)
            yield

    def ssd():
        xw = cext_ref.shape[1]
        hist = CONV_C - 1
        cext_ref[off:off + tt, :] = cin_ref[0, :, 0:xw].astype(F32)
        acc = ccb_ref[...] + ccw_ref[CONV_C - 1:CONV_C, :] * cext_ref[off:off + tt, :]
        for j in range(CONV_C - 1):
            sh = CONV_C - 1 - j
            acc = acc + ccw_ref[j:j + 1, :] * cext_ref[off - sh:off - sh + tt, :]
            yield
        tail = cext_ref[off + tt - hist:off + tt, :]
        cext_ref[off - hist:off, :] = tail
        conv_ref[0] = tail
        cext_ref[off:off + tt, :] = acc * jax.nn.sigmoid(acc)
        yield
        a_x = -jnp.exp(alog_ref[...])
        dtb, dskip, ng = dtb_ref[...], dskip_ref[...], cng_ref[...]
        for c in range(tt // SPAT):
            rows = slice(c * SPAT, (c + 1) * SPAT)
            erows = slice(off + c * SPAT, off + (c + 1) * SPAT)
            dt_p = _bf16_pieces(gate_ref[0, rows, :], GATE_PIECES)
            yield
            dt = jax.nn.softplus(_dot_pieces(dt_p, ex_dt) + dtb)
            y = yield from _ssd_chunk(cext_ref[erows, 0:w], cext_ref[erows, w:2 * w], cext_ref[erows, 2 * w:3 * w],
                                      cin_ref[0, rows, xw:xw + w].astype(F32), dt, a_x, dskip, ng, s_ref, 0, live)
            ybuf_ref[wslot, rows, 2 * w:3 * w] = y.astype(BF16)
            yield

    def gmlp():
        u = jax.nn.gelu(din_ref[0, :, 0:w].astype(F32))
        vr = jax.nn.gelu(din_ref[0, :, w:2 * w].astype(F32))
        yield
        xc = vr - jnp.mean(vr, axis=-1, keepdims=True)
        vn = (xc * lax.rsqrt(jnp.mean(xc * xc, axis=-1, keepdims=True) + EPS) * dng_ref[...]).astype(BF16)
        rows = _iota((N_HEADS * SPAT, SPAT), 0) % SPAT
        wst = jnp.where(rows >= _iota((N_HEADS * SPAT, SPAT), 1), dws_ref[...], 0.0).astype(BF16)
        wcat = jnp.concatenate([wst[g * SPAT:(g + 1) * SPAT] for g in range(N_HEADS)], axis=1)
        lmask = _head_lane_masks(SPAT, w)
        zero = jnp.zeros((SPAT, w), BF16)
        yield
        for i in range(tt // SPAT):
            vi = vn[i * SPAT:(i + 1) * SPAT]
            v_groups = jnp.concatenate([jnp.where(lmask[g], vi, zero) for g in range(N_HEADS)], axis=0)
            yield
            fgate = dbx_ref[...] + jnp.dot(wcat, v_groups, preferred_element_type=F32)
            yield
            ybuf_ref[wslot, i * SPAT:(i + 1) * SPAT, 3 * w:4 * w] = (u[i * SPAT:(i + 1) * SPAT] * fgate).astype(BF16)
            yield

    def post():
        hist = CONV_F - 1
        mod = mod_ref[0]
        mix = jnp.dot(ybuf_ref[rslot], wo_ref[...], preferred_element_type=F32)
        yield
        x1 = x_ref[0] + mod[:, 2 * d:3 * d] * mix
        h2 = _rmsnorm_rows(x1, gffn_ref[...]) * (1.0 + mod[:, 4 * d:5 * d]) + mod[:, 3 * d:4 * d]
        hb = h2.astype(BF16)
        yield
        down = None
        act_b16, act_cols = None, None
        for ci, c0 in enumerate(range(0, f, fchunk)):
            cols = slice(c0, c0 + fchunk)
            g = jnp.dot(hb, wg_ref[:, cols], preferred_element_type=F32)
            u = jnp.dot(hb, wu_ref[:, cols], preferred_element_type=F32)
            if act_b16 is not None:
                p = jnp.dot(act_b16, wd_ref[act_cols, :], preferred_element_type=F32)
                down = p if down is None else down + p
            yield
            buf = ci % 2
            fext_ref[buf, off - hist:off, :] = fhist_ref[0, off - hist:off, cols]
            fext_ref[buf, off:off + tt, :] = g
            acc = fcb_ref[:, cols] + fcw_ref[CONV_F - 1:CONV_F, cols] * g
            for j in range(CONV_F - 1):
                sh = CONV_F - 1 - j
                acc = acc + fcw_ref[j:j + 1, cols] * fext_ref[buf, off - sh:off - sh + tt, :]
            fhist_ref[0, off - hist:off, cols] = g[tt - hist:tt]
            act_b16, act_cols = ((acc * jax.nn.sigmoid(acc)) * u).astype(BF16), cols
            yield
        down = down + jnp.dot(act_b16, wd_ref[act_cols, :], preferred_element_type=F32)
        fc_ref[...] = fhist_ref[:, off - hist:off, :]
        x2 = x1 + mod[:, 5 * d:6 * d] * down
        if final:
            x2 = _rmsnorm_rows(x2, gfin_ref[...])
        o_ref[0] = x2

    _interleave([post(), mlstm(), ssd(), attention(), gmlp()])

    @pl.when(mj == nt - 1)
    def _():
        cout_ref[0] = _blockdiag_to_heads(c_ref[0])


def _mix_post(qkv_a, qkvo_b, c_in, d_in, gates, x, mod, p, g_final, tt, final):
    bsz, t, d = x.shape
    w = d // 4
    nt = t // tt
    n_tiles = bsz * nt
    f = p["f_w_gate"].shape[1]
    xw = c_in.shape[2] - w
    ng, n_c, gp = p["ssm_shape"]
    band = (N_BAND + 1) * CHUNK
    fchunk = 2 * LANE
    assert f % fchunk == 0 and tt % SPAT == 0 and t % tt == 0
    mix_tile = lambda width: pl.BlockSpec(
        (1, tt, width), lambda s: (jnp.minimum(s, n_tiles - 1) // nt, jnp.minimum(s, n_tiles - 1) % nt, 0))
    post_tile = lambda width: pl.BlockSpec(
        (1, tt, width), lambda s: (jnp.maximum(s - 1, 0) // nt, jnp.maximum(s - 1, 0) % nt, 0))
    mix_state = lambda shape: pl.BlockSpec(
        (1,) + shape, lambda s: (jnp.minimum(s, n_tiles - 1) // nt,) + (0,) * len(shape))
    post_state = lambda shape: pl.BlockSpec(
        (1,) + shape, lambda s: (jnp.maximum(s - 1, 0) // nt,) + (0,) * len(shape))
    consts = [p["bias_tab"].reshape(N_HEADS * CHUNK, band), p["bi"], p["bf"], p["b_norm_g"], p["c_conv_w"],
              p["c_conv_b"], p["dtb"], p["a_log_x"], p["dskip"], p["c_norm_g"], p["d_norm_g"], p["d_w_s"], p["d_b_x"],
              p["g_ffn"].reshape(1, d), p["w_out"], p["f_w_gate"], p["f_w_up"], p["f_w_down"], p["f_conv_w"],
              p["f_conv_b"], g_final.reshape(1, d)]
    in_widths = (qkv_a.shape[2], qkvo_b.shape[2], c_in.shape[2], d_in.shape[2], gates.shape[2])
    n_hist = A_WIN // tt
    assert n_hist * tt == A_WIN

    def hist_tile(back, col):
        def index(s):
            m = jnp.minimum(s, n_tiles - 1)
            return (m // nt, jnp.maximum(m % nt - back, 0), col)
        return pl.BlockSpec((1, tt, w), index)

    hist_specs = [hist_tile(back, col) for back in range(n_hist, 0, -1) for col in (1, 2)]
    in_bytes = sum(a.shape[2] * a.dtype.itemsize for a in (qkv_a, qkvo_b, c_in, d_in, gates))
    est = ((d * d + 3 * d * f) * 2 + 2 * tt * in_bytes + 4 * tt * d * 4 + (tt + SUBLANE) * (2 * fchunk + xw) * 4
           + 2 * (A_WIN + tt) * w * 2 + 8 * tt * w * 2 + 12 * tt * d * 4 + 6 * w * w * 4
           + 4 * n_hist * tt * w * qkv_a.dtype.itemsize)
    return pl.pallas_call(
        functools.partial(_mixpost_kernel, nt=nt, final=final, fchunk=fchunk, n_hist=n_hist),
        out_shape=[jax.ShapeDtypeStruct((bsz, t, d), F32),
                   jax.ShapeDtypeStruct((bsz, w, w // N_HEADS), F32),
                   jax.ShapeDtypeStruct((bsz, 1, w), F32),
                   jax.ShapeDtypeStruct((bsz, 1, w), F32),
                   jax.ShapeDtypeStruct((bsz, CONV_C - 1, xw), F32),
                   jax.ShapeDtypeStruct((bsz, ng, n_c, gp), F32),
                   jax.ShapeDtypeStruct((bsz, CONV_F - 1, f), F32)],
        grid=(n_tiles + 1,),
        in_specs=[mix_tile(wd) for wd in in_widths] + [post_tile(d), post_state((1, mod.shape[2]))] + hist_specs
                 + [_const_spec(c.shape) for c in consts],
        out_specs=[post_tile(d), mix_state((w, w // N_HEADS)), mix_state((1, w)), mix_state((1, w)),
                   mix_state((CONV_C - 1, xw)), mix_state((ng, n_c, gp)), post_state((CONV_F - 1, f))],
        scratch_shapes=[pltpu.VMEM((A_WIN + tt, w), BF16), pltpu.VMEM((A_WIN + tt, w), BF16),
                        pltpu.VMEM((2, tt, d), BF16), pltpu.VMEM((tt + SUBLANE, xw), F32),
                        pltpu.VMEM((2, tt + SUBLANE, fchunk), F32), pltpu.VMEM((1, SUBLANE, f), F32),
                        pltpu.VMEM((1, w, w), F32)],
        name="mix_post",
        compiler_params=_params(("arbitrary",), est),
    )(qkv_a, qkvo_b, c_in, d_in, gates, x, mod, *([qkv_a] * len(hist_specs)), *consts)


def _layer(x, mod, st, p, tiles, final, g_final):
    bsz, t, d = x.shape
    gw = d // 4
    dh = gw // N_HEADS
    bb, tt, tmix, nb = tiles
    mod3 = mod.reshape(bsz, 1, mod.shape[1])
    dtypes = (BF16,) * 4 + (F32,) if st is None else (F32,) * 5
    qkv_a, qkvo_b, c_in, d_in, gates = _in_proj(x, mod3, p["g_mix"], p["w_in"], p["widths"], dtypes, bb, tt)

    if st is None:
        x_new, c_new, n_new, m_new, conv_new, ssm_new, fconv_new = _mix_post(
            qkv_a, qkvo_b, c_in, d_in, gates, x, mod3, p, g_final, tmix, final)
        vn = []
    else:
        y_a = _attention_cached(qkv_a, st["a_kt"], st["a_vt"], st["layer"], p["bias_tab"])
        y_b, c_new, n_new, m_new = _mlstm(qkvo_b, gates, p["bi"], p["bf"], p["b_norm_g"], st["b_c"], st["b_n"],
                                          st["b_m"], nb, tmix)
        y_c, conv_new, ssm_new = _ssd(c_in, gates, p["c_conv_w"], p["c_conv_b"], p["dtb"], p["a_log_x"], p["dskip"],
                                      p["c_norm_g"], st["c_conv"], st["c_ssm"], nb, tmix)
        y_d, *vn = _gmlp(d_in, p["d_norm_g"], p["d_w_s"], p["d_b_x"], tmix, True)
        x_new, fconv_new = _post(x, (y_a, y_b, y_c, y_d), mod3, p["g_ffn"], p["w_out"], p["f_w_gate"], p["f_w_up"],
                                 p["f_w_down"], p["f_conv_w"], p["f_conv_b"], st["f_conv"], g_final, bb, tt, final)

    keep = min(A_WIN, t)
    new_k = qkv_a[:, t - keep:, gw:2 * gw].reshape(bsz, keep, N_HEADS, dh).astype(F32)
    new_v = qkv_a[:, t - keep:, 2 * gw:3 * gw].reshape(bsz, keep, N_HEADS, dh).astype(F32)
    c_heads = c_new.reshape(bsz, N_HEADS, dh, dh)
    n_heads = n_new.reshape(bsz, N_HEADS, dh)
    m_heads = m_new[:, 0, ::dh]
    ssm = jnp.swapaxes(ssm_new, 2, 3).reshape(bsz, N_HEADS, gw // N_HEADS, ssm_new.shape[2])
    outs = (new_k, new_v, c_heads, n_heads, m_heads, ssm, conv_new, fconv_new, *vn)
    return x_new, outs


def _prep_layer(l, w_in, w_out, a_rel_bias, b_i_bias, b_f_bias, b_norm_g, c_conv_w, c_conv_b, c_dt_bias, c_a_log,
                c_d_skip, c_norm_g, d_norm_g, d_w_s, d_b_s, f_w_gate, f_w_up, f_conv_w, f_conv_b, f_w_down,
                g_norm_mix, g_norm_ffn):
    d = w_in.shape[1]
    gw = d // 4
    nh = b_i_bias.shape[1]
    xbc_w = c_conv_w.shape[2]
    sizes = (gw,) * 7 + (nh, nh, gw, xbc_w, c_dt_bias.shape[1], gw, gw)
    offs = [0]
    for s in sizes:
        offs.append(offs[-1] + s)
    moves, dst = [], 0
    for i in (0, 1, 2, 3, 4, 5, 6, 10, 9, 12, 13):
        moves.append((offs[i], dst, sizes[i]))
        dst += sizes[i]
    assert sizes[7] == sizes[8] == sizes[11] == N_HEADS
    packed = (dst, ((offs[7], GATE_I, N_HEADS), (offs[8], GATE_F, N_HEADS), (offs[11], GATE_DT, N_HEADS)))
    widths = (3 * gw, 4 * gw, xbc_w + gw, 2 * gw, GATE_W)
    dst += GATE_W
    assert dst == sum(widths)
    heads_x = lambda v: jnp.repeat(v.astype(F32), gw // nh)[None, :]
    return dict(
        w_in=_regroup_cast(w_in, l, moves, packed, dst), widths=widths, g_mix=g_norm_mix[l], g_ffn=g_norm_ffn[l],
        w_out=w_out[l].astype(BF16), bias_tab=_bias_table(a_rel_bias[l]),
        bi=heads_x(b_i_bias[l]), bf=heads_x(b_f_bias[l]), b_norm_g=b_norm_g[l][None, :],
        c_conv_w=c_conv_w[l], c_conv_b=c_conv_b[l][None, :], dtb=heads_x(c_dt_bias[l]),
        a_log_x=heads_x(c_a_log[l]), dskip=heads_x(c_d_skip[l]), c_norm_g=c_norm_g[l][None, :],
        d_norm_g=d_norm_g[l][None, :], d_w_s=d_w_s[l].reshape(-1, d_w_s.shape[-1]),
        d_b_x=jnp.repeat(d_b_s[l].T, gw // d_b_s.shape[1], axis=1),
        f_w_gate=f_w_gate[l].astype(BF16), f_w_up=f_w_up[l].astype(BF16), f_w_down=f_w_down[l].astype(BF16),
        f_conv_w=f_conv_w[l], f_conv_b=f_conv_b[l][None, :],
    )


def kernel(x_prompt, x_sample, c_prompt, c_sample, cache_a_k, cache_a_v, state_b_c, state_b_n, state_b_m, state_c_ssm, state_c_conv, state_ffn_conv, w_ada, b_ada, g_norm_mix, g_norm_ffn, w_in, w_out, a_rel_bias, b_i_bias, b_f_bias, b_norm_g, c_conv_w, c_conv_b, c_dt_bias, c_a_log, c_d_skip, c_norm_g, d_norm_g, d_w_s, d_b_s, f_w_gate, f_w_up, f_conv_w, f_conv_b, f_w_down, g_final):
    depth = w_in.shape[0]
    bp, tp, d = x_prompt.shape
    bs, ts, _ = x_sample.shape
    gw = d // 4
    dh = gw // N_HEADS
    xbc_w = c_conv_w.shape[2]
    n_c = state_c_ssm.shape[-1]
    g_c = (xbc_w - gw) // 2 // n_c

    mod_all = _ada(jnp.concatenate([c_prompt, c_sample], axis=0), w_ada, b_ada)

    nb = max(n for n in (8, 4, 2, 1) if bs % n == 0)
    tiles_p = (1, min(A_WIN, tp), min(A_WIN, tp), 1)
    tiles_s = (bs, ts, ts, nb)

    kt_all = jnp.transpose(cache_a_k, (0, 1, 3, 4, 2)).reshape(depth, bs, gw, -1)
    vt_all = jnp.transpose(cache_a_v, (0, 1, 3, 4, 2)).reshape(depth, bs, gw, -1)

    xp, xs = x_prompt, x_sample
    p_states, s_states = [], []
    for l in range(depth):
        p = _prep_layer(l, w_in, w_out, a_rel_bias, b_i_bias, b_f_bias, b_norm_g, c_conv_w, c_conv_b, c_dt_bias,
                        c_a_log, c_d_skip, c_norm_g, d_norm_g, d_w_s, d_b_s, f_w_gate, f_w_up, f_conv_w, f_conv_b,
                        f_w_down, g_norm_mix, g_norm_ffn)
        p["ssm_shape"] = (g_c, n_c, gw // g_c)
        final = l == depth - 1
        xp, sp = _layer(xp, mod_all[l, :bp], None, p, tiles_p, final, g_final)
        st_s = dict(a_kt=kt_all, a_vt=vt_all, layer=l,
                    b_c=state_b_c[l].reshape(bs, gw, dh), b_n=state_b_n[l].reshape(bs, 1, gw),
                    b_m=jnp.repeat(state_b_m[l], dh, axis=1)[:, None, :],
                    c_conv=state_c_conv[l],
                    c_ssm=jnp.swapaxes(state_c_ssm[l].reshape(bs, g_c, gw // g_c, n_c), 2, 3),
                    f_conv=state_ffn_conv[l])
        xs, ss = _layer(xs, mod_all[l, bp:], st_s, p, tiles_s, final, g_final)
        p_states.append(sp)
        s_states.append(ss)

    stack = lambda states, i: jnp.stack([s[i] for s in states])
    return (xp, xs,
            *(stack(p_states, i) for i in range(8)),
            *(stack(s_states, i) for i in range(9)))
```
